```python
import jax, jax.numpy as jnp
from jax import lax
import numpy as np

D_MODEL = 2048
BATCH = 1
SEQ = 16384
DEPTH = 2

N_MIXERS = 2
N_META = 16
BLOCK = 128
META_PAD = (-N_META) % BLOCK
RMS_EPS = 1e-6
NEG = -1e30
HG_EXPAND = 128
HG_HEADS = D_MODEL // HG_EXPAND
HG_DK = HG_EXPAND
HG_DV = D_MODEL // HG_HEADS
HG_FDIM = HG_HEADS * HG_DK
HG_IN = 2 * HG_FDIM + 2 * D_MODEL
FOX_HEAD_DIM = 128
FOX_HEADS = D_MODEL // FOX_HEAD_DIM
FOX_IN = 4 * D_MODEL + FOX_HEADS
FFN_DIM = 5504
N_HGRN = (DEPTH + 1) // 2
N_FOX = DEPTH // 2

kernel_name = "hybrid_hgrn2_fox_macaron_meta"


def rmsnorm(x, gain):
    xf = x.astype(jnp.float32)
    y = xf * lax.rsqrt(jnp.mean(xf * xf, axis=-1, keepdims=True) + RMS_EPS)
    return (y * gain.astype(jnp.float32)).astype(x.dtype)


def swiglu(x, w_gu, w_down):
    a, b = jnp.split(x @ w_gu, 2, axis=-1)
    return (jax.nn.silu(a) * b) @ w_down


def pad_front(t):
    widths = [(0, 0)] * t.ndim
    widths[1] = (META_PAD, 0)
    return jnp.pad(t, widths)


def hgrn2_chunk(S, inp):
    q, k, v, g = inp
    C = q.shape[2]
    causal = jnp.tril(jnp.ones((C, C), dtype=bool))
    b = jnp.cumsum(g, axis=2)
    diff = b[:, :, :, None, :] - b[:, :, None, :, :]
    decay = jnp.exp(jnp.where(causal[:, :, None], diff, -jnp.inf))
    attn = jnp.einsum('bhtd,bhsd,bhtsd->bhts', q, k, decay)
    o = jnp.einsum('bhts,bhsv->bhtv', attn, v) + jnp.einsum('bhtd,bhdv->bhtv', q * jnp.exp(b), S)
    b_last = b[:, :, -1:, :]
    S = jnp.exp(b_last[:, :, 0, :])[..., None] * S + jnp.einsum('bhsd,bhsv->bhdv', k * jnp.exp(b_last - b), v)
    return S, o


def hgrn2_mixer(xn, w_in, w_out, lb, onorm):
    B, L, _ = xn.shape
    q, f, i, g = jnp.split(xn @ w_in, [HG_FDIM, 2 * HG_FDIM, 2 * HG_FDIM + D_MODEL], axis=-1)
    q = jax.nn.silu(q.astype(jnp.float32))
    fg = lb + (1.0 - lb) * jax.nn.sigmoid(f.astype(jnp.float32))
    k = 1.0 - fg
    logf = jnp.log(fg)
    v = i.astype(jnp.float32)

    def to_chunks(t, hd):
        t = pad_front(t)
        n = t.shape[1] // BLOCK
        return t.reshape(B, n, BLOCK, HG_HEADS, hd).transpose(1, 0, 3, 2, 4)

    S0 = jnp.zeros((B, HG_HEADS, HG_DK, HG_DV), jnp.float32)
    _, o = lax.scan(hgrn2_chunk, S0, (to_chunks(q, HG_DK), to_chunks(k, HG_DK),
                                       to_chunks(v, HG_DV), to_chunks(logf, HG_DK)))
    Lp = L + META_PAD
    o = o.transpose(1, 0, 3, 2, 4).reshape(B, Lp, HG_HEADS, HG_DV)[:, META_PAD:]
    o = rmsnorm(o, onorm).reshape(B, L, D_MODEL)
    o = o * jax.nn.sigmoid(g.astype(jnp.float32))
    return o.astype(xn.dtype) @ w_out


def fox_mixer(xn, w_in, b_f, w_out, qnorm, knorm):
    B, L, _ = xn.shape
    H, dh = FOX_HEADS, FOX_HEAD_DIM
    q, k, v, g, f = jnp.split(xn @ w_in, [D_MODEL, 2 * D_MODEL, 3 * D_MODEL, 4 * D_MODEL], axis=-1)
    q = rmsnorm(q.reshape(B, L, H, dh), qnorm)
    k = rmsnorm(k.reshape(B, L, H, dh), knorm)
    v = v.reshape(B, L, H, dh)
    logf = jax.nn.log_sigmoid((f + b_f).astype(jnp.float32))
    q, k, v, logf = pad_front(q), pad_front(k), pad_front(v), pad_front(logf)
    Lp = L + META_PAD
    nB = Lp // BLOCK
    c = jnp.cumsum(logf, axis=1).transpose(0, 2, 1)
    qh = q.transpose(0, 2, 1, 3)
    kh = k.transpose(0, 2, 1, 3)
    vh = v.transpose(0, 2, 1, 3)
    q_blocks = qh.reshape(B, H, nB, BLOCK, dh).transpose(2, 0, 1, 3, 4)
    c_blocks = c.reshape(B, H, nB, BLOCK).transpose(2, 0, 1, 3)
    starts = jnp.arange(nB, dtype=jnp.int32) * BLOCK
    key_pos = jnp.arange(Lp, dtype=jnp.int32)
    key_real = key_pos >= META_PAD
    scale = dh ** -0.5

    def attend(args):
        qb, cb, start = args
        s = jnp.einsum('bhqd,bhkd->bhqk', qb, kh).astype(jnp.float32) * scale
        s = s + cb[..., :, None] - c[..., None, :]
        qpos = start + jnp.arange(BLOCK, dtype=jnp.int32)
        mask = (key_pos[None, :] <= qpos[:, None]) & key_real[None, :]
        p = jax.nn.softmax(jnp.where(mask, s, NEG), axis=-1)
        return jnp.einsum('bhqk,bhkd->bhqd', p.astype(vh.dtype), vh)

    o = lax.map(attend, (q_blocks, c_blocks, starts))
    o = o.transpose(1, 0, 3, 2, 4).reshape(B, Lp, D_MODEL)[:, META_PAD:]
    o = o * jax.nn.sigmoid(g.astype(jnp.float32)).astype(o.dtype)
    return o @ w_out


def setup_inputs(seed: int = 0) -> dict:
    key = jax.random.key(seed)
    ks = jax.random.split(key, 16)
    nrm = jax.random.normal
    D = D_MODEL
    return {
        "x": nrm(ks[0], (BATCH, SEQ, D), jnp.float32),
        "meta_tokens": nrm(ks[1], (N_META, D), jnp.float32),
        "norm_g": 1.0 + 0.02 * nrm(ks[2], (DEPTH, 3, D), jnp.float32),
        "ffn_w_gu": nrm(ks[3], (DEPTH, 2, D, 2 * FFN_DIM), jnp.float32) * D ** -0.5,
        "ffn_w_down": nrm(ks[4], (DEPTH, 2, FFN_DIM, D), jnp.float32) * FFN_DIM ** -0.5,
        "lb_logits": 0.5 * nrm(ks[5], (DEPTH + 1, HG_FDIM), jnp.float32),
        "hg_w_in": nrm(ks[6], (N_HGRN, D, HG_IN), jnp.float32) * D ** -0.5,
        "hg_w_out": nrm(ks[7], (N_HGRN, D, D), jnp.float32) * D ** -0.5,
        "hg_onorm": 1.0 + 0.02 * nrm(ks[8], (N_HGRN, HG_DV), jnp.float32),
        "fox_w_in": nrm(ks[9], (N_FOX, D, FOX_IN), jnp.float32) * D ** -0.5,
        "fox_b_f": 2.0 + 0.1 * nrm(ks[10], (N_FOX, FOX_HEADS), jnp.float32),
        "fox_w_out": nrm(ks[11], (N_FOX, D, D), jnp.float32) * D ** -0.5,
        "fox_qnorm": 1.0 + 0.02 * nrm(ks[12], (N_FOX, FOX_HEAD_DIM), jnp.float32),
        "fox_knorm": 1.0 + 0.02 * nrm(ks[13], (N_FOX, FOX_HEAD_DIM), jnp.float32),
    }


def reference(x, meta_tokens, norm_g, ffn_w_gu, ffn_w_down, lb_logits, hg_w_in, hg_w_out,
              hg_onorm, fox_w_in, fox_b_f, fox_w_out, fox_qnorm, fox_knorm):
    B = x.shape[0]
    lbs = jnp.cumsum(jax.nn.softmax(lb_logits.astype(jnp.float32), axis=0), axis=0)
    meta = jnp.broadcast_to(meta_tokens.astype(x.dtype)[None], (B, N_META, D_MODEL))
    h = jnp.concatenate([meta, x], axis=1)
    for layer in range(DEPTH):
        h = h + 0.5 * swiglu(rmsnorm(h, norm_g[layer, 0]), ffn_w_gu[layer, 0], ffn_w_down[layer, 0])
        hn = rmsnorm(h, norm_g[layer, 1])
        j = layer // N_MIXERS
        if layer % N_MIXERS == 0:
            mix = hgrn2_mixer(hn, hg_w_in[j], hg_w_out[j], lbs[layer], hg_onorm[j])
        else:
            mix = fox_mixer(hn, fox_w_in[j], fox_b_f[j], fox_w_out[j], fox_qnorm[j], fox_knorm[j])
        h = h + mix
        h = h + 0.5 * swiglu(rmsnorm(h, norm_g[layer, 2]), ffn_w_gu[layer, 1], ffn_w_down[layer, 1])
    return h[:, N_META:]
```

```python
import functools

import numpy as np
import jax
import jax.numpy as jnp
from jax import lax
from jax.experimental import pallas as pl
from jax.experimental.pallas import tpu as pltpu

F32 = jnp.float32
BF16 = jnp.bfloat16

D_MODEL = 2048
DEPTH = 2
N_META = 16
BLOCK = 128
META_PAD = (-N_META) % BLOCK
RMS_EPS = 1e-6
HEADS = 16
HEAD_DIM = 128
FFN_DIM = 5504
NEG_BIG = -1e30

LANES = 128
ROW_TILE = 640
FFN_TILE = 512
FFN_PAD = -(-FFN_DIM // FFN_TILE) * FFN_TILE
PROJ_TILE = 512
ATT_TILE = 640
VMEM_LIMIT = 56 * 1024 * 1024

_NT = (((1,), (1,)), ((), ()))


def _params(*sem):
    return pltpu.CompilerParams(dimension_semantics=sem, vmem_limit_bytes=VMEM_LIMIT)


def _sigmoid(x):
    return 1.0 / (1.0 + jnp.exp(-x))


def _rms_rows(x):
    return x * lax.rsqrt(jnp.mean(x * x, axis=-1, keepdims=True) + RMS_EPS)


def _split3(x):
    hi = x.astype(BF16)
    r1 = x - hi.astype(F32)
    mid = r1.astype(BF16)
    lo = (r1 - mid.astype(F32)).astype(BF16)
    return hi, mid, lo


def _ffn_kernel(h_ref, g_ref, wg_ref, wu_ref, wd_ref, o_ref, xn_ref):
    @pl.when(pl.program_id(1) == 0)
    def _():
        x = h_ref[...]
        xn_ref[...] = (_rms_rows(x) * g_ref[...]).astype(BF16)
        o_ref[...] = x

    xn = xn_ref[...]
    a = jnp.dot(xn, wg_ref[...], preferred_element_type=F32)
    b = jnp.dot(xn, wu_ref[...], preferred_element_type=F32)
    act = (a * (0.5 * _sigmoid(a)) * b).astype(BF16)
    o_ref[...] += jnp.dot(act, wd_ref[...], preferred_element_type=F32)


def _ffn(hp, gain, w_gu, w_down):
    lp = hp.shape[0]
    nf = FFN_PAD // FFN_TILE
    pad = FFN_PAD - FFN_DIM
    wgu = jnp.concatenate(
        [jnp.pad(w_gu[:, :FFN_DIM].astype(BF16), ((0, 0), (0, pad))),
         jnp.pad(w_gu[:, FFN_DIM:].astype(BF16), ((0, 0), (0, pad)))], axis=1)
    wd = jnp.pad(w_down.astype(BF16), ((0, pad), (0, 0)))
    return pl.pallas_call(
        _ffn_kernel,
        grid=(lp // ROW_TILE, nf),
        in_specs=[
            pl.BlockSpec((ROW_TILE, D_MODEL), lambda i, j: (i, 0)),
            pl.BlockSpec((1, D_MODEL), lambda i, j: (0, 0)),
            pl.BlockSpec((D_MODEL, FFN_TILE), lambda i, j: (0, j)),
            pl.BlockSpec((D_MODEL, FFN_TILE), lambda i, j: (0, j + nf)),
            pl.BlockSpec((FFN_TILE, D_MODEL), lambda i, j: (j, 0)),
        ],
        out_specs=pl.BlockSpec((ROW_TILE, D_MODEL), lambda i, j: (i, 0)),
        out_shape=jax.ShapeDtypeStruct((lp, D_MODEL), F32),
        scratch_shapes=[pltpu.VMEM((ROW_TILE, D_MODEL), BF16)],
        compiler_params=_params("parallel", "arbitrary"),
        name="ffn",
    )(hp, gain.reshape(1, D_MODEL), wgu, wgu, wd)


def _out_proj_kernel(h_ref, o_ref, w_ref, out_ref):
    out_ref[...] = h_ref[...] + jnp.dot(o_ref[...], w_ref[...], preferred_element_type=F32)


def _out_proj(hp, og, w_out):
    lp = hp.shape[0]
    return pl.pallas_call(
        _out_proj_kernel,
        grid=(lp // ROW_TILE,),
        in_specs=[
            pl.BlockSpec((ROW_TILE, D_MODEL), lambda i: (i, 0)),
            pl.BlockSpec((ROW_TILE, D_MODEL), lambda i: (i, 0)),
            pl.BlockSpec((D_MODEL, D_MODEL), lambda i: (0, 0)),
        ],
        out_specs=pl.BlockSpec((ROW_TILE, D_MODEL), lambda i: (i, 0)),
        out_shape=jax.ShapeDtypeStruct((lp, D_MODEL), F32),
        compiler_params=_params("parallel"),
        name="out_proj",
    )(hp, og, w_out.astype(BF16))


def _hgrn_in_kernel(h_ref, g_ref, wq_ref, wf_ref, wi_ref, wg_ref, lbl_ref,
                    q_ref, k_ref, lf_ref, v_ref, sg_ref, xn_ref, *, layer):
    @pl.when(pl.program_id(1) == 0)
    def _():
        xn_ref[...] = (_rms_rows(h_ref[...]) * g_ref[...]).astype(BF16)

    xn = xn_ref[...]
    qp = jnp.dot(xn, wq_ref[...], preferred_element_type=F32)
    fp = jnp.dot(xn, wf_ref[...], preferred_element_type=F32)
    ip = jnp.dot(xn, wi_ref[...], preferred_element_type=F32)
    gp = jnp.dot(xn, wg_ref[...], preferred_element_type=F32)
    lg = lbl_ref[...]
    e = jnp.exp(lg - jnp.max(lg, axis=0, keepdims=True))
    lb = jnp.sum(e[:layer + 1], axis=0, keepdims=True) / jnp.sum(e, axis=0, keepdims=True)
    q = qp * _sigmoid(qp)
    fg = lb + (1.0 - lb) * _sigmoid(fp)
    k = 1.0 - fg
    lf = jnp.log(fg)
    sg = _sigmoid(gp)
    for hh in range(PROJ_TILE // HEAD_DIM):
        sl = slice(hh * HEAD_DIM, (hh + 1) * HEAD_DIM)
        q_ref[hh] = q[:, sl].astype(BF16)
        k_ref[hh] = k[:, sl].astype(BF16)
        lf_ref[hh] = lf[:, sl]
        v_ref[hh] = ip[:, sl].astype(BF16)
        sg_ref[hh] = sg[:, sl].astype(BF16)


def _hgrn_in(hp, gain, w_in, lb_logits, layer):
    lp = hp.shape[0]
    nc = D_MODEL // PROJ_TILE
    hpt = PROJ_TILE // HEAD_DIM
    w = w_in.astype(BF16)
    wspec = lambda s: pl.BlockSpec((D_MODEL, PROJ_TILE), lambda i, c, s=s: (0, s * nc + c))
    hm_spec = pl.BlockSpec((hpt, ROW_TILE, HEAD_DIM), lambda i, c: (c, i, 0))
    hm = lambda dt: jax.ShapeDtypeStruct((HEADS, lp, HEAD_DIM), dt)
    return pl.pallas_call(
        functools.partial(_hgrn_in_kernel, layer=layer),
        grid=(lp // ROW_TILE, nc),
        in_specs=[
            pl.BlockSpec((ROW_TILE, D_MODEL), lambda i, c: (i, 0)),
            pl.BlockSpec((1, D_MODEL), lambda i, c: (0, 0)),
            wspec(0), wspec(1), wspec(2), wspec(3),
            pl.BlockSpec((DEPTH + 1, PROJ_TILE), lambda i, c: (0, c)),
        ],
        out_specs=[hm_spec] * 5,
        out_shape=[hm(BF16), hm(BF16), hm(F32), hm(BF16), hm(BF16)],
        scratch_shapes=[pltpu.VMEM((ROW_TILE, D_MODEL), BF16)],
        compiler_params=_params("parallel", "arbitrary"),
        name="hgrn_in",
    )(hp, gain.reshape(1, D_MODEL), w, w, w, w, lb_logits)


_LEVELS = (64, 32, 16, 8, 4, 2)


def _hgrn_consts():
    t = np.arange(BLOCK)[:, None]
    j = np.arange(BLOCK)[None, :]
    blocks = [j <= t]
    for b in _LEVELS:
        ref = 2 * b * (t // (2 * b)) + b - 1
        odd = (t // b) % 2 == 1
        blocks.append(np.where(odd, (j > ref) & (j <= t), (j > t) & (j <= ref)))
    g_all = np.concatenate(blocks, axis=0).astype(np.float32)
    masks = []
    for b in _LEVELS + (1,):
        masks.append((t // (2 * b) == j // (2 * b)) & ((t // b) % 2 == 1) & ((j // b) % 2 == 0))
    masks.append(t == j)
    return jnp.asarray(g_all, BF16), jnp.asarray(np.stack(masks).astype(np.float32))


def _hgrn_rec_kernel(q_ref, k_ref, lf_ref, v_ref, sg_ref, gmat_ref, mask_ref, on_ref,
                     o_ref, st_ref, ob_ref):
    @pl.when(pl.program_id(0) == 0)
    def _():
        st_ref[...] = jnp.zeros_like(st_ref)

    nlev = len(_LEVELS)

    def head(h, carry):
        lf = lf_ref[h]
        q = q_ref[h].astype(F32)
        k = k_ref[h].astype(F32)
        lf3 = jnp.concatenate(_split3(lf), axis=1)
        x3 = jnp.dot(gmat_ref[...], lf3, preferred_element_type=F32)
        x = x3[:, :HEAD_DIM] + x3[:, HEAD_DIM:2 * HEAD_DIM] + x3[:, 2 * HEAD_DIM:]
        ex = jnp.exp(x)
        b = x[:BLOCK]
        b_last = b[BLOCK - 1:BLOCK]
        att = jnp.zeros((BLOCK, BLOCK), F32)
        for lv in range(nlev + 2):
            if lv < nlev:
                el = ex[(lv + 1) * BLOCK:(lv + 2) * BLOCK]
                qt, kt = q * el, k * el
            elif lv == nlev:
                qt, kt = q * jnp.exp(lf), k
            else:
                qt, kt = q, k
            a = lax.dot_general(qt.astype(BF16), kt.astype(BF16), _NT, preferred_element_type=F32)
            att = att + mask_ref[lv] * a
        qb = (q * ex[:BLOCK]).astype(BF16)
        kd = (k * jnp.exp(b_last - b)).astype(BF16)
        vt = v_ref[h].astype(F32).T.astype(BF16)
        st = st_ref[h]
        lhs = jnp.concatenate([att.astype(BF16), qb], axis=1)
        rhs = jnp.concatenate([vt, st.astype(BF16)], axis=1)
        o = lax.dot_general(lhs, rhs, _NT, preferred_element_type=F32)
        st_ref[h] = st * jnp.exp(b_last) + jnp.dot(vt, kd, preferred_element_type=F32)
        og = _rms_rows(o) * on_ref[...] * sg_ref[h].astype(F32)
        ob_ref[h] = og.astype(BF16)
        return carry

    lax.fori_loop(0, HEADS, head, 0)
    for h in range(HEADS):
        o_ref[:, h * HEAD_DIM:(h + 1) * HEAD_DIM] = ob_ref[h]


def _hgrn_rec(q, k, lf, v, sg, onorm):
    lp = q.shape[1]
    gmat, masks = _hgrn_consts()
    hm_spec = pl.BlockSpec((HEADS, BLOCK, HEAD_DIM), lambda c: (0, c, 0))
    return pl.pallas_call(
        _hgrn_rec_kernel,
        grid=(lp // BLOCK,),
        in_specs=[hm_spec] * 5 + [
            pl.BlockSpec(gmat.shape, lambda c: (0, 0)),
            pl.BlockSpec(masks.shape, lambda c: (0, 0, 0)),
            pl.BlockSpec((1, HEAD_DIM), lambda c: (0, 0)),
        ],
        out_specs=pl.BlockSpec((BLOCK, D_MODEL), lambda c: (c, 0)),
        out_shape=jax.ShapeDtypeStruct((lp, D_MODEL), BF16),
        scratch_shapes=[pltpu.VMEM((HEADS, HEAD_DIM, HEAD_DIM), F32),
                        pltpu.VMEM((HEADS, BLOCK, HEAD_DIM), BF16)],
        compiler_params=_params("arbitrary"),
        name="hgrn_rec",
    )(q, k, lf, v, sg, gmat, masks, onorm.reshape(1, HEAD_DIM))


def _fox_in_kernel(h_ref, g_ref, wq_ref, wk_ref, wv_ref, wg_ref, wf_ref, bf_ref, qn_ref, kn_ref,
                   q_ref, k_ref, v_ref, sg_ref, lf_ref, xn_ref):
    @pl.when(pl.program_id(1) == 0)
    def _():
        xn = (_rms_rows(h_ref[...]) * g_ref[...]).astype(BF16)
        xn_ref[...] = xn
        z = jnp.dot(xn, wf_ref[...], preferred_element_type=F32) + bf_ref[...]
        lf_ref[...] = jnp.minimum(z, 0.0) - jnp.log(1.0 + jnp.exp(-jnp.abs(z)))

    xn = xn_ref[...]
    qp = jnp.dot(xn, wq_ref[...], preferred_element_type=F32)
    kp = jnp.dot(xn, wk_ref[...], preferred_element_type=F32)
    vp = jnp.dot(xn, wv_ref[...], preferred_element_type=F32)
    gp = jnp.dot(xn, wg_ref[...], preferred_element_type=F32)
    sg = _sigmoid(gp)
    scale = HEAD_DIM ** -0.5
    for hh in range(PROJ_TILE // HEAD_DIM):
        sl = slice(hh * HEAD_DIM, (hh + 1) * HEAD_DIM)
        q_ref[hh] = (_rms_rows(qp[:, sl]) * (qn_ref[...] * scale)).astype(BF16)
        k_ref[hh] = (_rms_rows(kp[:, sl]) * kn_ref[...]).astype(BF16)
        v_ref[hh] = vp[:, sl].astype(BF16)
        sg_ref[hh] = sg[:, sl].astype(BF16)


def _fox_in(hp, gain, w_in, b_f, qnorm, knorm):
    lp = hp.shape[0]
    nc = D_MODEL // PROJ_TILE
    hpt = PROJ_TILE // HEAD_DIM
    w = w_in[:, :4 * D_MODEL].astype(BF16)
    wf = w_in[:, 4 * D_MODEL:].astype(BF16)
    wspec = lambda s: pl.BlockSpec((D_MODEL, PROJ_TILE), lambda i, c, s=s: (0, s * nc + c))
    hm_spec = pl.BlockSpec((hpt, ROW_TILE, HEAD_DIM), lambda i, c: (c, i, 0))
    hm = jax.ShapeDtypeStruct((HEADS, lp, HEAD_DIM), BF16)
    vec = lambda n: pl.BlockSpec((1, n), lambda i, c: (0, 0))
    return pl.pallas_call(
        _fox_in_kernel,
        grid=(lp // ROW_TILE, nc),
        in_specs=[
            pl.BlockSpec((ROW_TILE, D_MODEL), lambda i, c: (i, 0)),
            vec(D_MODEL),
            wspec(0), wspec(1), wspec(2), wspec(3),
            pl.BlockSpec((D_MODEL, HEADS), lambda i, c: (0, 0)),
            vec(HEADS), vec(HEAD_DIM), vec(HEAD_DIM),
        ],
        out_specs=[hm_spec] * 4 + [pl.BlockSpec((ROW_TILE, HEADS), lambda i, c: (i, 0))],
        out_shape=[hm, hm, hm, hm, jax.ShapeDtypeStruct((lp, HEADS), F32)],
        scratch_shapes=[pltpu.VMEM((ROW_TILE, D_MODEL), BF16)],
        compiler_params=_params("parallel", "arbitrary"),
        name="fox_in",
    )(hp, gain.reshape(1, D_MODEL), w, w, w, w, wf, b_f.reshape(1, HEADS),
      qnorm.reshape(1, HEAD_DIM), knorm.reshape(1, HEAD_DIM))


def _fox_aug_kernel(lf_ref, tri_ref, aq_ref, ak_ref, carry_ref):
    i = pl.program_id(0)

    @pl.when(i == 0)
    def _():
        carry_ref[...] = jnp.zeros_like(carry_ref)

    row16 = i * BLOCK + lax.broadcasted_iota(jnp.int32, (BLOCK, HEADS), 0)
    lf = jnp.where(row16 >= META_PAD, lf_ref[...], 0.0)
    tri = tri_ref[...]
    c = carry_ref[...]
    for part in _split3(lf):
        c = c + jnp.dot(tri, part, preferred_element_type=F32)
    carry_ref[...] = c[BLOCK - 1:BLOCK]
    lane = lax.broadcasted_iota(jnp.int32, (BLOCK, LANES), 1)
    row = i * BLOCK + lax.broadcasted_iota(jnp.int32, (BLOCK, LANES), 0)
    for h in range(HEADS):
        col = jnp.broadcast_to(c[:, h:h + 1], (BLOCK, LANES))
        hi, mid, lo = (p.astype(F32) for p in _split3(col))
        aq = jnp.where(lane == 0, hi, jnp.where(lane == 1, mid, jnp.where(lane == 2, lo,
             jnp.where(lane < 6, 1.0, 0.0))))
        ak = jnp.where(lane < 3, 1.0, jnp.where(lane == 3, jnp.where(row >= META_PAD, -hi, NEG_BIG),
             jnp.where(lane == 4, -mid, jnp.where(lane == 5, -lo, 0.0))))
        aq_ref[h] = aq.astype(BF16)
        ak_ref[h] = ak.astype(BF16)


def _fox_aug(lf):
    lp = lf.shape[0]
    tri = jnp.asarray(np.tril(np.ones((BLOCK, BLOCK), np.float32)), BF16)
    hm_spec = pl.BlockSpec((HEADS, BLOCK, LANES), lambda i: (0, i, 0))
    hm = jax.ShapeDtypeStruct((HEADS, lp, LANES), BF16)
    return pl.pallas_call(
        _fox_aug_kernel,
        grid=(lp // BLOCK,),
        in_specs=[pl.BlockSpec((BLOCK, HEADS), lambda i: (i, 0)),
                  pl.BlockSpec((BLOCK, BLOCK), lambda i: (0, 0))],
        out_specs=[hm_spec, hm_spec],
        out_shape=[hm, hm],
        scratch_shapes=[pltpu.VMEM((1, HEADS), F32)],
        compiler_params=_params("arbitrary"),
        name="fox_aug",
    )(lf, tri)


def _fox_attn_kernel(q_ref, aq_ref, k_ref, ak_ref, v_ref, sg_ref, o_ref, m_ref, l_ref, acc_ref):
    i = pl.program_id(1)
    t = ATT_TILE
    qq = jnp.concatenate([q_ref[...], aq_ref[...]], axis=1)
    m_ref[...] = jnp.full_like(m_ref, -jnp.inf)
    l_ref[...] = jnp.zeros_like(l_ref)
    acc_ref[...] = jnp.zeros_like(acc_ref)

    def step(j, masked):
        ks = pl.multiple_of(j * t, t)
        kk = jnp.concatenate([k_ref[pl.ds(ks, t), :], ak_ref[pl.ds(ks, t), :]], axis=1)
        s = lax.dot_general(qq, kk, _NT, preferred_element_type=F32)
        if masked:
            r = lax.broadcasted_iota(jnp.int32, (t, t), 0)
            c = lax.broadcasted_iota(jnp.int32, (t, t), 1)
            s = jnp.where(c <= r, s, NEG_BIG)
        m_prev = m_ref[...]
        m_new = jnp.maximum(m_prev, jnp.max(s, axis=1, keepdims=True))
        alpha = jnp.exp(m_prev - m_new)
        p = jnp.exp(s - pltpu.repeat(m_new, t // LANES, axis=1))
        l_ref[...] = alpha * l_ref[...] + jnp.sum(p, axis=1, keepdims=True)
        m_ref[...] = m_new
        acc_ref[...] = alpha * acc_ref[...] + jnp.dot(
            p.astype(BF16), v_ref[pl.ds(ks, t), :], preferred_element_type=F32)

    def body(j, carry):
        step(j, False)
        return carry

    lax.fori_loop(0, i, body, 0)
    step(i, True)
    row = i * t + lax.broadcasted_iota(jnp.int32, (t, HEAD_DIM), 0)
    o = acc_ref[...] / l_ref[...] * sg_ref[...].astype(F32)
    o_ref[...] = jnp.where(row >= META_PAD, o, 0.0).astype(BF16)


def _fox_attn(q, aq, k, ak, v, sg):
    lp = q.shape[1]
    t = ATT_TILE
    qspec = pl.BlockSpec((None, t, HEAD_DIM), lambda h, i: (h, i, 0))
    kspec = pl.BlockSpec((None, lp, HEAD_DIM), lambda h, i: (h, 0, 0))
    return pl.pallas_call(
        _fox_attn_kernel,
        grid=(HEADS, lp // t),
        in_specs=[qspec, qspec, kspec, kspec, kspec, qspec],
        out_specs=pl.BlockSpec((t, HEAD_DIM), lambda h, i: (i, h)),
        out_shape=jax.ShapeDtypeStruct((lp, D_MODEL), BF16),
        scratch_shapes=[pltpu.VMEM((t, LANES), F32), pltpu.VMEM((t, LANES), F32),
                        pltpu.VMEM((t, HEAD_DIM), F32)],
        compiler_params=_params("parallel", "arbitrary"),
        name="fox_attn",
    )(q, aq, k, ak, v, sg)


def kernel(x, meta_tokens, norm_g, ffn_w_gu, ffn_w_down, lb_logits, hg_w_in, hg_w_out, hg_onorm,
           fox_w_in, fox_b_f, fox_w_out, fox_qnorm, fox_knorm):
    assert x.shape[0] == 1 and x.shape[2] == D_MODEL
    seq = x.shape[1]
    real = META_PAD + N_META + seq
    lp = -(-real // ROW_TILE) * ROW_TILE
    hp = jnp.concatenate([jnp.zeros((META_PAD, D_MODEL), F32), meta_tokens.astype(F32), x[0],
                          jnp.zeros((lp - real, D_MODEL), F32)], axis=0)
    for layer in range(DEPTH):
        hp = _ffn(hp, norm_g[layer, 0], ffn_w_gu[layer, 0], ffn_w_down[layer, 0])
        j = layer // 2
        if layer % 2 == 0:
            q, k, lf, v, sg = _hgrn_in(hp, norm_g[layer, 1], hg_w_in[j], lb_logits, layer)
            og = _hgrn_rec(q, k, lf, v, sg, hg_onorm[j])
            hp = _out_proj(hp, og, hg_w_out[j])
        else:
            q, k, v, sg, lf = _fox_in(hp, norm_g[layer, 1], fox_w_in[j], fox_b_f[j],
                                      fox_qnorm[j], fox_knorm[j])
            aq, ak = _fox_aug(lf)
            og = _fox_attn(q, aq, k, ak, v, sg)
            hp = _out_proj(hp, og, fox_w_out[j])
        hp = _ffn(hp, norm_g[layer, 2], ffn_w_gu[layer, 1], ffn_w_down[layer, 1])
    return hp[META_PAD + N_META:META_PAD + N_META + seq][None]
```

```python
import functools

import numpy as np
import jax
import jax.numpy as jnp
from jax import lax
from jax.experimental import pallas as pl
from jax.experimental.pallas import tpu as pltpu

F32 = jnp.float32
BF16 = jnp.bfloat16

D_MODEL = 2048
DEPTH = 2
N_META = 16
BLOCK = 128
META_PAD = (-N_META) % BLOCK
RMS_EPS = 1e-6
HEADS = 16
HEAD_DIM = 128
FFN_DIM = 5504
NEG_BIG = -1e30
LOG2E = 1.4426950408889634

LANES = 128
ROW_TILE = 640
FFN_TILE = 512
FFN_PAD = -(-FFN_DIM // FFN_TILE) * FFN_TILE
PROJ_TILE = 512
ATT_Q = 1280
ATT_K = 1280
ATT_KSUB = 640
ATT_GROUP = 256
ATT_AHEAD = 3
V_ROWS = HEAD_DIM + 16
VMEM_LIMIT = 56 * 1024 * 1024

_NT = (((1,), (1,)), ((), ()))


def _params(*sem):
    return pltpu.CompilerParams(dimension_semantics=sem, vmem_limit_bytes=VMEM_LIMIT)


def _sigmoid(x):
    return 1.0 / (1.0 + jnp.exp(-x))


def _rms_rows(x):
    return x * lax.rsqrt(jnp.mean(x * x, axis=-1, keepdims=True) + RMS_EPS)


def _split3(x):
    hi = x.astype(BF16)
    r1 = x - hi.astype(F32)
    mid = r1.astype(BF16)
    lo = (r1 - mid.astype(F32)).astype(BF16)
    return hi, mid, lo


def _ffn_kernel(h_ref, g_ref, wg_ref, wu_ref, wd_ref, o_ref, xn_ref):
    @pl.when(pl.program_id(1) == 0)
    def _():
        x = h_ref[...]
        xn_ref[...] = (_rms_rows(x) * g_ref[...]).astype(BF16)
        o_ref[...] = x

    xn = xn_ref[...]
    a = jnp.dot(xn, wg_ref[...], preferred_element_type=F32)
    b = jnp.dot(xn, wu_ref[...], preferred_element_type=F32)
    act = (a * (0.5 * _sigmoid(a)) * b).astype(BF16)
    o_ref[...] += jnp.dot(act, wd_ref[...], preferred_element_type=F32)


def _ffn(hp, gain, w_gu, w_down):
    lp = hp.shape[0]
    nf = FFN_PAD // FFN_TILE
    pad = FFN_PAD - FFN_DIM
    wgu = jnp.concatenate(
        [jnp.pad(w_gu[:, :FFN_DIM].astype(BF16), ((0, 0), (0, pad))),
         jnp.pad(w_gu[:, FFN_DIM:].astype(BF16), ((0, 0), (0, pad)))], axis=1)
    wd = jnp.pad(w_down.astype(BF16), ((0, pad), (0, 0)))
    return pl.pallas_call(
        _ffn_kernel,
        grid=(lp // ROW_TILE, nf),
        in_specs=[
            pl.BlockSpec((ROW_TILE, D_MODEL), lambda i, j: (i, 0)),
            pl.BlockSpec((1, D_MODEL), lambda i, j: (0, 0)),
            pl.BlockSpec((D_MODEL, FFN_TILE), lambda i, j: (0, j)),
            pl.BlockSpec((D_MODEL, FFN_TILE), lambda i, j: (0, j + nf)),
            pl.BlockSpec((FFN_TILE, D_MODEL), lambda i, j: (j, 0)),
        ],
        out_specs=pl.BlockSpec((ROW_TILE, D_MODEL), lambda i, j: (i, 0)),
        out_shape=jax.ShapeDtypeStruct((lp, D_MODEL), F32),
        scratch_shapes=[pltpu.VMEM((ROW_TILE, D_MODEL), BF16)],
        compiler_params=_params("parallel", "arbitrary"),
        name="ffn",
    )(hp, gain.reshape(1, D_MODEL), wgu, wgu, wd)


def _out_proj_kernel(h_ref, o_ref, w_ref, out_ref):
    out_ref[...] = h_ref[...] + jnp.dot(o_ref[...], w_ref[...], preferred_element_type=F32)


def _out_proj(hp, og, w_out):
    lp = hp.shape[0]
    return pl.pallas_call(
        _out_proj_kernel,
        grid=(lp // ROW_TILE,),
        in_specs=[
            pl.BlockSpec((ROW_TILE, D_MODEL), lambda i: (i, 0)),
            pl.BlockSpec((ROW_TILE, D_MODEL), lambda i: (i, 0)),
            pl.BlockSpec((D_MODEL, D_MODEL), lambda i: (0, 0)),
        ],
        out_specs=pl.BlockSpec((ROW_TILE, D_MODEL), lambda i: (i, 0)),
        out_shape=jax.ShapeDtypeStruct((lp, D_MODEL), F32),
        compiler_params=_params("parallel"),
        name="out_proj",
    )(hp, og, w_out.astype(BF16))


def _hgrn_in_kernel(h_ref, g_ref, wq_ref, wf_ref, wi_ref, wg_ref, lbl_ref,
                    q_ref, k_ref, lf_ref, v_ref, sg_ref, xn_ref, *, layer):
    @pl.when(pl.program_id(1) == 0)
    def _():
        xn_ref[...] = (_rms_rows(h_ref[...]) * g_ref[...]).astype(BF16)

    xn = xn_ref[...]
    qp = jnp.dot(xn, wq_ref[...], preferred_element_type=F32)
    fp = jnp.dot(xn, wf_ref[...], preferred_element_type=F32)
    ip = jnp.dot(xn, wi_ref[...], preferred_element_type=F32)
    gp = jnp.dot(xn, wg_ref[...], preferred_element_type=F32)
    lg = lbl_ref[...]
    e = jnp.exp(lg - jnp.max(lg, axis=0, keepdims=True))
    lb = jnp.sum(e[:layer + 1], axis=0, keepdims=True) / jnp.sum(e, axis=0, keepdims=True)
    q = qp * _sigmoid(qp)
    fg = lb + (1.0 - lb) * _sigmoid(fp)
    k = 1.0 - fg
    lf = jnp.log(fg)
    sg = _sigmoid(gp)
    for hh in range(PROJ_TILE // HEAD_DIM):
        sl = slice(hh * HEAD_DIM, (hh + 1) * HEAD_DIM)
        q_ref[hh] = q[:, sl].astype(BF16)
        k_ref[hh] = k[:, sl].astype(BF16)
        lf_ref[hh] = lf[:, sl]
        v_ref[hh] = ip[:, sl].astype(BF16)
        sg_ref[hh] = sg[:, sl].astype(BF16)


def _hgrn_in(hp, gain, w_in, lb_logits, layer):
    lp = hp.shape[0]
    nc = D_MODEL // PROJ_TILE
    hpt = PROJ_TILE // HEAD_DIM
    w = w_in.astype(BF16)
    wspec = lambda s: pl.BlockSpec((D_MODEL, PROJ_TILE), lambda i, c, s=s: (0, s * nc + c))
    hm_spec = pl.BlockSpec((hpt, ROW_TILE, HEAD_DIM), lambda i, c: (c, i, 0))
    hm = lambda dt: jax.ShapeDtypeStruct((HEADS, lp, HEAD_DIM), dt)
    return pl.pallas_call(
        functools.partial(_hgrn_in_kernel, layer=layer),
        grid=(lp // ROW_TILE, nc),
        in_specs=[
            pl.BlockSpec((ROW_TILE, D_MODEL), lambda i, c: (i, 0)),
            pl.BlockSpec((1, D_MODEL), lambda i, c: (0, 0)),
            wspec(0), wspec(1), wspec(2), wspec(3),
            pl.BlockSpec((DEPTH + 1, PROJ_TILE), lambda i, c: (0, c)),
        ],
        out_specs=[hm_spec] * 5,
        out_shape=[hm(BF16), hm(BF16), hm(F32), hm(BF16), hm(BF16)],
        scratch_shapes=[pltpu.VMEM((ROW_TILE, D_MODEL), BF16)],
        compiler_params=_params("parallel", "arbitrary"),
        name="hgrn_in",
    )(hp, gain.reshape(1, D_MODEL), w, w, w, w, lb_logits)


_ROW_LEVELS = (64, 32, 16, 8)
_MM_LEVELS = (4, 2)
HGRN_UNROLL = 4


def _hgrn_consts():
    t = np.arange(BLOCK)[:, None]
    j = np.arange(BLOCK)[None, :]
    blocks = [j <= t]
    for b in _MM_LEVELS:
        ref = 2 * b * (t // (2 * b)) + b - 1
        odd = (t // b) % 2 == 1
        blocks.append(np.where(odd, (j > ref) & (j <= t), (j > t) & (j <= ref)))
    g_all = np.concatenate(blocks, axis=0).astype(np.float32)
    masks = []
    for b in _ROW_LEVELS + _MM_LEVELS + (1,):
        masks.append((t // (2 * b) == j // (2 * b)) & ((t // b) % 2 == 1) & ((j // b) % 2 == 0))
    masks.append(t == j)
    return jnp.asarray(g_all, BF16), jnp.asarray(np.stack(masks).astype(np.float32))


def _hgrn_rec_kernel(q_ref, k_ref, lf_ref, v_ref, sg_ref, gmat_ref, mask_ref, on_ref,
                     o_ref, st_ref, ob_ref):
    @pl.when(pl.program_id(0) == 0)
    def _():
        st_ref[...] = jnp.zeros_like(st_ref)

    def boundary_rows(b, size):
        return jnp.concatenate(
            [jnp.broadcast_to(b[s + size - 1:s + size], (2 * size, HEAD_DIM))
             for s in range(0, BLOCK, 2 * size)], axis=0)

    def heads(hg, carry):
        hs = [hg * HGRN_UNROLL + u for u in range(HGRN_UNROLL)]
        lf = [lf_ref[h] for h in hs]
        q = [q_ref[h].astype(F32) for h in hs]
        k = [k_ref[h].astype(F32) for h in hs]
        x3 = [jnp.dot(gmat_ref[...], jnp.concatenate(_split3(l), axis=1),
                      preferred_element_type=F32) for l in lf]
        x = [t[:, :HEAD_DIM] + t[:, HEAD_DIM:2 * HEAD_DIM] + t[:, 2 * HEAD_DIM:] for t in x3]
        b = [t[:BLOCK] for t in x]
        b_last = [t[BLOCK - 1:BLOCK] for t in b]
        att = [jnp.zeros((BLOCK, BLOCK), F32) for _ in hs]
        n_row, n_mm = len(_ROW_LEVELS), len(_MM_LEVELS)
        for lv in range(n_row + n_mm + 2):
            for u in range(HGRN_UNROLL):
                if lv < n_row:
                    el = jnp.exp(-jnp.abs(b[u] - boundary_rows(b[u], _ROW_LEVELS[lv])))
                    qt, kt = q[u] * el, k[u] * el
                elif lv < n_row + n_mm:
                    m = lv - n_row + 1
                    el = jnp.exp(x[u][m * BLOCK:(m + 1) * BLOCK])
                    qt, kt = q[u] * el, k[u] * el
                elif lv == n_row + n_mm:
                    qt, kt = q[u] * jnp.exp(lf[u]), k[u]
                else:
                    qt, kt = q[u], k[u]
                a = lax.dot_general(qt.astype(BF16), kt.astype(BF16), _NT,
                                    preferred_element_type=F32)
                att[u] = att[u] + mask_ref[lv] * a
        for u, h in enumerate(hs):
            qb = (q[u] * jnp.exp(b[u])).astype(BF16)
            kd = (k[u] * jnp.exp(b_last[u] - b[u])).astype(BF16)
            vt = v_ref[h].astype(F32).T.astype(BF16)
            st = st_ref[h]
            lhs = jnp.concatenate([att[u].astype(BF16), qb], axis=1)
            rhs = jnp.concatenate([vt, st.astype(BF16)], axis=1)
            o = lax.dot_general(lhs, rhs, _NT, preferred_element_type=F32)
            st_ref[h] = st * jnp.exp(b_last[u]) + jnp.dot(vt, kd, preferred_element_type=F32)
            og = _rms_rows(o) * on_ref[...] * sg_ref[h].astype(F32)
            ob_ref[h] = og.astype(BF16)
        return carry

    lax.fori_loop(0, HEADS // HGRN_UNROLL, heads, 0)
    for h in range(HEADS):
        o_ref[:, h * HEAD_DIM:(h + 1) * HEAD_DIM] = ob_ref[h]


def _hgrn_rec(q, k, lf, v, sg, onorm):
    lp = q.shape[1]
    gmat, masks = _hgrn_consts()
    hm_spec = pl.BlockSpec((HEADS, BLOCK, HEAD_DIM), lambda c: (0, c, 0))
    return pl.pallas_call(
        _hgrn_rec_kernel,
        grid=(lp // BLOCK,),
        in_specs=[hm_spec] * 5 + [
            pl.BlockSpec(gmat.shape, lambda c: (0, 0)),
            pl.BlockSpec(masks.shape, lambda c: (0, 0, 0)),
            pl.BlockSpec((1, HEAD_DIM), lambda c: (0, 0)),
        ],
        out_specs=pl.BlockSpec((BLOCK, D_MODEL), lambda c: (c, 0)),
        out_shape=jax.ShapeDtypeStruct((lp, D_MODEL), BF16),
        scratch_shapes=[pltpu.VMEM((HEADS, HEAD_DIM, HEAD_DIM), F32),
                        pltpu.VMEM((HEADS, BLOCK, HEAD_DIM), BF16)],
        compiler_params=_params("arbitrary"),
        name="hgrn_rec",
    )(q, k, lf, v, sg, gmat, masks, onorm.reshape(1, HEAD_DIM))


def _fox_in_kernel(h_ref, g_ref, wq_ref, wk_ref, wv_ref, wg_ref, wf_ref, bf_ref, qn_ref, kn_ref,
                   q_ref, k_ref, v_ref, sg_ref, lf_ref, xn_ref):
    @pl.when(pl.program_id(1) == 0)
    def _():
        xn = (_rms_rows(h_ref[...]) * g_ref[...]).astype(BF16)
        xn_ref[...] = xn
        z = jnp.dot(xn, wf_ref[...], preferred_element_type=F32) + bf_ref[...]
        lf_ref[...] = jnp.minimum(z, 0.0) - jnp.log(1.0 + jnp.exp(-jnp.abs(z)))

    xn = xn_ref[...]
    qp = jnp.dot(xn, wq_ref[...], preferred_element_type=F32)
    kp = jnp.dot(xn, wk_ref[...], preferred_element_type=F32)
    vp = jnp.dot(xn, wv_ref[...], preferred_element_type=F32)
    gp = jnp.dot(xn, wg_ref[...], preferred_element_type=F32)
    sg = _sigmoid(gp)
    scale = HEAD_DIM ** -0.5 * LOG2E
    for hh in range(PROJ_TILE // HEAD_DIM):
        sl = slice(hh * HEAD_DIM, (hh + 1) * HEAD_DIM)
        q_ref[hh] = (_rms_rows(qp[:, sl]) * (qn_ref[...] * scale)).astype(BF16)
        k_ref[hh] = (_rms_rows(kp[:, sl]) * kn_ref[...]).astype(BF16)
        v_ref[hh] = vp[:, sl].astype(BF16)
        sg_ref[hh] = sg[:, sl].astype(BF16)


def _fox_in(hp, gain, w_in, b_f, qnorm, knorm):
    lp = hp.shape[0]
    nc = D_MODEL // PROJ_TILE
    hpt = PROJ_TILE // HEAD_DIM
    w = w_in[:, :4 * D_MODEL].astype(BF16)
    wf = w_in[:, 4 * D_MODEL:].astype(BF16)
    wspec = lambda s: pl.BlockSpec((D_MODEL, PROJ_TILE), lambda i, c, s=s: (0, s * nc + c))
    hm_spec = pl.BlockSpec((hpt, ROW_TILE, HEAD_DIM), lambda i, c: (c, i, 0))
    hm = jax.ShapeDtypeStruct((HEADS, lp, HEAD_DIM), BF16)
    vec = lambda n: pl.BlockSpec((1, n), lambda i, c: (0, 0))
    return pl.pallas_call(
        _fox_in_kernel,
        grid=(lp // ROW_TILE, nc),
        in_specs=[
            pl.BlockSpec((ROW_TILE, D_MODEL), lambda i, c: (i, 0)),
            vec(D_MODEL),
            wspec(0), wspec(1), wspec(2), wspec(3),
            pl.BlockSpec((D_MODEL, HEADS), lambda i, c: (0, 0)),
            vec(HEADS), vec(HEAD_DIM), vec(HEAD_DIM),
        ],
        out_specs=[hm_spec] * 4 + [pl.BlockSpec((ROW_TILE, HEADS), lambda i, c: (i, 0))],
        out_shape=[hm, hm, hm, hm, jax.ShapeDtypeStruct((lp, HEADS), F32)],
        scratch_shapes=[pltpu.VMEM((ROW_TILE, D_MODEL), BF16)],
        compiler_params=_params("parallel", "arbitrary"),
        name="fox_in",
    )(hp, gain.reshape(1, D_MODEL), w, w, w, w, wf, b_f.reshape(1, HEADS),
      qnorm.reshape(1, HEAD_DIM), knorm.reshape(1, HEAD_DIM))


def _fox_prep_kernel(lf_ref, tri_ref, q_ref, k_ref, v_ref, qqt_ref, kk_ref, vt_ref, carry_ref):
    i = pl.program_id(0)

    @pl.when(i == 0)
    def _():
        carry_ref[...] = jnp.zeros_like(carry_ref)

    row16 = i * BLOCK + lax.broadcasted_iota(jnp.int32, (BLOCK, HEADS), 0)
    lf = jnp.where(row16 >= META_PAD, lf_ref[...], 0.0)
    tri = tri_ref[...]
    c = carry_ref[...]
    for part in _split3(lf):
        c = c + jnp.dot(tri, part, preferred_element_type=F32)
    carry_ref[...] = c[BLOCK - 1:BLOCK]
    lane = lax.broadcasted_iota(jnp.int32, (BLOCK, LANES), 1)
    row = i * BLOCK + lax.broadcasted_iota(jnp.int32, (BLOCK, LANES), 0)
    for h in range(HEADS):
        col = jnp.broadcast_to(c[:, h:h + 1] * LOG2E, (BLOCK, LANES))
        hi, mid, lo = (p.astype(F32) for p in _split3(col))
        aq = jnp.where(lane == 0, hi, jnp.where(lane == 1, mid, jnp.where(lane == 2, lo,
             jnp.where(lane < 6, 1.0, 0.0))))
        ak = jnp.where(lane < 3, 1.0, jnp.where(lane == 3, jnp.where(row >= META_PAD, -hi, NEG_BIG),
             jnp.where(lane == 4, -mid, jnp.where(lane == 5, -lo, 0.0))))
        qqt_ref[h, :HEAD_DIM, :] = q_ref[h].astype(F32).T.astype(BF16)
        qqt_ref[h, HEAD_DIM:, :] = aq.T.astype(BF16)
        kk_ref[h, :, :HEAD_DIM] = k_ref[h]
        kk_ref[h, :, HEAD_DIM:] = ak.astype(BF16)
        vt_ref[h, :HEAD_DIM, :] = v_ref[h].astype(F32).T.astype(BF16)
        vt_ref[h, HEAD_DIM:, :] = jnp.ones((V_ROWS - HEAD_DIM, BLOCK), BF16)


def _fox_prep(lf, q, k, v):
    lp = lf.shape[0]
    kdim = HEAD_DIM + LANES
    tri = jnp.asarray(np.tril(np.ones((BLOCK, BLOCK), np.float32)), BF16)
    hm_spec = pl.BlockSpec((HEADS, BLOCK, HEAD_DIM), lambda i: (0, i, 0))
    return pl.pallas_call(
        _fox_prep_kernel,
        grid=(lp // BLOCK,),
        in_specs=[pl.BlockSpec((BLOCK, HEADS), lambda i: (i, 0)),
                  pl.BlockSpec((BLOCK, BLOCK), lambda i: (0, 0)),
                  hm_spec, hm_spec, hm_spec],
        out_specs=[pl.BlockSpec((HEADS, kdim, BLOCK), lambda i: (0, 0, i)),
                   pl.BlockSpec((HEADS, BLOCK, kdim), lambda i: (0, i, 0)),
                   pl.BlockSpec((HEADS, V_ROWS, BLOCK), lambda i: (0, 0, i))],
        out_shape=[jax.ShapeDtypeStruct((HEADS, kdim, lp), BF16),
                   jax.ShapeDtypeStruct((HEADS, lp, kdim), BF16),
                   jax.ShapeDtypeStruct((HEADS, V_ROWS, lp), BF16)],
        scratch_shapes=[pltpu.VMEM((1, HEADS), F32)],
        compiler_params=_params("arbitrary"),
        name="fox_prep",
    )(lf, tri, q, k, v)


def _fox_attn_kernel(qqt_ref, kk_ref, vt_ref, sg_ref, o_ref, m_ref, acc_ref):
    i = pl.program_id(1)
    qs = i * ATT_Q
    m_ref[...] = jnp.full_like(m_ref, -jnp.inf)
    acc_ref[...] = jnp.zeros_like(acc_ref)

    def scores(item):
        g, ks, nk, _ = item
        gs = slice(g * ATT_GROUP, (g + 1) * ATT_GROUP)
        return jnp.dot(kk_ref[pl.ds(ks, nk), :], qqt_ref[:, gs], preferred_element_type=F32)

    def update(item, s):
        g, ks, nk, masked = item
        gs = slice(g * ATT_GROUP, (g + 1) * ATT_GROUP)
        if masked:
            r = lax.broadcasted_iota(jnp.int32, s.shape, 0)
            c = lax.broadcasted_iota(jnp.int32, s.shape, 1)
            s = jnp.where(r <= c, s, NEG_BIG)
        m_prev = m_ref[:, gs]
        m_new = jnp.maximum(m_prev, jnp.max(s, axis=0, keepdims=True))
        alpha = jnp.exp2(m_prev - m_new)
        p = jnp.exp2((s - m_new).astype(BF16))
        m_ref[:, gs] = m_new
        acc_ref[:, gs] = alpha * acc_ref[:, gs] + jnp.dot(
            vt_ref[:, pl.ds(ks, nk)], p, preferred_element_type=F32)

    def run(items):
        pending = [scores(item) for item in items[:ATT_AHEAD]]
        for n, item in enumerate(items):
            if n + ATT_AHEAD < len(items):
                pending.append(scores(items[n + ATT_AHEAD]))
            update(item, pending.pop(0))

    n_groups = ATT_Q // ATT_GROUP

    def body(j, carry):
        base = j * ATT_K
        run([(g, pl.multiple_of(base + sub * ATT_KSUB, ATT_KSUB), ATT_KSUB, False)
             for sub in range(ATT_K // ATT_KSUB) for g in range(n_groups)])
        return carry

    lax.fori_loop(0, i * (ATT_Q // ATT_K), body, 0)
    run([(g, pl.multiple_of(qs + kb * ATT_GROUP, ATT_GROUP), ATT_GROUP, g == kb)
         for kb in range(n_groups) for g in range(kb, n_groups)])
    o = (acc_ref[:HEAD_DIM, :] / acc_ref[HEAD_DIM:HEAD_DIM + 1, :]).T * sg_ref[...].astype(F32)
    row = qs + lax.broadcasted_iota(jnp.int32, o.shape, 0)
    o_ref[...] = jnp.where(row >= META_PAD, o, 0.0).astype(BF16)


def _fox_attn(qqt, kk, vt, sg):
    lp = kk.shape[1]
    kdim = kk.shape[2]
    return pl.pallas_call(
        _fox_attn_kernel,
        grid=(HEADS, lp // ATT_Q),
        in_specs=[pl.BlockSpec((None, kdim, ATT_Q), lambda h, i: (h, 0, i)),
                  pl.BlockSpec((None, lp, kdim), lambda h, i: (h, 0, 0)),
                  pl.BlockSpec((None, V_ROWS, lp), lambda h, i: (h, 0, 0)),
                  pl.BlockSpec((None, ATT_Q, HEAD_DIM), lambda h, i: (h, i, 0))],
        out_specs=pl.BlockSpec((ATT_Q, HEAD_DIM), lambda h, i: (i, h)),
        out_shape=jax.ShapeDtypeStruct((lp, D_MODEL), BF16),
        scratch_shapes=[pltpu.VMEM((1, ATT_Q), F32), pltpu.VMEM((V_ROWS, ATT_Q), F32)],
        compiler_params=_params("parallel", "arbitrary"),
        name="fox_attn",
    )(qqt, kk, vt, sg)


def kernel(x, meta_tokens, norm_g, ffn_w_gu, ffn_w_down, lb_logits, hg_w_in, hg_w_out, hg_onorm,
           fox_w_in, fox_b_f, fox_w_out, fox_qnorm, fox_knorm):
    assert x.shape[0] == 1 and x.shape[2] == D_MODEL
    seq = x.shape[1]
    real = META_PAD + N_META + seq
    lp = -(-real // ATT_Q) * ATT_Q
    hp = jnp.concatenate([jnp.zeros((META_PAD, D_MODEL), F32), meta_tokens.astype(F32), x[0],
                          jnp.zeros((lp - real, D_MODEL), F32)], axis=0)
    for layer in range(DEPTH):
        hp = _ffn(hp, norm_g[layer, 0], ffn_w_gu[layer, 0], ffn_w_down[layer, 0])
        j = layer // 2
        if layer % 2 == 0:
            q, k, lf, v, sg = _hgrn_in(hp, norm_g[layer, 1], hg_w_in[j], lb_logits, layer)
            og = _hgrn_rec(q, k, lf, v, sg, hg_onorm[j])
            hp = _out_proj(hp, og, hg_w_out[j])
        else:
            q, k, v, sg, lf = _fox_in(hp, norm_g[layer, 1], fox_w_in[j], fox_b_f[j],
                                      fox_qnorm[j], fox_knorm[j])
            qqt, kk, vt = _fox_prep(lf, q, k, v)
            og = _fox_attn(qqt, kk, vt, sg)
            hp = _out_proj(hp, og, fox_w_out[j])
        hp = _ffn(hp, norm_g[layer, 2], ffn_w_gu[layer, 1], ffn_w_down[layer, 1])
    return hp[META_PAD + N_META:META_PAD + N_META + seq][None]
```

```python
import functools

import numpy as np
import jax
import jax.numpy as jnp
from jax import lax
from jax.experimental import pallas as pl
from jax.experimental.pallas import tpu as pltpu

F32 = jnp.float32
BF16 = jnp.bfloat16

D_MODEL = 2048
DEPTH = 2
N_META = 16
BLOCK = 128
META_PAD = (-N_META) % BLOCK
RMS_EPS = 1e-6
HEADS = 16
HEAD_DIM = 128
FFN_DIM = 5504
NEG_BIG = -1e30
LOG2E = 1.4426950408889634

LANES = 128
ROW_TILE = 640
FFN_TILE = 512
FFN_PAD = -(-FFN_DIM // FFN_TILE) * FFN_TILE
PROJ_TILE = 512
ATT_Q = 1280
ATT_K = 1280
ATT_KSPLIT = (640, 640)
ATT_GROUP = 256
ATT_AHEAD = 3
V_ROWS = HEAD_DIM + 16
VMEM_LIMIT = 56 * 1024 * 1024

_NT = (((1,), (1,)), ((), ()))


def _params(*sem):
    return pltpu.CompilerParams(dimension_semantics=sem, vmem_limit_bytes=VMEM_LIMIT)


def _sigmoid(x):
    return 1.0 / (1.0 + jnp.exp(-x))


def _rms_rows(x):
    return x * lax.rsqrt(jnp.mean(x * x, axis=-1, keepdims=True) + RMS_EPS)


def _split3(x):
    hi = x.astype(BF16)
    r1 = x - hi.astype(F32)
    mid = r1.astype(BF16)
    lo = (r1 - mid.astype(F32)).astype(BF16)
    return hi, mid, lo


def _ffn_kernel(h_ref, g_ref, wg_ref, wu_ref, wd_ref, o_ref, xn_ref):
    @pl.when(pl.program_id(1) == 0)
    def _():
        x = h_ref[...]
        xn_ref[...] = (_rms_rows(x) * g_ref[...]).astype(BF16)
        o_ref[...] = x

    xn = xn_ref[...]
    a = jnp.dot(xn, wg_ref[...], preferred_element_type=F32)
    b = jnp.dot(xn, wu_ref[...], preferred_element_type=F32)
    act = (a * (0.5 * _sigmoid(a)) * b).astype(BF16)
    o_ref[...] += jnp.dot(act, wd_ref[...], preferred_element_type=F32)


def _ffn(hp, gain, w_gu, w_down):
    lp = hp.shape[0]
    nf = FFN_PAD // FFN_TILE
    pad = FFN_PAD - FFN_DIM
    wgu = jnp.concatenate(
        [jnp.pad(w_gu[:, :FFN_DIM].astype(BF16), ((0, 0), (0, pad))),
         jnp.pad(w_gu[:, FFN_DIM:].astype(BF16), ((0, 0), (0, pad)))], axis=1)
    wd = jnp.pad(w_down.astype(BF16), ((0, pad), (0, 0)))
    return pl.pallas_call(
        _ffn_kernel,
        grid=(lp // ROW_TILE, nf),
        in_specs=[
            pl.BlockSpec((ROW_TILE, D_MODEL), lambda i, j: (i, 0)),
            pl.BlockSpec((1, D_MODEL), lambda i, j: (0, 0)),
            pl.BlockSpec((D_MODEL, FFN_TILE), lambda i, j: (0, j)),
            pl.BlockSpec((D_MODEL, FFN_TILE), lambda i, j: (0, j + nf)),
            pl.BlockSpec((FFN_TILE, D_MODEL), lambda i, j: (j, 0)),
        ],
        out_specs=pl.BlockSpec((ROW_TILE, D_MODEL), lambda i, j: (i, 0)),
        out_shape=jax.ShapeDtypeStruct((lp, D_MODEL), F32),
        scratch_shapes=[pltpu.VMEM((ROW_TILE, D_MODEL), BF16)],
        compiler_params=_params("parallel", "arbitrary"),
        name="ffn",
    )(hp, gain.reshape(1, D_MODEL), wgu, wgu, wd)


def _out_proj_kernel(h_ref, o_ref, w_ref, out_ref):
    out_ref[...] = h_ref[...] + jnp.dot(o_ref[...], w_ref[...], preferred_element_type=F32)


def _out_proj(hp, og, w_out):
    lp = hp.shape[0]
    return pl.pallas_call(
        _out_proj_kernel,
        grid=(lp // ROW_TILE,),
        in_specs=[
            pl.BlockSpec((ROW_TILE, D_MODEL), lambda i: (i, 0)),
            pl.BlockSpec((ROW_TILE, D_MODEL), lambda i: (i, 0)),
            pl.BlockSpec((D_MODEL, D_MODEL), lambda i: (0, 0)),
        ],
        out_specs=pl.BlockSpec((ROW_TILE, D_MODEL), lambda i: (i, 0)),
        out_shape=jax.ShapeDtypeStruct((lp, D_MODEL), F32),
        compiler_params=_params("parallel"),
        name="out_proj",
    )(hp, og, w_out.astype(BF16))


def _hgrn_in_kernel(h_ref, g_ref, wq_ref, wf_ref, wi_ref, wg_ref, lbl_ref,
                    q_ref, k_ref, lf_ref, v_ref, sg_ref, xn_ref, *, layer):
    @pl.when(pl.program_id(1) == 0)
    def _():
        xn_ref[...] = (_rms_rows(h_ref[...]) * g_ref[...]).astype(BF16)

    xn = xn_ref[...]
    qp = jnp.dot(xn, wq_ref[...], preferred_element_type=F32)
    fp = jnp.dot(xn, wf_ref[...], preferred_element_type=F32)
    ip = jnp.dot(xn, wi_ref[...], preferred_element_type=F32)
    gp = jnp.dot(xn, wg_ref[...], preferred_element_type=F32)
    lg = lbl_ref[...]
    e = jnp.exp(lg - jnp.max(lg, axis=0, keepdims=True))
    lb = jnp.sum(e[:layer + 1], axis=0, keepdims=True) / jnp.sum(e, axis=0, keepdims=True)
    q = qp * _sigmoid(qp)
    fg = lb + (1.0 - lb) * _sigmoid(fp)
    k = 1.0 - fg
    lf = jnp.log(fg) * LOG2E
    sg = _sigmoid(gp)
    for hh in range(PROJ_TILE // HEAD_DIM):
        sl = slice(hh * HEAD_DIM, (hh + 1) * HEAD_DIM)
        q_ref[hh] = q[:, sl].astype(BF16)
        k_ref[hh] = k[:, sl].astype(BF16)
        lf_ref[hh] = lf[:, sl]
        v_ref[hh] = ip[:, sl].astype(BF16)
        sg_ref[hh] = sg[:, sl].astype(BF16)


def _hgrn_in(hp, gain, w_in, lb_logits, layer):
    lp = hp.shape[0]
    nc = D_MODEL // PROJ_TILE
    hpt = PROJ_TILE // HEAD_DIM
    w = w_in.astype(BF16)
    wspec = lambda s: pl.BlockSpec((D_MODEL, PROJ_TILE), lambda i, c, s=s: (0, s * nc + c))
    hm_spec = pl.BlockSpec((hpt, ROW_TILE, HEAD_DIM), lambda i, c: (c, i, 0))
    hm = lambda dt: jax.ShapeDtypeStruct((HEADS, lp, HEAD_DIM), dt)
    return pl.pallas_call(
        functools.partial(_hgrn_in_kernel, layer=layer),
        grid=(lp // ROW_TILE, nc),
        in_specs=[
            pl.BlockSpec((ROW_TILE, D_MODEL), lambda i, c: (i, 0)),
            pl.BlockSpec((1, D_MODEL), lambda i, c: (0, 0)),
            wspec(0), wspec(1), wspec(2), wspec(3),
            pl.BlockSpec((DEPTH + 1, PROJ_TILE), lambda i, c: (0, c)),
        ],
        out_specs=[hm_spec] * 5,
        out_shape=[hm(BF16), hm(BF16), hm(F32), hm(BF16), hm(BF16)],
        scratch_shapes=[pltpu.VMEM((ROW_TILE, D_MODEL), BF16)],
        compiler_params=_params("parallel", "arbitrary"),
        name="hgrn_in",
    )(hp, gain.reshape(1, D_MODEL), w, w, w, w, lb_logits)


_ROW_LEVELS = (64, 32, 16, 8)
_MM_LEVELS = (4, 2)
HGRN_UNROLL = 4


def _hgrn_consts():
    t = np.arange(BLOCK)[:, None]
    j = np.arange(BLOCK)[None, :]
    blocks = [j <= t]
    for b in _MM_LEVELS:
        ref = 2 * b * (t // (2 * b)) + b - 1
        odd = (t // b) % 2 == 1
        blocks.append(np.where(odd, (j > ref) & (j <= t), (j > t) & (j <= ref)))
    g_all = np.concatenate(blocks, axis=0).astype(np.float32)
    masks = []
    for b in _ROW_LEVELS + _MM_LEVELS + (1,):
        masks.append((t // (2 * b) == j // (2 * b)) & ((t // b) % 2 == 1) & ((j // b) % 2 == 0))
    masks.append(t == j)
    return jnp.asarray(g_all, BF16), jnp.asarray(np.stack(masks).astype(np.float32))


def _hgrn_rec_kernel(q_ref, k_ref, lf_ref, v_ref, sg_ref, gmat_ref, mask_ref, on_ref,
                     o_ref, st_ref, ob_ref):
    @pl.when(pl.program_id(0) == 0)
    def _():
        st_ref[...] = jnp.zeros_like(st_ref)

    def boundary_rows(b, size):
        return jnp.concatenate(
            [jnp.broadcast_to(b[s + size - 1:s + size], (2 * size, HEAD_DIM))
             for s in range(0, BLOCK, 2 * size)], axis=0)

    def heads(hg, carry):
        hs = [hg * HGRN_UNROLL + u for u in range(HGRN_UNROLL)]
        lf = [lf_ref[h] for h in hs]
        q = [q_ref[h] for h in hs]
        k = [k_ref[h] for h in hs]
        x3 = [jnp.dot(gmat_ref[...], jnp.concatenate(_split3(l), axis=1),
                      preferred_element_type=F32) for l in lf]
        x = [t[:, :HEAD_DIM] + t[:, HEAD_DIM:2 * HEAD_DIM] + t[:, 2 * HEAD_DIM:] for t in x3]
        b = [t[:BLOCK] for t in x]
        b_last = [t[BLOCK - 1:BLOCK] for t in b]
        att = [jnp.zeros((BLOCK, BLOCK), F32) for _ in hs]
        n_row, n_mm = len(_ROW_LEVELS), len(_MM_LEVELS)
        for lv in range(n_row + n_mm + 2):
            for u in range(HGRN_UNROLL):
                if lv < n_row:
                    d = b[u] - boundary_rows(b[u], _ROW_LEVELS[lv])
                    el = jnp.exp2(-jnp.abs(d)).astype(BF16)
                    qt, kt = q[u] * el, k[u] * el
                elif lv < n_row + n_mm:
                    m = lv - n_row + 1
                    el = jnp.exp2(x[u][m * BLOCK:(m + 1) * BLOCK]).astype(BF16)
                    qt, kt = q[u] * el, k[u] * el
                elif lv == n_row + n_mm:
                    qt, kt = q[u] * jnp.exp2(lf[u]).astype(BF16), k[u]
                else:
                    qt, kt = q[u], k[u]
                a = lax.dot_general(qt, kt, _NT, preferred_element_type=F32)
                att[u] = att[u] + mask_ref[lv] * a
        for u, h in enumerate(hs):
            qb = q[u] * jnp.exp2(b[u]).astype(BF16)
            kd = k[u] * jnp.exp2(b_last[u] - b[u]).astype(BF16)
            vt = v_ref[h].astype(F32).T.astype(BF16)
            st = st_ref[h]
            lhs = jnp.concatenate([att[u].astype(BF16), qb], axis=1)
            rhs = jnp.concatenate([vt, st.astype(BF16)], axis=1)
            o = lax.dot_general(lhs, rhs, _NT, preferred_element_type=F32)
            st_ref[h] = st * jnp.exp2(b_last[u]) + jnp.dot(vt, kd, preferred_element_type=F32)
            og = _rms_rows(o) * on_ref[...] * sg_ref[h].astype(F32)
            ob_ref[h] = og.astype(BF16)
        return carry

    lax.fori_loop(0, HEADS // HGRN_UNROLL, heads, 0)
    for h in range(HEADS):
        o_ref[:, h * HEAD_DIM:(h + 1) * HEAD_DIM] = ob_ref[h]


def _hgrn_rec(q, k, lf, v, sg, onorm):
    lp = q.shape[1]
    gmat, masks = _hgrn_consts()
    hm_spec = pl.BlockSpec((HEADS, BLOCK, HEAD_DIM), lambda c: (0, c, 0))
    return pl.pallas_call(
        _hgrn_rec_kernel,
        grid=(lp // BLOCK,),
        in_specs=[hm_spec] * 5 + [
            pl.BlockSpec(gmat.shape, lambda c: (0, 0)),
            pl.BlockSpec(masks.shape, lambda c: (0, 0, 0)),
            pl.BlockSpec((1, HEAD_DIM), lambda c: (0, 0)),
        ],
        out_specs=pl.BlockSpec((BLOCK, D_MODEL), lambda c: (c, 0)),
        out_shape=jax.ShapeDtypeStruct((lp, D_MODEL), BF16),
        scratch_shapes=[pltpu.VMEM((HEADS, HEAD_DIM, HEAD_DIM), F32),
                        pltpu.VMEM((HEADS, BLOCK, HEAD_DIM), BF16)],
        compiler_params=_params("arbitrary"),
        name="hgrn_rec",
    )(q, k, lf, v, sg, gmat, masks, onorm.reshape(1, HEAD_DIM))


def _fox_in_kernel(h_ref, g_ref, wq_ref, wk_ref, wv_ref, wg_ref, wf_ref, bf_ref, qn_ref, kn_ref,
                   q_ref, k_ref, v_ref, sg_ref, lf_ref, xn_ref):
    @pl.when(pl.program_id(1) == 0)
    def _():
        xn = (_rms_rows(h_ref[...]) * g_ref[...]).astype(BF16)
        xn_ref[...] = xn
        z = jnp.dot(xn, wf_ref[...], preferred_element_type=F32) + bf_ref[...]
        lf_ref[...] = jnp.minimum(z, 0.0) - jnp.log(1.0 + jnp.exp(-jnp.abs(z)))

    xn = xn_ref[...]
    qp = jnp.dot(xn, wq_ref[...], preferred_element_type=F32)
    kp = jnp.dot(xn, wk_ref[...], preferred_element_type=F32)
    vp = jnp.dot(xn, wv_ref[...], preferred_element_type=F32)
    gp = jnp.dot(xn, wg_ref[...], preferred_element_type=F32)
    sg = _sigmoid(gp)
    scale = HEAD_DIM ** -0.5 * LOG2E
    for hh in range(PROJ_TILE // HEAD_DIM):
        sl = slice(hh * HEAD_DIM, (hh + 1) * HEAD_DIM)
        q_ref[hh] = (_rms_rows(qp[:, sl]) * (qn_ref[...] * scale)).astype(BF16)
        k_ref[hh] = (_rms_rows(kp[:, sl]) * kn_ref[...]).astype(BF16)
        v_ref[hh] = vp[:, sl].astype(BF16)
        sg_ref[hh] = sg[:, sl].astype(BF16)


def _fox_in(hp, gain, w_in, b_f, qnorm, knorm):
    lp = hp.shape[0]
    nc = D_MODEL // PROJ_TILE
    hpt = PROJ_TILE // HEAD_DIM
    w = w_in[:, :4 * D_MODEL].astype(BF16)
    wf = w_in[:, 4 * D_MODEL:].astype(BF16)
    wspec = lambda s: pl.BlockSpec((D_MODEL, PROJ_TILE), lambda i, c, s=s: (0, s * nc + c))
    hm_spec = pl.BlockSpec((hpt, ROW_TILE, HEAD_DIM), lambda i, c: (c, i, 0))
    hm = jax.ShapeDtypeStruct((HEADS, lp, HEAD_DIM), BF16)
    vec = lambda n: pl.BlockSpec((1, n), lambda i, c: (0, 0))
    return pl.pallas_call(
        _fox_in_kernel,
        grid=(lp // ROW_TILE, nc),
        in_specs=[
            pl.BlockSpec((ROW_TILE, D_MODEL), lambda i, c: (i, 0)),
            vec(D_MODEL),
            wspec(0), wspec(1), wspec(2), wspec(3),
            pl.BlockSpec((D_MODEL, HEADS), lambda i, c: (0, 0)),
            vec(HEADS), vec(HEAD_DIM), vec(HEAD_DIM),
        ],
        out_specs=[hm_spec] * 4 + [pl.BlockSpec((ROW_TILE, HEADS), lambda i, c: (i, 0))],
        out_shape=[hm, hm, hm, hm, jax.ShapeDtypeStruct((lp, HEADS), F32)],
        scratch_shapes=[pltpu.VMEM((ROW_TILE, D_MODEL), BF16)],
        compiler_params=_params("parallel", "arbitrary"),
        name="fox_in",
    )(hp, gain.reshape(1, D_MODEL), w, w, w, w, wf, b_f.reshape(1, HEADS),
      qnorm.reshape(1, HEAD_DIM), knorm.reshape(1, HEAD_DIM))


def _fox_prep_kernel(lf_ref, tri_ref, q_ref, k_ref, v_ref, qqt_ref, kk_ref, vt_ref, carry_ref):
    i = pl.program_id(0)

    @pl.when(i == 0)
    def _():
        carry_ref[...] = jnp.zeros_like(carry_ref)

    row16 = i * BLOCK + lax.broadcasted_iota(jnp.int32, (BLOCK, HEADS), 0)
    lf = jnp.where(row16 >= META_PAD, lf_ref[...], 0.0)
    tri = tri_ref[...]
    c = carry_ref[...]
    for part in _split3(lf):
        c = c + jnp.dot(tri, part, preferred_element_type=F32)
    carry_ref[...] = c[BLOCK - 1:BLOCK]
    lane = lax.broadcasted_iota(jnp.int32, (BLOCK, LANES), 1)
    row = i * BLOCK + lax.broadcasted_iota(jnp.int32, (BLOCK, LANES), 0)
    for h in range(HEADS):
        col = jnp.broadcast_to(c[:, h:h + 1] * LOG2E, (BLOCK, LANES))
        hi, mid, lo = (p.astype(F32) for p in _split3(col))
        aq = jnp.where(lane == 0, hi, jnp.where(lane == 1, mid, jnp.where(lane == 2, lo,
             jnp.where(lane < 6, 1.0, 0.0))))
        ak = jnp.where(lane < 3, 1.0, jnp.where(lane == 3, jnp.where(row >= META_PAD, -hi, NEG_BIG),
             jnp.where(lane == 4, -mid, jnp.where(lane == 5, -lo, 0.0))))
        qqt_ref[h, :HEAD_DIM, :] = q_ref[h].astype(F32).T.astype(BF16)
        qqt_ref[h, HEAD_DIM:, :] = aq.T.astype(BF16)
        kk_ref[h, :, :HEAD_DIM] = k_ref[h]
        kk_ref[h, :, HEAD_DIM:] = ak.astype(BF16)
        vt_ref[h, :HEAD_DIM, :] = v_ref[h].astype(F32).T.astype(BF16)
        vt_ref[h, HEAD_DIM:, :] = jnp.ones((V_ROWS - HEAD_DIM, BLOCK), BF16)


def _fox_prep(lf, q, k, v):
    lp = lf.shape[0]
    kdim = HEAD_DIM + LANES
    tri = jnp.asarray(np.tril(np.ones((BLOCK, BLOCK), np.float32)), BF16)
    hm_spec = pl.BlockSpec((HEADS, BLOCK, HEAD_DIM), lambda i: (0, i, 0))
    return pl.pallas_call(
        _fox_prep_kernel,
        grid=(lp // BLOCK,),
        in_specs=[pl.BlockSpec((BLOCK, HEADS), lambda i: (i, 0)),
                  pl.BlockSpec((BLOCK, BLOCK), lambda i: (0, 0)),
                  hm_spec, hm_spec, hm_spec],
        out_specs=[pl.BlockSpec((HEADS, kdim, BLOCK), lambda i: (0, 0, i)),
                   pl.BlockSpec((HEADS, BLOCK, kdim), lambda i: (0, i, 0)),
                   pl.BlockSpec((HEADS, V_ROWS, BLOCK), lambda i: (0, 0, i))],
        out_shape=[jax.ShapeDtypeStruct((HEADS, kdim, lp), BF16),
                   jax.ShapeDtypeStruct((HEADS, lp, kdim), BF16),
                   jax.ShapeDtypeStruct((HEADS, V_ROWS, lp), BF16)],
        scratch_shapes=[pltpu.VMEM((1, HEADS), F32)],
        compiler_params=_params("arbitrary"),
        name="fox_prep",
    )(lf, tri, q, k, v)


def _fox_attn_kernel(qqt_ref, kk_ref, vt_ref, sg_ref, o_ref, m_ref, acc_ref, ring_ref):
    i = pl.program_id(1)
    qs = i * ATT_Q
    m_ref[...] = jnp.full_like(m_ref, -jnp.inf)
    acc_ref[...] = jnp.zeros_like(acc_ref)

    def scores(item):
        g, ks, nk, _ = item
        gs = slice(g * ATT_GROUP, (g + 1) * ATT_GROUP)
        return jnp.dot(kk_ref[pl.ds(ks, nk), :], qqt_ref[:, gs], preferred_element_type=F32)

    def update(item, s):
        g, ks, nk, masked = item
        gs = slice(g * ATT_GROUP, (g + 1) * ATT_GROUP)
        if masked:
            r = lax.broadcasted_iota(jnp.int32, s.shape, 0)
            c = lax.broadcasted_iota(jnp.int32, s.shape, 1) + g * ATT_GROUP
            s = jnp.where(r <= c, s, NEG_BIG)
        m_prev = m_ref[:, gs]
        m_new = jnp.maximum(m_prev, jnp.max(s, axis=0, keepdims=True))
        alpha = jnp.exp2(m_prev - m_new)
        p = jnp.exp2((s - m_new).astype(BF16))
        m_ref[:, gs] = m_new
        acc_ref[:, gs] = alpha * acc_ref[:, gs] + jnp.dot(
            vt_ref[:, pl.ds(ks, nk)], p, preferred_element_type=F32)

    def run(items, first, ahead):
        pending = list(first) if first else [scores(item) for item in items[:ATT_AHEAD]]
        for n, item in enumerate(items):
            if n + ATT_AHEAD < len(items):
                pending.append(scores(items[n + ATT_AHEAD]))
            else:
                a = n + ATT_AHEAD - len(items)
                ring_ref[a] = scores(ahead[a])
            update(item, pending.pop(0))

    n_groups = ATT_Q // ATT_GROUP
    n_steps = i * (ATT_Q // ATT_K)

    def step_items(j):
        return [(g, pl.multiple_of(j * ATT_K + sum(ATT_KSPLIT[:sub]), ATT_GROUP), nk, False)
                for sub, nk in enumerate(ATT_KSPLIT) for g in range(n_groups)]

    run([(g, pl.multiple_of(qs, ATT_Q), (g + 1) * ATT_GROUP, True) for g in range(n_groups)],
        None, step_items(0)[:ATT_AHEAD])

    def body(j, carry):
        run(step_items(j), [ring_ref[a] for a in range(ATT_AHEAD)],
            step_items(jnp.minimum(j + 1, n_steps - 1))[:ATT_AHEAD])
        return carry

    lax.fori_loop(0, n_steps, body, 0)
    o = (acc_ref[:HEAD_DIM, :] / acc_ref[HEAD_DIM:HEAD_DIM + 1, :]).T * sg_ref[...].astype(F32)
    row = qs + lax.broadcasted_iota(jnp.int32, o.shape, 0)
    o_ref[...] = jnp.where(row >= META_PAD, o, 0.0).astype(BF16)


def _fox_attn(qqt, kk, vt, sg):
    lp = kk.shape[1]
    kdim = kk.shape[2]
    return pl.pallas_call(
        _fox_attn_kernel,
        grid=(HEADS, lp // ATT_Q),
        in_specs=[pl.BlockSpec((None, kdim, ATT_Q), lambda h, i: (h, 0, i)),
                  pl.BlockSpec((None, lp, kdim), lambda h, i: (h, 0, 0)),
                  pl.BlockSpec((None, V_ROWS, lp), lambda h, i: (h, 0, 0)),
                  pl.BlockSpec((None, ATT_Q, HEAD_DIM), lambda h, i: (h, i, 0))],
        out_specs=pl.BlockSpec((ATT_Q, HEAD_DIM), lambda h, i: (i, h)),
        out_shape=jax.ShapeDtypeStruct((lp, D_MODEL), BF16),
        scratch_shapes=[pltpu.VMEM((1, ATT_Q), F32), pltpu.VMEM((V_ROWS, ATT_Q), F32),
                        pltpu.VMEM((ATT_AHEAD, ATT_KSPLIT[0], ATT_GROUP), F32)],
        compiler_params=_params("parallel", "arbitrary"),
        name="fox_attn",
    )(qqt, kk, vt, sg)


def kernel(x, meta_tokens, norm_g, ffn_w_gu, ffn_w_down, lb_logits, hg_w_in, hg_w_out, hg_onorm,
           fox_w_in, fox_b_f, fox_w_out, fox_qnorm, fox_knorm):
    assert x.shape[0] == 1 and x.shape[2] == D_MODEL
    seq = x.shape[1]
    real = META_PAD + N_META + seq
    lp = -(-real // ATT_Q) * ATT_Q
    hp = jnp.concatenate([jnp.zeros((META_PAD, D_MODEL), F32), meta_tokens.astype(F32), x[0],
                          jnp.zeros((lp - real, D_MODEL), F32)], axis=0)
    for layer in range(DEPTH):
        hp = _ffn(hp, norm_g[layer, 0], ffn_w_gu[layer, 0], ffn_w_down[layer, 0])
        j = layer // 2
        if layer % 2 == 0:
            q, k, lf, v, sg = _hgrn_in(hp, norm_g[layer, 1], hg_w_in[j], lb_logits, layer)
            og = _hgrn_rec(q, k, lf, v, sg, hg_onorm[j])
            hp = _out_proj(hp, og, hg_w_out[j])
        else:
            q, k, v, sg, lf = _fox_in(hp, norm_g[layer, 1], fox_w_in[j], fox_b_f[j],
                                      fox_qnorm[j], fox_knorm[j])
            qqt, kk, vt = _fox_prep(lf, q, k, v)
            og = _fox_attn(qqt, kk, vt, sg)
            hp = _out_proj(hp, og, fox_w_out[j])
        hp = _ffn(hp, norm_g[layer, 2], ffn_w_gu[layer, 1], ffn_w_down[layer, 1])
    return hp[META_PAD + N_META:META_PAD + N_META + seq][None]
```

```python
import functools

import numpy as np
import jax
import jax.numpy as jnp
from jax import lax
from jax.experimental import pallas as pl
from jax.experimental.pallas import tpu as pltpu

F32 = jnp.float32
BF16 = jnp.bfloat16

D_MODEL = 2048
DEPTH = 2
N_META = 16
BLOCK = 128
META_PAD = (-N_META) % BLOCK
RMS_EPS = 1e-6
HEADS = 16
HEAD_DIM = 128
FFN_DIM = 5504
NEG_BIG = -1e30
LOG2E = 1.4426950408889634

LANES = 128
ROW_TILE = 640
FFN_TILE = 512
FFN_PAD = -(-FFN_DIM // FFN_TILE) * FFN_TILE
PROJ_TILE = 512
CAST_ROWS = 256
ATT_Q = 1280
ATT_K = 1280
ATT_KSPLIT = (640, 640)
ATT_GROUP = 256
ATT_AHEAD = 3
V_ROWS = HEAD_DIM + 16
VMEM_LIMIT = 56 * 1024 * 1024

_NT = (((1,), (1,)), ((), ()))


def _params(*sem):
    return pltpu.CompilerParams(dimension_semantics=sem, vmem_limit_bytes=VMEM_LIMIT)


def _sigmoid(x):
    return 1.0 / (1.0 + jnp.exp(-x))


def _rms_rows(x):
    return x * lax.rsqrt(jnp.mean(x * x, axis=-1, keepdims=True) + RMS_EPS)


def _split3(x):
    hi = x.astype(BF16)
    r1 = x - hi.astype(F32)
    mid = r1.astype(BF16)
    lo = (r1 - mid.astype(F32)).astype(BF16)
    return hi, mid, lo


def _cast_wgu_kernel(w_ref, o_ref):
    zeros = jnp.zeros((CAST_ROWS, FFN_PAD - FFN_DIM), BF16)
    o_ref[:, :FFN_DIM] = w_ref[:, :FFN_DIM].astype(BF16)
    o_ref[:, FFN_DIM:FFN_PAD] = zeros
    o_ref[:, FFN_PAD:FFN_PAD + FFN_DIM] = w_ref[:, FFN_DIM:].astype(BF16)
    o_ref[:, FFN_PAD + FFN_DIM:] = zeros


def _cast_wd_kernel(w_ref, o_ref):
    row = pl.program_id(1) * FFN_TILE + lax.broadcasted_iota(jnp.int32, (FFN_TILE, D_MODEL), 0)
    o_ref[...] = jnp.where(row < FFN_DIM, w_ref[...], 0.0).astype(BF16)


def _cast_kernel(w_ref, o_ref):
    o_ref[...] = w_ref[...].astype(BF16)


def _cast_split_kernel(w_ref, o_ref, t_ref):
    n = o_ref.shape[-1]
    o_ref[...] = w_ref[:, :n].astype(BF16)
    t_ref[...] = w_ref[:, n:].astype(BF16)


def _cast_ffn_weights(w_gu, w_down):
    n = w_gu.shape[0]
    wgu = pl.pallas_call(
        _cast_wgu_kernel,
        grid=(n, D_MODEL // CAST_ROWS),
        in_specs=[pl.BlockSpec((None, CAST_ROWS, 2 * FFN_DIM), lambda a, r: (a, r, 0))],
        out_specs=pl.BlockSpec((None, CAST_ROWS, 2 * FFN_PAD), lambda a, r: (a, r, 0)),
        out_shape=jax.ShapeDtypeStruct((n, D_MODEL, 2 * FFN_PAD), BF16),
        compiler_params=_params("parallel", "parallel"),
        name="cast_wgu",
    )(w_gu)
    wd = pl.pallas_call(
        _cast_wd_kernel,
        grid=(n, FFN_PAD // FFN_TILE),
        in_specs=[pl.BlockSpec((None, FFN_TILE, D_MODEL), lambda a, r: (a, r, 0))],
        out_specs=pl.BlockSpec((None, FFN_TILE, D_MODEL), lambda a, r: (a, r, 0)),
        out_shape=jax.ShapeDtypeStruct((n, FFN_PAD, D_MODEL), BF16),
        compiler_params=_params("parallel", "parallel"),
        name="cast_wd",
    )(w_down)
    return wgu, wd


def _cast_proj_weights(w, n_main):
    width = w.shape[1]
    in_spec = pl.BlockSpec((CAST_ROWS, width), lambda r: (r, 0))
    main_spec = pl.BlockSpec((CAST_ROWS, n_main), lambda r: (r, 0))
    main = jax.ShapeDtypeStruct((D_MODEL, n_main), BF16)
    if width == n_main:
        return pl.pallas_call(
            _cast_kernel, grid=(D_MODEL // CAST_ROWS,), in_specs=[in_spec], out_specs=main_spec,
            out_shape=main, compiler_params=_params("parallel"), name="cast_proj")(w)
    return pl.pallas_call(
        _cast_split_kernel, grid=(D_MODEL // CAST_ROWS,), in_specs=[in_spec],
        out_specs=[main_spec, pl.BlockSpec((CAST_ROWS, width - n_main), lambda r: (r, 0))],
        out_shape=[main, jax.ShapeDtypeStruct((D_MODEL, width - n_main), BF16)],
        compiler_params=_params("parallel"), name="cast_proj_split")(w)


def _ffn_kernel(h_ref, g_ref, wg_ref, wu_ref, wd_ref, o_ref, xn_ref):
    @pl.when(pl.program_id(1) == 0)
    def _():
        x = h_ref[...]
        xn_ref[...] = (_rms_rows(x) * g_ref[...]).astype(BF16)
        o_ref[...] = x

    xn = xn_ref[...]
    a = jnp.dot(xn, wg_ref[...], preferred_element_type=F32)
    b = jnp.dot(xn, wu_ref[...], preferred_element_type=F32)
    act = (a * (0.5 * _sigmoid(a)) * b).astype(BF16)
    o_ref[...] += jnp.dot(act, wd_ref[...], preferred_element_type=F32)


def _ffn(hp, gain, wgu, wd, a):
    lp = hp.shape[0]
    nf = FFN_PAD // FFN_TILE
    return pl.pallas_call(
        _ffn_kernel,
        grid=(lp // ROW_TILE, nf),
        in_specs=[
            pl.BlockSpec((ROW_TILE, D_MODEL), lambda i, j: (i, 0)),
            pl.BlockSpec((1, D_MODEL), lambda i, j: (0, 0)),
            pl.BlockSpec((None, D_MODEL, FFN_TILE), lambda i, j: (a, 0, j)),
            pl.BlockSpec((None, D_MODEL, FFN_TILE), lambda i, j: (a, 0, j + nf)),
            pl.BlockSpec((None, FFN_TILE, D_MODEL), lambda i, j: (a, j, 0)),
        ],
        out_specs=pl.BlockSpec((ROW_TILE, D_MODEL), lambda i, j: (i, 0)),
        out_shape=jax.ShapeDtypeStruct((lp, D_MODEL), F32),
        scratch_shapes=[pltpu.VMEM((ROW_TILE, D_MODEL), BF16)],
        compiler_params=_params("parallel", "arbitrary"),
        name="ffn",
    )(hp, gain.reshape(1, D_MODEL), wgu, wgu, wd)


def _out_proj_kernel(h_ref, o_ref, w_ref, out_ref):
    out_ref[...] = h_ref[...] + jnp.dot(o_ref[...], w_ref[...], preferred_element_type=F32)


def _out_proj(hp, og, w_out):
    lp = hp.shape[0]
    return pl.pallas_call(
        _out_proj_kernel,
        grid=(lp // ROW_TILE,),
        in_specs=[
            pl.BlockSpec((ROW_TILE, D_MODEL), lambda i: (i, 0)),
            pl.BlockSpec((ROW_TILE, D_MODEL), lambda i: (i, 0)),
            pl.BlockSpec((D_MODEL, D_MODEL), lambda i: (0, 0)),
        ],
        out_specs=pl.BlockSpec((ROW_TILE, D_MODEL), lambda i: (i, 0)),
        out_shape=jax.ShapeDtypeStruct((lp, D_MODEL), F32),
        compiler_params=_params("parallel"),
        name="out_proj",
    )(hp, og, w_out.astype(BF16))


def _hgrn_in_kernel(h_ref, g_ref, wq_ref, wf_ref, wi_ref, wg_ref, lbl_ref,
                    q_ref, k_ref, lf_ref, v_ref, sg_ref, xn_ref, *, layer):
    @pl.when(pl.program_id(1) == 0)
    def _():
        xn_ref[...] = (_rms_rows(h_ref[...]) * g_ref[...]).astype(BF16)

    xn = xn_ref[...]
    qp = jnp.dot(xn, wq_ref[...], preferred_element_type=F32)
    fp = jnp.dot(xn, wf_ref[...], preferred_element_type=F32)
    ip = jnp.dot(xn, wi_ref[...], preferred_element_type=F32)
    gp = jnp.dot(xn, wg_ref[...], preferred_element_type=F32)
    lg = lbl_ref[...]
    e = jnp.exp(lg - jnp.max(lg, axis=0, keepdims=True))
    lb = jnp.sum(e[:layer + 1], axis=0, keepdims=True) / jnp.sum(e, axis=0, keepdims=True)
    q = qp * _sigmoid(qp)
    fg = lb + (1.0 - lb) * _sigmoid(fp)
    k = 1.0 - fg
    lf = jnp.log(fg) * LOG2E
    sg = _sigmoid(gp)
    for hh in range(PROJ_TILE // HEAD_DIM):
        sl = slice(hh * HEAD_DIM, (hh + 1) * HEAD_DIM)
        q_ref[hh] = q[:, sl].astype(BF16)
        k_ref[hh] = k[:, sl].astype(BF16)
        lf_ref[hh] = lf[:, sl]
        v_ref[hh] = ip[:, sl].astype(BF16)
        sg_ref[hh] = sg[:, sl].astype(BF16)


def _hgrn_in(hp, gain, w_in, lb_logits, layer):
    lp = hp.shape[0]
    nc = D_MODEL // PROJ_TILE
    hpt = PROJ_TILE // HEAD_DIM
    w = _cast_proj_weights(w_in, 4 * D_MODEL)
    wspec = lambda s: pl.BlockSpec((D_MODEL, PROJ_TILE), lambda i, c, s=s: (0, s * nc + c))
    hm_spec = pl.BlockSpec((hpt, ROW_TILE, HEAD_DIM), lambda i, c: (c, i, 0))
    hm = lambda dt: jax.ShapeDtypeStruct((HEADS, lp, HEAD_DIM), dt)
    return pl.pallas_call(
        functools.partial(_hgrn_in_kernel, layer=layer),
        grid=(lp // ROW_TILE, nc),
        in_specs=[
            pl.BlockSpec((ROW_TILE, D_MODEL), lambda i, c: (i, 0)),
            pl.BlockSpec((1, D_MODEL), lambda i, c: (0, 0)),
            wspec(0), wspec(1), wspec(2), wspec(3),
            pl.BlockSpec((DEPTH + 1, PROJ_TILE), lambda i, c: (0, c)),
        ],
        out_specs=[hm_spec] * 5,
        out_shape=[hm(BF16), hm(BF16), hm(F32), hm(BF16), hm(BF16)],
        scratch_shapes=[pltpu.VMEM((ROW_TILE, D_MODEL), BF16)],
        compiler_params=_params("parallel", "arbitrary"),
        name="hgrn_in",
    )(hp, gain.reshape(1, D_MODEL), w, w, w, w, lb_logits)


_ROW_LEVELS = (64, 32, 16, 8)
_MM_LEVELS = (4, 2)
HGRN_UNROLL = 4


def _hgrn_consts():
    t = np.arange(BLOCK)[:, None]
    j = np.arange(BLOCK)[None, :]
    blocks = [j <= t]
    for b in _MM_LEVELS:
        ref = 2 * b * (t // (2 * b)) + b - 1
        odd = (t // b) % 2 == 1
        blocks.append(np.where(odd, (j > ref) & (j <= t), (j > t) & (j <= ref)))
    g_all = np.concatenate(blocks, axis=0).astype(np.float32)
    masks = []
    for b in _ROW_LEVELS + _MM_LEVELS + (1,):
        masks.append((t // (2 * b) == j // (2 * b)) & ((t // b) % 2 == 1) & ((j // b) % 2 == 0))
    masks.append(t == j)
    return jnp.asarray(g_all, BF16), jnp.asarray(np.stack(masks).astype(np.float32))


def _hgrn_rec_kernel(q_ref, k_ref, lf_ref, v_ref, sg_ref, gmat_ref, mask_ref, on_ref,
                     o_ref, st_ref, ob_ref):
    @pl.when(pl.program_id(0) == 0)
    def _():
        st_ref[...] = jnp.zeros_like(st_ref)

    def boundary_rows(b, size):
        return jnp.concatenate(
            [jnp.broadcast_to(b[s + size - 1:s + size], (2 * size, HEAD_DIM))
             for s in range(0, BLOCK, 2 * size)], axis=0)

    def heads(hg, carry):
        hs = [hg * HGRN_UNROLL + u for u in range(HGRN_UNROLL)]
        lf = [lf_ref[h] for h in hs]
        q = [q_ref[h] for h in hs]
        k = [k_ref[h] for h in hs]
        x3 = [jnp.dot(gmat_ref[...], jnp.concatenate(_split3(l), axis=1),
                      preferred_element_type=F32) for l in lf]
        x = [t[:, :HEAD_DIM] + t[:, HEAD_DIM:2 * HEAD_DIM] + t[:, 2 * HEAD_DIM:] for t in x3]
        b = [t[:BLOCK] for t in x]
        b_last = [t[BLOCK - 1:BLOCK] for t in b]
        att = [jnp.zeros((BLOCK, BLOCK), F32) for _ in hs]
        n_row, n_mm = len(_ROW_LEVELS), len(_MM_LEVELS)
        for lv in range(n_row + n_mm + 2):
            for u in range(HGRN_UNROLL):
                if lv < n_row:
                    d = b[u] - boundary_rows(b[u], _ROW_LEVELS[lv])
                    el = jnp.exp2(-jnp.abs(d)).astype(BF16)
                    qt, kt = q[u] * el, k[u] * el
                elif lv < n_row + n_mm:
                    m = lv - n_row + 1
                    el = jnp.exp2(x[u][m * BLOCK:(m + 1) * BLOCK]).astype(BF16)
                    qt, kt = q[u] * el, k[u] * el
                elif lv == n_row + n_mm:
                    qt, kt = q[u] * jnp.exp2(lf[u]).astype(BF16), k[u]
                else:
                    qt, kt = q[u], k[u]
                a = lax.dot_general(qt, kt, _NT, preferred_element_type=F32)
                att[u] = att[u] + mask_ref[lv] * a
        for u, h in enumerate(hs):
            qb = q[u] * jnp.exp2(b[u]).astype(BF16)
            kd = k[u] * jnp.exp2(b_last[u] - b[u]).astype(BF16)
            vt = v_ref[h].astype(F32).T.astype(BF16)
            st = st_ref[h]
            lhs = jnp.concatenate([att[u].astype(BF16), qb], axis=1)
            rhs = jnp.concatenate([vt, st.astype(BF16)], axis=1)
            o = lax.dot_general(lhs, rhs, _NT, preferred_element_type=F32)
            st_ref[h] = st * jnp.exp2(b_last[u]) + jnp.dot(vt, kd, preferred_element_type=F32)
            og = _rms_rows(o) * on_ref[...] * sg_ref[h].astype(F32)
            ob_ref[h] = og.astype(BF16)
        return carry

    lax.fori_loop(0, HEADS // HGRN_UNROLL, heads, 0)
    for h in range(HEADS):
        o_ref[:, h * HEAD_DIM:(h + 1) * HEAD_DIM] = ob_ref[h]


def _hgrn_rec(q, k, lf, v, sg, onorm):
    lp = q.shape[1]
    gmat, masks = _hgrn_consts()
    hm_spec = pl.BlockSpec((HEADS, BLOCK, HEAD_DIM), lambda c: (0, c, 0))
    return pl.pallas_call(
        _hgrn_rec_kernel,
        grid=(lp // BLOCK,),
        in_specs=[hm_spec] * 5 + [
            pl.BlockSpec(gmat.shape, lambda c: (0, 0)),
            pl.BlockSpec(masks.shape, lambda c: (0, 0, 0)),
            pl.BlockSpec((1, HEAD_DIM), lambda c: (0, 0)),
        ],
        out_specs=pl.BlockSpec((BLOCK, D_MODEL), lambda c: (c, 0)),
        out_shape=jax.ShapeDtypeStruct((lp, D_MODEL), BF16),
        scratch_shapes=[pltpu.VMEM((HEADS, HEAD_DIM, HEAD_DIM), F32),
                        pltpu.VMEM((HEADS, BLOCK, HEAD_DIM), BF16)],
        compiler_params=_params("arbitrary"),
        name="hgrn_rec",
    )(q, k, lf, v, sg, gmat, masks, onorm.reshape(1, HEAD_DIM))


def _fox_in_kernel(h_ref, g_ref, wq_ref, wk_ref, wv_ref, wg_ref, wf_ref, bf_ref, qn_ref, kn_ref,
                   q_ref, k_ref, v_ref, sg_ref, lf_ref, xn_ref):
    @pl.when(pl.program_id(1) == 0)
    def _():
        xn = (_rms_rows(h_ref[...]) * g_ref[...]).astype(BF16)
        xn_ref[...] = xn
        z = jnp.dot(xn, wf_ref[...], preferred_element_type=F32) + bf_ref[...]
        lf_ref[...] = jnp.minimum(z, 0.0) - jnp.log(1.0 + jnp.exp(-jnp.abs(z)))

    xn = xn_ref[...]
    qp = jnp.dot(xn, wq_ref[...], preferred_element_type=F32)
    kp = jnp.dot(xn, wk_ref[...], preferred_element_type=F32)
    vp = jnp.dot(xn, wv_ref[...], preferred_element_type=F32)
    gp = jnp.dot(xn, wg_ref[...], preferred_element_type=F32)
    sg = _sigmoid(gp)
    scale = HEAD_DIM ** -0.5 * LOG2E
    for hh in range(PROJ_TILE // HEAD_DIM):
        sl = slice(hh * HEAD_DIM, (hh + 1) * HEAD_DIM)
        q_ref[hh] = (_rms_rows(qp[:, sl]) * (qn_ref[...] * scale)).astype(BF16)
        k_ref[hh] = (_rms_rows(kp[:, sl]) * kn_ref[...]).astype(BF16)
        v_ref[hh] = vp[:, sl].astype(BF16)
        sg_ref[hh] = sg[:, sl].astype(BF16)


def _fox_in(hp, gain, w_in, b_f, qnorm, knorm):
    lp = hp.shape[0]
    nc = D_MODEL // PROJ_TILE
    hpt = PROJ_TILE // HEAD_DIM
    w, wf = _cast_proj_weights(w_in, 4 * D_MODEL)
    wspec = lambda s: pl.BlockSpec((D_MODEL, PROJ_TILE), lambda i, c, s=s: (0, s * nc + c))
    hm_spec = pl.BlockSpec((hpt, ROW_TILE, HEAD_DIM), lambda i, c: (c, i, 0))
    hm = jax.ShapeDtypeStruct((HEADS, lp, HEAD_DIM), BF16)
    vec = lambda n: pl.BlockSpec((1, n), lambda i, c: (0, 0))
    return pl.pallas_call(
        _fox_in_kernel,
        grid=(lp // ROW_TILE, nc),
        in_specs=[
            pl.BlockSpec((ROW_TILE, D_MODEL), lambda i, c: (i, 0)),
            vec(D_MODEL),
            wspec(0), wspec(1), wspec(2), wspec(3),
            pl.BlockSpec((D_MODEL, HEADS), lambda i, c: (0, 0)),
            vec(HEADS), vec(HEAD_DIM), vec(HEAD_DIM),
        ],
        out_specs=[hm_spec] * 4 + [pl.BlockSpec((ROW_TILE, HEADS), lambda i, c: (i, 0))],
        out_shape=[hm, hm, hm, hm, jax.ShapeDtypeStruct((lp, HEADS), F32)],
        scratch_shapes=[pltpu.VMEM((ROW_TILE, D_MODEL), BF16)],
        compiler_params=_params("parallel", "arbitrary"),
        name="fox_in",
    )(hp, gain.reshape(1, D_MODEL), w, w, w, w, wf, b_f.reshape(1, HEADS),
      qnorm.reshape(1, HEAD_DIM), knorm.reshape(1, HEAD_DIM))


def _fox_prep_kernel(lf_ref, tri_ref, q_ref, k_ref, v_ref, qqt_ref, kk_ref, vt_ref, carry_ref):
    i = pl.program_id(0)

    @pl.when(i == 0)
    def _():
        carry_ref[...] = jnp.zeros_like(carry_ref)

    row16 = i * BLOCK + lax.broadcasted_iota(jnp.int32, (BLOCK, HEADS), 0)
    lf = jnp.where(row16 >= META_PAD, lf_ref[...], 0.0)
    tri = tri_ref[...]
    c = carry_ref[...]
    for part in _split3(lf):
        c = c + jnp.dot(tri, part, preferred_element_type=F32)
    carry_ref[...] = c[BLOCK - 1:BLOCK]
    lane = lax.broadcasted_iota(jnp.int32, (BLOCK, LANES), 1)
    row = i * BLOCK + lax.broadcasted_iota(jnp.int32, (BLOCK, LANES), 0)
    for h in range(HEADS):
        col = jnp.broadcast_to(c[:, h:h + 1] * LOG2E, (BLOCK, LANES))
        hi, mid, lo = (p.astype(F32) for p in _split3(col))
        aq = jnp.where(lane == 0, hi, jnp.where(lane == 1, mid, jnp.where(lane == 2, lo,
             jnp.where(lane < 6, 1.0, 0.0))))
        ak = jnp.where(lane < 3, 1.0, jnp.where(lane == 3, jnp.where(row >= META_PAD, -hi, NEG_BIG),
             jnp.where(lane == 4, -mid, jnp.where(lane == 5, -lo, 0.0))))
        qqt_ref[h, :HEAD_DIM, :] = q_ref[h].astype(F32).T.astype(BF16)
        qqt_ref[h, HEAD_DIM:, :] = aq.T.astype(BF16)
        kk_ref[h, :, :HEAD_DIM] = k_ref[h]
        kk_ref[h, :, HEAD_DIM:] = ak.astype(BF16)
        vt_ref[h, :HEAD_DIM, :] = v_ref[h].astype(F32).T.astype(BF16)
        vt_ref[h, HEAD_DIM:, :] = jnp.ones((V_ROWS - HEAD_DIM, BLOCK), BF16)


def _fox_prep(lf, q, k, v):
    lp = lf.shape[0]
    kdim = HEAD_DIM + LANES
    tri = jnp.asarray(np.tril(np.ones((BLOCK, BLOCK), np.float32)), BF16)
    hm_spec = pl.BlockSpec((HEADS, BLOCK, HEAD_DIM), lambda i: (0, i, 0))
    return pl.pallas_call(
        _fox_prep_kernel,
        grid=(lp // BLOCK,),
        in_specs=[pl.BlockSpec((BLOCK, HEADS), lambda i: (i, 0)),
                  pl.BlockSpec((BLOCK, BLOCK), lambda i: (0, 0)),
                  hm_spec, hm_spec, hm_spec],
        out_specs=[pl.BlockSpec((HEADS, kdim, BLOCK), lambda i: (0, 0, i)),
                   pl.BlockSpec((HEADS, BLOCK, kdim), lambda i: (0, i, 0)),
                   pl.BlockSpec((HEADS, V_ROWS, BLOCK), lambda i: (0, 0, i))],
        out_shape=[jax.ShapeDtypeStruct((HEADS, kdim, lp), BF16),
                   jax.ShapeDtypeStruct((HEADS, lp, kdim), BF16),
                   jax.ShapeDtypeStruct((HEADS, V_ROWS, lp), BF16)],
        scratch_shapes=[pltpu.VMEM((1, HEADS), F32)],
        compiler_params=_params("arbitrary"),
        name="fox_prep",
    )(lf, tri, q, k, v)


def _fox_attn_kernel(qqt_ref, kk_ref, vt_ref, sg_ref, o_ref, m_ref, acc_ref, ring_ref):
    i = pl.program_id(1)
    qs = i * ATT_Q
    m_ref[...] = jnp.full_like(m_ref, -jnp.inf)
    acc_ref[...] = jnp.zeros_like(acc_ref)

    def scores(item):
        g, ks, nk, _ = item
        gs = slice(g * ATT_GROUP, (g + 1) * ATT_GROUP)
        return jnp.dot(kk_ref[pl.ds(ks, nk), :], qqt_ref[:, gs], preferred_element_type=F32)

    def update(item, s):
        g, ks, nk, masked = item
        gs = slice(g * ATT_GROUP, (g + 1) * ATT_GROUP)
        if masked:
            r = lax.broadcasted_iota(jnp.int32, s.shape, 0)
            c = lax.broadcasted_iota(jnp.int32, s.shape, 1) + g * ATT_GROUP
            s = jnp.where(r <= c, s, NEG_BIG)
        m_prev = m_ref[:, gs]
        m_new = jnp.maximum(m_prev, jnp.max(s, axis=0, keepdims=True))
        alpha = jnp.exp2(m_prev - m_new)
        p = jnp.exp2((s - m_new).astype(BF16))
        m_ref[:, gs] = m_new
        acc_ref[:, gs] = alpha * acc_ref[:, gs] + jnp.dot(
            vt_ref[:, pl.ds(ks, nk)], p, preferred_element_type=F32)

    def run(items, first, ahead):
        pending = list(first) if first else [scores(item) for item in items[:ATT_AHEAD]]
        for n, item in enumerate(items):
            if n + ATT_AHEAD < len(items):
                pending.append(scores(items[n + ATT_AHEAD]))
            else:
                a = n + ATT_AHEAD - len(items)
                ring_ref[a] = scores(ahead[a])
            update(item, pending.pop(0))

    n_groups = ATT_Q // ATT_GROUP
    n_steps = i * (ATT_Q // ATT_K)

    def step_items(j):
        return [(g, pl.multiple_of(j * ATT_K + sum(ATT_KSPLIT[:sub]), ATT_GROUP), nk, False)
                for sub, nk in enumerate(ATT_KSPLIT) for g in range(n_groups)]

    run([(g, pl.multiple_of(qs, ATT_Q), (g + 1) * ATT_GROUP, True) for g in range(n_groups)],
        None, step_items(0)[:ATT_AHEAD])

    def body(j, carry):
        run(step_items(j), [ring_ref[a] for a in range(ATT_AHEAD)],
            step_items(jnp.minimum(j + 1, n_steps - 1))[:ATT_AHEAD])
        return carry

    lax.fori_loop(0, n_steps, body, 0)
    o = (acc_ref[:HEAD_DIM, :] / acc_ref[HEAD_DIM:HEAD_DIM + 1, :]).T * sg_ref[...].astype(F32)
    row = qs + lax.broadcasted_iota(jnp.int32, o.shape, 0)
    o_ref[...] = jnp.where(row >= META_PAD, o, 0.0).astype(BF16)


def _fox_attn(qqt, kk, vt, sg):
    lp = kk.shape[1]
    kdim = kk.shape[2]
    return pl.pallas_call(
        _fox_attn_kernel,
        grid=(HEADS, lp // ATT_Q),
        in_specs=[pl.BlockSpec((None, kdim, ATT_Q), lambda h, i: (h, 0, i)),
                  pl.BlockSpec((None, lp, kdim), lambda h, i: (h, 0, 0)),
                  pl.BlockSpec((None, V_ROWS, lp), lambda h, i: (h, 0, 0)),
                  pl.BlockSpec((None, ATT_Q, HEAD_DIM), lambda h, i: (h, i, 0))],
        out_specs=pl.BlockSpec((ATT_Q, HEAD_DIM), lambda h, i: (i, h)),
        out_shape=jax.ShapeDtypeStruct((lp, D_MODEL), BF16),
        scratch_shapes=[pltpu.VMEM((1, ATT_Q), F32), pltpu.VMEM((V_ROWS, ATT_Q), F32),
                        pltpu.VMEM((ATT_AHEAD, ATT_KSPLIT[0], ATT_GROUP), F32)],
        compiler_params=_params("parallel", "arbitrary"),
        name="fox_attn",
    )(qqt, kk, vt, sg)


def kernel(x, meta_tokens, norm_g, ffn_w_gu, ffn_w_down, lb_logits, hg_w_in, hg_w_out, hg_onorm,
           fox_w_in, fox_b_f, fox_w_out, fox_qnorm, fox_knorm):
    assert x.shape[0] == 1 and x.shape[2] == D_MODEL
    seq = x.shape[1]
    real = META_PAD + N_META + seq
    lp = -(-real // ATT_Q) * ATT_Q
    hp = jnp.concatenate([jnp.zeros((META_PAD, D_MODEL), F32), meta_tokens.astype(F32), x[0],
                          jnp.zeros((lp - real, D_MODEL), F32)], axis=0)
    wgu, wd = _cast_ffn_weights(ffn_w_gu.reshape(2 * DEPTH, D_MODEL, 2 * FFN_DIM),
                                ffn_w_down.reshape(2 * DEPTH, FFN_DIM, D_MODEL))
    for layer in range(DEPTH):
        hp = _ffn(hp, norm_g[layer, 0], wgu, wd, 2 * layer)
        j = layer // 2
        if layer % 2 == 0:
            q, k, lf, v, sg = _hgrn_in(hp, norm_g[layer, 1], hg_w_in[j], lb_logits, layer)
            og = _hgrn_rec(q, k, lf, v, sg, hg_onorm[j])
            hp = _out_proj(hp, og, hg_w_out[j])
        else:
            q, k, v, sg, lf = _fox_in(hp, norm_g[layer, 1], fox_w_in[j], fox_b_f[j],
                                      fox_qnorm[j], fox_knorm[j])
            qqt, kk, vt = _fox_prep(lf, q, k, v)
            og = _fox_attn(qqt, kk, vt, sg)
            hp = _out_proj(hp, og, fox_w_out[j])
        hp = _ffn(hp, norm_g[layer, 2], wgu, wd, 2 * layer + 1)
    return hp[META_PAD + N_META:META_PAD + N_META + seq][None]
```

```python
import functools

import numpy as np
import jax
import jax.numpy as jnp
from jax import lax
from jax.experimental import pallas as pl
from jax.experimental.pallas import tpu as pltpu

F32 = jnp.float32
BF16 = jnp.bfloat16

D_MODEL = 2048
DEPTH = 2
N_META = 16
BLOCK = 128
META_PAD = (-N_META) % BLOCK
RMS_EPS = 1e-6
HEADS = 16
HEAD_DIM = 128
FFN_DIM = 5504
NEG_BIG = -1e30
LOG2E = 1.4426950408889634

LANES = 128
ROW_TILE = 640
FFN_TILE = 512
FFN_PAD = -(-FFN_DIM // FFN_TILE) * FFN_TILE
PROJ_TILE = 512
CAST_ROWS = 256
ATT_Q = 1280
ATT_K = 1280
ATT_KSPLIT = (512, 768)
ATT_GROUP = 256
ATT_AHEAD = 3
V_ROWS = HEAD_DIM + 16
SAFE_SHIFT_RANGE = 80.0
VMEM_LIMIT = 56 * 1024 * 1024

_NT = (((1,), (1,)), ((), ()))


def _params(*sem):
    return pltpu.CompilerParams(dimension_semantics=sem, vmem_limit_bytes=VMEM_LIMIT)


def _sigmoid(x):
    return 1.0 / (1.0 + jnp.exp(-x))


def _rms_rows(x):
    return x * lax.rsqrt(jnp.mean(x * x, axis=-1, keepdims=True) + RMS_EPS)


def _split3(x):
    hi = x.astype(BF16)
    r1 = x - hi.astype(F32)
    mid = r1.astype(BF16)
    lo = (r1 - mid.astype(F32)).astype(BF16)
    return hi, mid, lo


def _cast_wgu_kernel(w_ref, o_ref):
    zeros = jnp.zeros((CAST_ROWS, FFN_PAD - FFN_DIM), BF16)
    o_ref[:, :FFN_DIM] = w_ref[:, :FFN_DIM].astype(BF16)
    o_ref[:, FFN_DIM:FFN_PAD] = zeros
    o_ref[:, FFN_PAD:FFN_PAD + FFN_DIM] = w_ref[:, FFN_DIM:].astype(BF16)
    o_ref[:, FFN_PAD + FFN_DIM:] = zeros


def _cast_wd_kernel(w_ref, o_ref):
    row = pl.program_id(1) * FFN_TILE + lax.broadcasted_iota(jnp.int32, (FFN_TILE, D_MODEL), 0)
    o_ref[...] = jnp.where(row < FFN_DIM, w_ref[...], 0.0).astype(BF16)


def _cast_kernel(w_ref, o_ref):
    o_ref[...] = w_ref[...].astype(BF16)


def _cast_split_kernel(w_ref, o_ref, t_ref):
    n = o_ref.shape[-1]
    o_ref[...] = w_ref[:, :n].astype(BF16)
    t_ref[...] = w_ref[:, n:].astype(BF16)


def _cast_ffn_weights(w_gu, w_down):
    n = w_gu.shape[0]
    wgu = pl.pallas_call(
        _cast_wgu_kernel,
        grid=(n, D_MODEL // CAST_ROWS),
        in_specs=[pl.BlockSpec((None, CAST_ROWS, 2 * FFN_DIM), lambda a, r: (a, r, 0))],
        out_specs=pl.BlockSpec((None, CAST_ROWS, 2 * FFN_PAD), lambda a, r: (a, r, 0)),
        out_shape=jax.ShapeDtypeStruct((n, D_MODEL, 2 * FFN_PAD), BF16),
        compiler_params=_params("parallel", "parallel"),
        name="cast_wgu",
    )(w_gu)
    wd = pl.pallas_call(
        _cast_wd_kernel,
        grid=(n, FFN_PAD // FFN_TILE),
        in_specs=[pl.BlockSpec((None, FFN_TILE, D_MODEL), lambda a, r: (a, r, 0))],
        out_specs=pl.BlockSpec((None, FFN_TILE, D_MODEL), lambda a, r: (a, r, 0)),
        out_shape=jax.ShapeDtypeStruct((n, FFN_PAD, D_MODEL), BF16),
        compiler_params=_params("parallel", "parallel"),
        name="cast_wd",
    )(w_down)
    return wgu, wd


def _cast_proj_weights(w, n_main):
    width = w.shape[1]
    in_spec = pl.BlockSpec((CAST_ROWS, width), lambda r: (r, 0))
    main_spec = pl.BlockSpec((CAST_ROWS, n_main), lambda r: (r, 0))
    main = jax.ShapeDtypeStruct((D_MODEL, n_main), BF16)
    if width == n_main:
        return pl.pallas_call(
            _cast_kernel, grid=(D_MODEL // CAST_ROWS,), in_specs=[in_spec], out_specs=main_spec,
            out_shape=main, compiler_params=_params("parallel"), name="cast_proj")(w)
    return pl.pallas_call(
        _cast_split_kernel, grid=(D_MODEL // CAST_ROWS,), in_specs=[in_spec],
        out_specs=[main_spec, pl.BlockSpec((CAST_ROWS, width - n_main), lambda r: (r, 0))],
        out_shape=[main, jax.ShapeDtypeStruct((D_MODEL, width - n_main), BF16)],
        compiler_params=_params("parallel"), name="cast_proj_split")(w)


def _ffn_kernel(h_ref, g_ref, wg_ref, wu_ref, wd_ref, o_ref, xn_ref):
    @pl.when(pl.program_id(1) == 0)
    def _():
        x = h_ref[...]
        xn_ref[...] = (_rms_rows(x) * g_ref[...]).astype(BF16)
        o_ref[...] = x

    xn = xn_ref[...]
    a = jnp.dot(xn, wg_ref[...], preferred_element_type=F32)
    b = jnp.dot(xn, wu_ref[...], preferred_element_type=F32)
    act = (a * (0.5 * _sigmoid(a)) * b).astype(BF16)
    o_ref[...] += jnp.dot(act, wd_ref[...], preferred_element_type=F32)


def _ffn(hp, gain, wgu, wd, a):
    lp = hp.shape[0]
    nf = FFN_PAD // FFN_TILE
    return pl.pallas_call(
        _ffn_kernel,
        grid=(lp // ROW_TILE, nf),
        in_specs=[
            pl.BlockSpec((ROW_TILE, D_MODEL), lambda i, j: (i, 0)),
            pl.BlockSpec((1, D_MODEL), lambda i, j: (0, 0)),
            pl.BlockSpec((None, D_MODEL, FFN_TILE), lambda i, j: (a, 0, j)),
            pl.BlockSpec((None, D_MODEL, FFN_TILE), lambda i, j: (a, 0, j + nf)),
            pl.BlockSpec((None, FFN_TILE, D_MODEL), lambda i, j: (a, j, 0)),
        ],
        out_specs=pl.BlockSpec((ROW_TILE, D_MODEL), lambda i, j: (i, 0)),
        out_shape=jax.ShapeDtypeStruct((lp, D_MODEL), F32),
        scratch_shapes=[pltpu.VMEM((ROW_TILE, D_MODEL), BF16)],
        compiler_params=_params("parallel", "arbitrary"),
        name="ffn",
    )(hp, gain.reshape(1, D_MODEL), wgu, wgu, wd)


def _out_proj_kernel(h_ref, o_ref, w_ref, out_ref):
    out_ref[...] = h_ref[...] + jnp.dot(o_ref[...], w_ref[...], preferred_element_type=F32)


def _out_proj(hp, og, w_out):
    lp = hp.shape[0]
    return pl.pallas_call(
        _out_proj_kernel,
        grid=(lp // ROW_TILE,),
        in_specs=[
            pl.BlockSpec((ROW_TILE, D_MODEL), lambda i: (i, 0)),
            pl.BlockSpec((ROW_TILE, D_MODEL), lambda i: (i, 0)),
            pl.BlockSpec((D_MODEL, D_MODEL), lambda i: (0, 0)),
        ],
        out_specs=pl.BlockSpec((ROW_TILE, D_MODEL), lambda i: (i, 0)),
        out_shape=jax.ShapeDtypeStruct((lp, D_MODEL), F32),
        compiler_params=_params("parallel"),
        name="out_proj",
    )(hp, og, w_out.astype(BF16))


def _hgrn_in_kernel(h_ref, g_ref, wq_ref, wf_ref, wi_ref, wg_ref, lbl_ref,
                    q_ref, k_ref, lf_ref, v_ref, sg_ref, xn_ref, *, layer):
    @pl.when(pl.program_id(1) == 0)
    def _():
        xn_ref[...] = (_rms_rows(h_ref[...]) * g_ref[...]).astype(BF16)

    xn = xn_ref[...]
    qp = jnp.dot(xn, wq_ref[...], preferred_element_type=F32)
    fp = jnp.dot(xn, wf_ref[...], preferred_element_type=F32)
    ip = jnp.dot(xn, wi_ref[...], preferred_element_type=F32)
    gp = jnp.dot(xn, wg_ref[...], preferred_element_type=F32)
    lg = lbl_ref[...]
    e = jnp.exp(lg - jnp.max(lg, axis=0, keepdims=True))
    lb = jnp.sum(e[:layer + 1], axis=0, keepdims=True) / jnp.sum(e, axis=0, keepdims=True)
    q = qp * _sigmoid(qp)
    fg = lb + (1.0 - lb) * _sigmoid(fp)
    k = 1.0 - fg
    lf = jnp.log(fg) * LOG2E
    sg = _sigmoid(gp)
    for hh in range(PROJ_TILE // HEAD_DIM):
        sl = slice(hh * HEAD_DIM, (hh + 1) * HEAD_DIM)
        q_ref[hh] = q[:, sl].astype(BF16)
        k_ref[hh] = k[:, sl].astype(BF16)
        lf_ref[hh] = lf[:, sl]
        v_ref[hh] = ip[:, sl].astype(BF16)
        sg_ref[hh] = sg[:, sl].astype(BF16)


def _hgrn_in(hp, gain, w_in, lb_logits, layer):
    lp = hp.shape[0]
    nc = D_MODEL // PROJ_TILE
    hpt = PROJ_TILE // HEAD_DIM
    w = _cast_proj_weights(w_in, 4 * D_MODEL)
    wspec = lambda s: pl.BlockSpec((D_MODEL, PROJ_TILE), lambda i, c, s=s: (0, s * nc + c))
    hm_spec = pl.BlockSpec((hpt, ROW_TILE, HEAD_DIM), lambda i, c: (c, i, 0))
    hm = lambda dt: jax.ShapeDtypeStruct((HEADS, lp, HEAD_DIM), dt)
    return pl.pallas_call(
        functools.partial(_hgrn_in_kernel, layer=layer),
        grid=(lp // ROW_TILE, nc),
        in_specs=[
            pl.BlockSpec((ROW_TILE, D_MODEL), lambda i, c: (i, 0)),
            pl.BlockSpec((1, D_MODEL), lambda i, c: (0, 0)),
            wspec(0), wspec(1), wspec(2), wspec(3),
            pl.BlockSpec((DEPTH + 1, PROJ_TILE), lambda i, c: (0, c)),
        ],
        out_specs=[hm_spec] * 5,
        out_shape=[hm(BF16), hm(BF16), hm(F32), hm(BF16), hm(BF16)],
        scratch_shapes=[pltpu.VMEM((ROW_TILE, D_MODEL), BF16)],
        compiler_params=_params("parallel", "arbitrary"),
        name="hgrn_in",
    )(hp, gain.reshape(1, D_MODEL), w, w, w, w, lb_logits)


_ROW_LEVELS = (64, 32, 16, 8)
_MM_LEVELS = (4, 2)
HGRN_UNROLL = 4


def _hgrn_consts():
    t = np.arange(BLOCK)[:, None]
    j = np.arange(BLOCK)[None, :]
    blocks = [j <= t]
    for b in _MM_LEVELS:
        ref = 2 * b * (t // (2 * b)) + b - 1
        odd = (t // b) % 2 == 1
        blocks.append(np.where(odd, (j > ref) & (j <= t), (j > t) & (j <= ref)))
    g_all = np.concatenate(blocks, axis=0).astype(np.float32)
    masks = []
    for b in _ROW_LEVELS + _MM_LEVELS + (1,):
        masks.append((t // (2 * b) == j // (2 * b)) & ((t // b) % 2 == 1) & ((j // b) % 2 == 0))
    masks.append(t == j)
    return jnp.asarray(g_all, BF16), jnp.asarray(np.stack(masks).astype(np.float32))


def _hgrn_rec_kernel(q_ref, k_ref, lf_ref, v_ref, sg_ref, gmat_ref, mask_ref, on_ref,
                     o_ref, st_ref, ob_ref):
    @pl.when(pl.program_id(0) == 0)
    def _():
        st_ref[...] = jnp.zeros_like(st_ref)

    def boundary_rows(b, size):
        return jnp.concatenate(
            [jnp.broadcast_to(b[s + size - 1:s + size], (2 * size, HEAD_DIM))
             for s in range(0, BLOCK, 2 * size)], axis=0)

    def heads(hg, carry):
        hs = [hg * HGRN_UNROLL + u for u in range(HGRN_UNROLL)]
        lf = [lf_ref[h] for h in hs]
        q = [q_ref[h] for h in hs]
        k = [k_ref[h] for h in hs]
        x3 = [jnp.dot(gmat_ref[...], jnp.concatenate(_split3(l), axis=1),
                      preferred_element_type=F32) for l in lf]
        x = [t[:, :HEAD_DIM] + t[:, HEAD_DIM:2 * HEAD_DIM] + t[:, 2 * HEAD_DIM:] for t in x3]
        b = [t[:BLOCK] for t in x]
        b_last = [t[BLOCK - 1:BLOCK] for t in b]
        att = [jnp.zeros((BLOCK, BLOCK), F32) for _ in hs]
        n_row, n_mm = len(_ROW_LEVELS), len(_MM_LEVELS)
        for lv in range(n_row + n_mm + 2):
            for u in range(HGRN_UNROLL):
                if lv < n_row:
                    d = b[u] - boundary_rows(b[u], _ROW_LEVELS[lv])
                    el = jnp.exp2(-jnp.abs(d)).astype(BF16)
                    qt, kt = q[u] * el, k[u] * el
                elif lv < n_row + n_mm:
                    m = lv - n_row + 1
                    el = jnp.exp2(x[u][m * BLOCK:(m + 1) * BLOCK]).astype(BF16)
                    qt, kt = q[u] * el, k[u] * el
                elif lv == n_row + n_mm:
                    qt, kt = q[u] * jnp.exp2(lf[u]).astype(BF16), k[u]
                else:
                    qt, kt = q[u], k[u]
                a = lax.dot_general(qt, kt, _NT, preferred_element_type=F32)
                att[u] = att[u] + mask_ref[lv] * a
        for u, h in enumerate(hs):
            qb = q[u] * jnp.exp2(b[u]).astype(BF16)
            kd = k[u] * jnp.exp2(b_last[u] - b[u]).astype(BF16)
            vt = v_ref[h].astype(F32).T.astype(BF16)
            st = st_ref[h]
            lhs = jnp.concatenate([att[u].astype(BF16), qb], axis=1)
            rhs = jnp.concatenate([vt, st.astype(BF16)], axis=1)
            o = lax.dot_general(lhs, rhs, _NT, preferred_element_type=F32)
            st_ref[h] = st * jnp.exp2(b_last[u]) + jnp.dot(vt, kd, preferred_element_type=F32)
            og = _rms_rows(o) * on_ref[...] * sg_ref[h].astype(F32)
            ob_ref[h] = og.astype(BF16)
        return carry

    lax.fori_loop(0, HEADS // HGRN_UNROLL, heads, 0)
    for h in range(HEADS):
        o_ref[:, h * HEAD_DIM:(h + 1) * HEAD_DIM] = ob_ref[h]


def _hgrn_rec(q, k, lf, v, sg, onorm):
    lp = q.shape[1]
    gmat, masks = _hgrn_consts()
    hm_spec = pl.BlockSpec((HEADS, BLOCK, HEAD_DIM), lambda c: (0, c, 0))
    return pl.pallas_call(
        _hgrn_rec_kernel,
        grid=(lp // BLOCK,),
        in_specs=[hm_spec] * 5 + [
            pl.BlockSpec(gmat.shape, lambda c: (0, 0)),
            pl.BlockSpec(masks.shape, lambda c: (0, 0, 0)),
            pl.BlockSpec((1, HEAD_DIM), lambda c: (0, 0)),
        ],
        out_specs=pl.BlockSpec((BLOCK, D_MODEL), lambda c: (c, 0)),
        out_shape=jax.ShapeDtypeStruct((lp, D_MODEL), BF16),
        scratch_shapes=[pltpu.VMEM((HEADS, HEAD_DIM, HEAD_DIM), F32),
                        pltpu.VMEM((HEADS, BLOCK, HEAD_DIM), BF16)],
        compiler_params=_params("arbitrary"),
        name="hgrn_rec",
    )(q, k, lf, v, sg, gmat, masks, onorm.reshape(1, HEAD_DIM))


def _fox_in_kernel(h_ref, g_ref, wq_ref, wk_ref, wv_ref, wg_ref, wf_ref, bf_ref, qn_ref, kn_ref,
                   q_ref, k_ref, v_ref, sg_ref, lf_ref, xn_ref):
    @pl.when(pl.program_id(1) == 0)
    def _():
        xn = (_rms_rows(h_ref[...]) * g_ref[...]).astype(BF16)
        xn_ref[...] = xn
        z = jnp.dot(xn, wf_ref[...], preferred_element_type=F32) + bf_ref[...]
        lf_ref[...] = jnp.minimum(z, 0.0) - jnp.log(1.0 + jnp.exp(-jnp.abs(z)))

    xn = xn_ref[...]
    qp = jnp.dot(xn, wq_ref[...], preferred_element_type=F32)
    kp = jnp.dot(xn, wk_ref[...], preferred_element_type=F32)
    vp = jnp.dot(xn, wv_ref[...], preferred_element_type=F32)
    gp = jnp.dot(xn, wg_ref[...], preferred_element_type=F32)
    sg = _sigmoid(gp)
    scale = HEAD_DIM ** -0.5 * LOG2E
    for hh in range(PROJ_TILE // HEAD_DIM):
        sl = slice(hh * HEAD_DIM, (hh + 1) * HEAD_DIM)
        q_ref[hh] = (_rms_rows(qp[:, sl]) * (qn_ref[...] * scale)).astype(BF16)
        k_ref[hh] = (_rms_rows(kp[:, sl]) * kn_ref[...]).astype(BF16)
        v_ref[hh] = vp[:, sl].astype(BF16)
        sg_ref[hh] = sg[:, sl].astype(BF16)


def _fox_in(hp, gain, w_in, b_f, qnorm, knorm):
    lp = hp.shape[0]
    nc = D_MODEL // PROJ_TILE
    hpt = PROJ_TILE // HEAD_DIM
    w, wf = _cast_proj_weights(w_in, 4 * D_MODEL)
    wspec = lambda s: pl.BlockSpec((D_MODEL, PROJ_TILE), lambda i, c, s=s: (0, s * nc + c))
    hm_spec = pl.BlockSpec((hpt, ROW_TILE, HEAD_DIM), lambda i, c: (c, i, 0))
    hm = jax.ShapeDtypeStruct((HEADS, lp, HEAD_DIM), BF16)
    vec = lambda n: pl.BlockSpec((1, n), lambda i, c: (0, 0))
    return pl.pallas_call(
        _fox_in_kernel,
        grid=(lp // ROW_TILE, nc),
        in_specs=[
            pl.BlockSpec((ROW_TILE, D_MODEL), lambda i, c: (i, 0)),
            vec(D_MODEL),
            wspec(0), wspec(1), wspec(2), wspec(3),
            pl.BlockSpec((D_MODEL, HEADS), lambda i, c: (0, 0)),
            vec(HEADS), vec(HEAD_DIM), vec(HEAD_DIM),
        ],
        out_specs=[hm_spec] * 4 + [pl.BlockSpec((ROW_TILE, HEADS), lambda i, c: (i, 0))],
        out_shape=[hm, hm, hm, hm, jax.ShapeDtypeStruct((lp, HEADS), F32)],
        scratch_shapes=[pltpu.VMEM((ROW_TILE, D_MODEL), BF16)],
        compiler_params=_params("parallel", "arbitrary"),
        name="fox_in",
    )(hp, gain.reshape(1, D_MODEL), w, w, w, w, wf, b_f.reshape(1, HEADS),
      qnorm.reshape(1, HEAD_DIM), knorm.reshape(1, HEAD_DIM))


def _fox_prep_kernel(lf_ref, tri_ref, q_ref, k_ref, v_ref,
                     qqt_ref, kk_ref, vt_ref, qn_ref, nmax_ref, carry_ref):
    i = pl.program_id(0)

    @pl.when(i == 0)
    def _():
        carry_ref[...] = jnp.zeros_like(carry_ref)
        nmax_ref[...] = jnp.zeros_like(nmax_ref)

    row16 = i * BLOCK + lax.broadcasted_iota(jnp.int32, (BLOCK, HEADS), 0)
    lf = jnp.where(row16 >= META_PAD, lf_ref[...], 0.0)
    tri = tri_ref[...]
    c = carry_ref[...]
    for part in _split3(lf):
        c = c + jnp.dot(tri, part, preferred_element_type=F32)
    carry_ref[...] = c[BLOCK - 1:BLOCK]
    lane = lax.broadcasted_iota(jnp.int32, (BLOCK, LANES), 1)
    row = i * BLOCK + lax.broadcasted_iota(jnp.int32, (BLOCK, LANES), 0)
    for h in range(HEADS):
        col = jnp.broadcast_to(c[:, h:h + 1] * LOG2E, (BLOCK, LANES))
        hi, mid, lo = (p.astype(F32) for p in _split3(col))
        aq = jnp.where(lane == 0, hi, jnp.where(lane == 1, mid, jnp.where(lane == 2, lo,
             jnp.where(lane < 6, 1.0, 0.0))))
        ak = jnp.where(lane < 3, 1.0, jnp.where(lane == 3, jnp.where(row >= META_PAD, -hi, NEG_BIG),
             jnp.where(lane == 4, -mid, jnp.where(lane == 5, -lo, 0.0))))
        qt = q_ref[h].astype(F32).T
        qqt_ref[h, :HEAD_DIM, :] = qt.astype(BF16)
        qqt_ref[h, HEAD_DIM:, :] = aq.T.astype(BF16)
        qn = jnp.sqrt(jnp.sum(qt * qt, axis=0, keepdims=True))
        qn_ref[h] = qn
        kf = k_ref[h].astype(F32)
        kn = jnp.sqrt(jnp.max(jnp.sum(kf * kf, axis=1, keepdims=True), axis=0, keepdims=True))
        nmax_ref[h:h + 1, :] = jnp.maximum(nmax_ref[h:h + 1, :], jnp.max(qn, axis=1, keepdims=True))
        nmax_ref[HEADS + h:HEADS + h + 1, :] = jnp.maximum(nmax_ref[HEADS + h:HEADS + h + 1, :], kn)
        kk_ref[h, :, :HEAD_DIM] = k_ref[h]
        kk_ref[h, :, HEAD_DIM:] = ak.astype(BF16)
        vt_ref[h, :HEAD_DIM, :] = v_ref[h].astype(F32).T.astype(BF16)
        vt_ref[h, HEAD_DIM:, :] = jnp.ones((V_ROWS - HEAD_DIM, BLOCK), BF16)


def _fox_prep(lf, q, k, v):
    lp = lf.shape[0]
    kdim = HEAD_DIM + LANES
    tri = jnp.asarray(np.tril(np.ones((BLOCK, BLOCK), np.float32)), BF16)
    hm_spec = pl.BlockSpec((HEADS, BLOCK, HEAD_DIM), lambda i: (0, i, 0))
    return pl.pallas_call(
        _fox_prep_kernel,
        grid=(lp // BLOCK,),
        in_specs=[pl.BlockSpec((BLOCK, HEADS), lambda i: (i, 0)),
                  pl.BlockSpec((BLOCK, BLOCK), lambda i: (0, 0)),
                  hm_spec, hm_spec, hm_spec],
        out_specs=[pl.BlockSpec((HEADS, kdim, BLOCK), lambda i: (0, 0, i)),
                   pl.BlockSpec((HEADS, BLOCK, kdim), lambda i: (0, i, 0)),
                   pl.BlockSpec((HEADS, V_ROWS, BLOCK), lambda i: (0, 0, i)),
                   pl.BlockSpec((HEADS, 1, BLOCK), lambda i: (0, 0, i)),
                   pl.BlockSpec((2 * HEADS, LANES), lambda i: (0, 0))],
        out_shape=[jax.ShapeDtypeStruct((HEADS, kdim, lp), BF16),
                   jax.ShapeDtypeStruct((HEADS, lp, kdim), BF16),
                   jax.ShapeDtypeStruct((HEADS, V_ROWS, lp), BF16),
                   jax.ShapeDtypeStruct((HEADS, 1, lp), F32),
                   jax.ShapeDtypeStruct((2 * HEADS, LANES), F32)],
        scratch_shapes=[pltpu.VMEM((1, HEADS), F32)],
        compiler_params=_params("arbitrary"),
        name="fox_prep",
    )(lf, tri, q, k, v)


def _fox_attn_kernel(safe_ref, kmax_ref, qqt_ref, qn_ref, kk_ref, vt_ref, sg_ref, o_ref,
                     m_ref, acc_ref, ring_ref):
    h = pl.program_id(0)
    i = pl.program_id(1)
    qs = i * ATT_Q
    n_groups = ATT_Q // ATT_GROUP
    n_steps = i * (ATT_Q // ATT_K)

    def scores(item):
        g, ks, nk, _ = item
        gs = slice(g * ATT_GROUP, (g + 1) * ATT_GROUP)
        return jnp.dot(kk_ref[pl.ds(ks, nk), :], qqt_ref[:, gs], preferred_element_type=F32)

    def step_items(j):
        return [(g, pl.multiple_of(j * ATT_K + sum(ATT_KSPLIT[:sub]), ATT_GROUP), nk, False)
                for sub, nk in enumerate(ATT_KSPLIT) for g in range(n_groups)]

    def sweep(fixed_shift):
        acc_ref[...] = jnp.zeros_like(acc_ref)
        if fixed_shift:
            m_ref[...] = qn_ref[...] * kmax_ref[h]
        else:
            m_ref[...] = jnp.full_like(m_ref, -jnp.inf)

        def update(item, s):
            g, ks, nk, masked = item
            gs = slice(g * ATT_GROUP, (g + 1) * ATT_GROUP)
            if masked:
                r = lax.broadcasted_iota(jnp.int32, s.shape, 0)
                c = lax.broadcasted_iota(jnp.int32, s.shape, 1) + g * ATT_GROUP
                s = jnp.where(r <= c, s, NEG_BIG)
            vt = vt_ref[:, pl.ds(ks, nk)]
            if fixed_shift:
                p = jnp.exp2(s - m_ref[:, gs]).astype(BF16)
                acc_ref[:, gs] += jnp.dot(vt, p, preferred_element_type=F32)
            else:
                m_prev = m_ref[:, gs]
                m_new = jnp.maximum(m_prev, jnp.max(s, axis=0, keepdims=True))
                alpha = jnp.exp2(m_prev - m_new)
                p = jnp.exp2((s - m_new).astype(BF16))
                m_ref[:, gs] = m_new
                acc_ref[:, gs] = alpha * acc_ref[:, gs] + jnp.dot(vt, p, preferred_element_type=F32)

        def run(items, first, ahead):
            pending = list(first) if first else [scores(item) for item in items[:ATT_AHEAD]]
            for n, item in enumerate(items):
                if n + ATT_AHEAD < len(items):
                    pending.append(scores(items[n + ATT_AHEAD]))
                else:
                    a = n + ATT_AHEAD - len(items)
                    ring_ref[a] = scores(ahead[a])
                update(item, pending.pop(0))

        run([(g, pl.multiple_of(qs, ATT_Q), (g + 1) * ATT_GROUP, True) for g in range(n_groups)],
            None, step_items(0)[:ATT_AHEAD])

        def body(j, carry):
            run(step_items(j), [ring_ref[a] for a in range(ATT_AHEAD)],
                step_items(jnp.minimum(j + 1, n_steps - 1))[:ATT_AHEAD])
            return carry

        lax.fori_loop(0, n_steps, body, 0)

    @pl.when(safe_ref[h] == 1)
    def _():
        sweep(True)

    @pl.when(safe_ref[h] != 1)
    def _():
        sweep(False)

    o = (acc_ref[:HEAD_DIM, :] / acc_ref[HEAD_DIM:HEAD_DIM + 1, :]).T * sg_ref[...].astype(F32)
    row = qs + lax.broadcasted_iota(jnp.int32, o.shape, 0)
    o_ref[...] = jnp.where(row >= META_PAD, o, 0.0).astype(BF16)


def _fox_attn(qqt, qn, nmax, kk, vt, sg):
    lp = kk.shape[1]
    kdim = kk.shape[2]
    qmax, kmax = nmax[:HEADS, 0], nmax[HEADS:, 0]
    safe = (2.0 * qmax * kmax <= SAFE_SHIFT_RANGE).astype(jnp.int32)
    grid_spec = pltpu.PrefetchScalarGridSpec(
        num_scalar_prefetch=2,
        grid=(HEADS, lp // ATT_Q),
        in_specs=[pl.BlockSpec((None, kdim, ATT_Q), lambda h, i, *_: (h, 0, i)),
                  pl.BlockSpec((None, 1, ATT_Q), lambda h, i, *_: (h, 0, i)),
                  pl.BlockSpec((None, lp, kdim), lambda h, i, *_: (h, 0, 0)),
                  pl.BlockSpec((None, V_ROWS, lp), lambda h, i, *_: (h, 0, 0)),
                  pl.BlockSpec((None, ATT_Q, HEAD_DIM), lambda h, i, *_: (h, i, 0))],
        out_specs=pl.BlockSpec((ATT_Q, HEAD_DIM), lambda h, i, *_: (i, h)),
        scratch_shapes=[pltpu.VMEM((1, ATT_Q), F32), pltpu.VMEM((V_ROWS, ATT_Q), F32),
                        pltpu.VMEM((ATT_AHEAD, ATT_KSPLIT[0], ATT_GROUP), F32)])
    return pl.pallas_call(
        _fox_attn_kernel,
        grid_spec=grid_spec,
        out_shape=jax.ShapeDtypeStruct((lp, D_MODEL), BF16),
        compiler_params=_params("parallel", "arbitrary"),
        name="fox_attn",
    )(safe, kmax, qqt, qn, kk, vt, sg)


def kernel(x, meta_tokens, norm_g, ffn_w_gu, ffn_w_down, lb_logits, hg_w_in, hg_w_out, hg_onorm,
           fox_w_in, fox_b_f, fox_w_out, fox_qnorm, fox_knorm):
    assert x.shape[0] == 1 and x.shape[2] == D_MODEL
    seq = x.shape[1]
    real = META_PAD + N_META + seq
    lp = -(-real // ATT_Q) * ATT_Q
    hp = jnp.concatenate([jnp.zeros((META_PAD, D_MODEL), F32), meta_tokens.astype(F32), x[0],
                          jnp.zeros((lp - real, D_MODEL), F32)], axis=0)
    wgu, wd = _cast_ffn_weights(ffn_w_gu.reshape(2 * DEPTH, D_MODEL, 2 * FFN_DIM),
                                ffn_w_down.reshape(2 * DEPTH, FFN_DIM, D_MODEL))
    for layer in range(DEPTH):
        hp = _ffn(hp, norm_g[layer, 0], wgu, wd, 2 * layer)
        j = layer // 2
        if layer % 2 == 0:
            q, k, lf, v, sg = _hgrn_in(hp, norm_g[layer, 1], hg_w_in[j], lb_logits, layer)
            og = _hgrn_rec(q, k, lf, v, sg, hg_onorm[j])
            hp = _out_proj(hp, og, hg_w_out[j])
        else:
            q, k, v, sg, lf = _fox_in(hp, norm_g[layer, 1], fox_w_in[j], fox_b_f[j],
                                      fox_qnorm[j], fox_knorm[j])
            qqt, kk, vt, qn, nmax = _fox_prep(lf, q, k, v)
            og = _fox_attn(qqt, qn, nmax, kk, vt, sg)
            hp = _out_proj(hp, og, fox_w_out[j])
        hp = _ffn(hp, norm_g[layer, 2], wgu, wd, 2 * layer + 1)
    return hp[META_PAD + N_META:META_PAD + N_META + seq][None]
```

```python
import functools

import numpy as np
import jax
import jax.numpy as jnp
from jax import lax
from jax.experimental import pallas as pl
from jax.experimental.pallas import tpu as pltpu

F32 = jnp.float32
BF16 = jnp.bfloat16

D_MODEL = 2048
DEPTH = 2
N_META = 16
BLOCK = 128
META_PAD = (-N_META) % BLOCK
RMS_EPS = 1e-6
HEADS = 16
HEAD_DIM = 128
FFN_DIM = 5504
NEG_BIG = -1e30
LOG2E = 1.4426950408889634

LANES = 128
ROW_TILE = 640
FFN_TILE = 512
FFN_PAD = -(-FFN_DIM // FFN_TILE) * FFN_TILE
PROJ_TILE = 512
CAST_ROWS = 256
ATT_Q = 1280
ATT_K = 1280
ATT_KSPLIT = (512, 768)
ATT_GROUP = 256
ATT_AHEAD = 2
V_ROWS = HEAD_DIM + 16
NORM_SLACK = 1.01
SAFE_SHIFT_RANGE = 80.0
VMEM_LIMIT = 56 * 1024 * 1024

_NT = (((1,), (1,)), ((), ()))


def _params(*sem):
    return pltpu.CompilerParams(dimension_semantics=sem, vmem_limit_bytes=VMEM_LIMIT)


def _sigmoid(x):
    return 1.0 / (1.0 + jnp.exp(-x))


def _rms_rows(x):
    return x * lax.rsqrt(jnp.mean(x * x, axis=-1, keepdims=True) + RMS_EPS)


def _split3(x):
    hi = x.astype(BF16)
    r1 = x - hi.astype(F32)
    mid = r1.astype(BF16)
    lo = (r1 - mid.astype(F32)).astype(BF16)
    return hi, mid, lo


def _cast_wgu_kernel(w_ref, o_ref):
    zeros = jnp.zeros((CAST_ROWS, FFN_PAD - FFN_DIM), BF16)
    o_ref[:, :FFN_DIM] = w_ref[:, :FFN_DIM].astype(BF16)
    o_ref[:, FFN_DIM:FFN_PAD] = zeros
    o_ref[:, FFN_PAD:FFN_PAD + FFN_DIM] = w_ref[:, FFN_DIM:].astype(BF16)
    o_ref[:, FFN_PAD + FFN_DIM:] = zeros


def _cast_wd_kernel(w_ref, o_ref):
    row = pl.program_id(1) * FFN_TILE + lax.broadcasted_iota(jnp.int32, (FFN_TILE, D_MODEL), 0)
    o_ref[...] = jnp.where(row < FFN_DIM, w_ref[...], 0.0).astype(BF16)


def _cast_kernel(w_ref, o_ref):
    o_ref[...] = w_ref[...].astype(BF16)


def _cast_split_kernel(w_ref, o_ref, t_ref):
    n = o_ref.shape[-1]
    o_ref[...] = w_ref[:, :n].astype(BF16)
    t_ref[...] = w_ref[:, n:].astype(BF16)


def _cast_ffn_weights(w_gu, w_down):
    n = w_gu.shape[0]
    wgu = pl.pallas_call(
        _cast_wgu_kernel,
        grid=(n, D_MODEL // CAST_ROWS),
        in_specs=[pl.BlockSpec((None, CAST_ROWS, 2 * FFN_DIM), lambda a, r: (a, r, 0))],
        out_specs=pl.BlockSpec((None, CAST_ROWS, 2 * FFN_PAD), lambda a, r: (a, r, 0)),
        out_shape=jax.ShapeDtypeStruct((n, D_MODEL, 2 * FFN_PAD), BF16),
        compiler_params=_params("parallel", "parallel"),
        name="cast_wgu",
    )(w_gu)
    wd = pl.pallas_call(
        _cast_wd_kernel,
        grid=(n, FFN_PAD // FFN_TILE),
        in_specs=[pl.BlockSpec((None, FFN_TILE, D_MODEL), lambda a, r: (a, r, 0))],
        out_specs=pl.BlockSpec((None, FFN_TILE, D_MODEL), lambda a, r: (a, r, 0)),
        out_shape=jax.ShapeDtypeStruct((n, FFN_PAD, D_MODEL), BF16),
        compiler_params=_params("parallel", "parallel"),
        name="cast_wd",
    )(w_down)
    return wgu, wd


def _cast_proj_weights(w, n_main):
    width = w.shape[1]
    in_spec = pl.BlockSpec((CAST_ROWS, width), lambda r: (r, 0))
    main_spec = pl.BlockSpec((CAST_ROWS, n_main), lambda r: (r, 0))
    main = jax.ShapeDtypeStruct((D_MODEL, n_main), BF16)
    if width == n_main:
        return pl.pallas_call(
            _cast_kernel, grid=(D_MODEL // CAST_ROWS,), in_specs=[in_spec], out_specs=main_spec,
            out_shape=main, compiler_params=_params("parallel"), name="cast_proj")(w)
    return pl.pallas_call(
        _cast_split_kernel, grid=(D_MODEL // CAST_ROWS,), in_specs=[in_spec],
        out_specs=[main_spec, pl.BlockSpec((CAST_ROWS, width - n_main), lambda r: (r, 0))],
        out_shape=[main, jax.ShapeDtypeStruct((D_MODEL, width - n_main), BF16)],
        compiler_params=_params("parallel"), name="cast_proj_split")(w)


ROW_CHUNKS = ROW_TILE // BLOCK


def _ffn_kernel(*refs, shift, n_src):
    chunk_refs = refs[:ROW_CHUNKS]
    head_ref, g_ref, wg_ref, wu_ref, wd_ref, o_ref, xn_ref = refs[ROW_CHUNKS:]

    @pl.when(pl.program_id(1) == 0)
    def _():
        parts = []
        for u, ref in enumerate(chunk_refs):
            idx = ROW_CHUNKS * pl.program_id(0) + u + shift
            blk = ref[...]
            if shift < 0:
                blk = jnp.where(idx < 0, head_ref[...], jnp.where(idx >= n_src, 0.0, blk))
            parts.append(blk)
        x = jnp.concatenate(parts, axis=0)
        xn_ref[...] = (_rms_rows(x) * g_ref[...]).astype(BF16)
        o_ref[...] = x

    xn = xn_ref[...]
    a = jnp.dot(xn, wg_ref[...], preferred_element_type=F32)
    b = jnp.dot(xn, wu_ref[...], preferred_element_type=F32)
    act = (a * (0.5 * _sigmoid(a)) * b).astype(BF16)
    o_ref[...] += jnp.dot(act, wd_ref[...], preferred_element_type=F32)


def _ffn(src, head, gain, wgu, wd, a, shift, out_rows):
    n_src = src.shape[0] // BLOCK
    nf = FFN_PAD // FFN_TILE
    chunk = lambda u: pl.BlockSpec(
        (BLOCK, D_MODEL), lambda i, j: (jnp.clip(ROW_CHUNKS * i + u + shift, 0, n_src - 1), 0))
    return pl.pallas_call(
        functools.partial(_ffn_kernel, shift=shift, n_src=n_src),
        grid=(pl.cdiv(out_rows, ROW_TILE), nf),
        in_specs=[chunk(u) for u in range(ROW_CHUNKS)] + [
            pl.BlockSpec((BLOCK, D_MODEL), lambda i, j: (0, 0)),
            pl.BlockSpec((1, D_MODEL), lambda i, j: (0, 0)),
            pl.BlockSpec((None, D_MODEL, FFN_TILE), lambda i, j: (a, 0, j)),
            pl.BlockSpec((None, D_MODEL, FFN_TILE), lambda i, j: (a, 0, j + nf)),
            pl.BlockSpec((None, FFN_TILE, D_MODEL), lambda i, j: (a, j, 0)),
        ],
        out_specs=pl.BlockSpec((ROW_TILE, D_MODEL), lambda i, j: (i, 0)),
        out_shape=jax.ShapeDtypeStruct((out_rows, D_MODEL), F32),
        scratch_shapes=[pltpu.VMEM((ROW_TILE, D_MODEL), BF16)],
        compiler_params=_params("parallel", "arbitrary"),
        name="ffn",
    )(*([src] * ROW_CHUNKS), head, gain.reshape(1, D_MODEL), wgu, wgu, wd)


def _out_proj_kernel(h_ref, o_ref, w_ref, out_ref):
    out_ref[...] = h_ref[...] + jnp.dot(o_ref[...], w_ref[...], preferred_element_type=F32)


def _out_proj(hp, og, w_out):
    lp = hp.shape[0]
    return pl.pallas_call(
        _out_proj_kernel,
        grid=(lp // ROW_TILE,),
        in_specs=[
            pl.BlockSpec((ROW_TILE, D_MODEL), lambda i: (i, 0)),
            pl.BlockSpec((ROW_TILE, D_MODEL), lambda i: (i, 0)),
            pl.BlockSpec((D_MODEL, D_MODEL), lambda i: (0, 0)),
        ],
        out_specs=pl.BlockSpec((ROW_TILE, D_MODEL), lambda i: (i, 0)),
        out_shape=jax.ShapeDtypeStruct((lp, D_MODEL), F32),
        compiler_params=_params("parallel"),
        name="out_proj",
    )(hp, og, w_out.astype(BF16))


def _hgrn_in_kernel(h_ref, g_ref, wq_ref, wf_ref, wi_ref, wg_ref, lbl_ref,
                    q_ref, k_ref, lf_ref, v_ref, sg_ref, xn_ref, *, layer):
    @pl.when(pl.program_id(1) == 0)
    def _():
        xn_ref[...] = (_rms_rows(h_ref[...]) * g_ref[...]).astype(BF16)

    xn = xn_ref[...]
    qp = jnp.dot(xn, wq_ref[...], preferred_element_type=F32)
    fp = jnp.dot(xn, wf_ref[...], preferred_element_type=F32)
    ip = jnp.dot(xn, wi_ref[...], preferred_element_type=F32)
    gp = jnp.dot(xn, wg_ref[...], preferred_element_type=F32)
    lg = lbl_ref[...]
    e = jnp.exp(lg - jnp.max(lg, axis=0, keepdims=True))
    lb = jnp.sum(e[:layer + 1], axis=0, keepdims=True) / jnp.sum(e, axis=0, keepdims=True)
    q = qp * _sigmoid(qp)
    fg = lb + (1.0 - lb) * _sigmoid(fp)
    k = 1.0 - fg
    lf = jnp.log(fg) * LOG2E
    sg = _sigmoid(gp)
    for hh in range(PROJ_TILE // HEAD_DIM):
        sl = slice(hh * HEAD_DIM, (hh + 1) * HEAD_DIM)
        q_ref[hh] = q[:, sl].astype(BF16)
        k_ref[hh] = k[:, sl].astype(BF16)
        lf_ref[hh] = lf[:, sl]
        v_ref[hh] = ip[:, sl].astype(BF16)
        sg_ref[hh] = sg[:, sl].astype(BF16)


def _hgrn_in(hp, gain, w_in, lb_logits, layer):
    lp = hp.shape[0]
    nc = D_MODEL // PROJ_TILE
    hpt = PROJ_TILE // HEAD_DIM
    w = _cast_proj_weights(w_in, 4 * D_MODEL)
    wspec = lambda s: pl.BlockSpec((D_MODEL, PROJ_TILE), lambda i, c, s=s: (0, s * nc + c))
    hm_spec = pl.BlockSpec((hpt, ROW_TILE, HEAD_DIM), lambda i, c: (c, i, 0))
    hm = lambda dt: jax.ShapeDtypeStruct((HEADS, lp, HEAD_DIM), dt)
    return pl.pallas_call(
        functools.partial(_hgrn_in_kernel, layer=layer),
        grid=(lp // ROW_TILE, nc),
        in_specs=[
            pl.BlockSpec((ROW_TILE, D_MODEL), lambda i, c: (i, 0)),
            pl.BlockSpec((1, D_MODEL), lambda i, c: (0, 0)),
            wspec(0), wspec(1), wspec(2), wspec(3),
            pl.BlockSpec((DEPTH + 1, PROJ_TILE), lambda i, c: (0, c)),
        ],
        out_specs=[hm_spec] * 5,
        out_shape=[hm(BF16), hm(BF16), hm(F32), hm(BF16), hm(BF16)],
        scratch_shapes=[pltpu.VMEM((ROW_TILE, D_MODEL), BF16)],
        compiler_params=_params("parallel", "arbitrary"),
        name="hgrn_in",
    )(hp, gain.reshape(1, D_MODEL), w, w, w, w, lb_logits)


_ROW_LEVELS = (64, 32, 16, 8)
_MM_LEVELS = (4, 2)
HGRN_UNROLL = 4


def _hgrn_consts():
    t = np.arange(BLOCK)[:, None]
    j = np.arange(BLOCK)[None, :]
    blocks = [j <= t]
    for b in _MM_LEVELS:
        ref = 2 * b * (t // (2 * b)) + b - 1
        odd = (t // b) % 2 == 1
        blocks.append(np.where(odd, (j > ref) & (j <= t), (j > t) & (j <= ref)))
    g_all = np.concatenate(blocks, axis=0).astype(np.float32)
    masks = []
    for b in _ROW_LEVELS + _MM_LEVELS + (1,):
        masks.append((t // (2 * b) == j // (2 * b)) & ((t // b) % 2 == 1) & ((j // b) % 2 == 0))
    masks.append(t == j)
    return jnp.asarray(g_all, BF16), jnp.asarray(np.stack(masks).astype(np.float32))


def _hgrn_rec_kernel(q_ref, k_ref, lf_ref, v_ref, sg_ref, gmat_ref, mask_ref, on_ref,
                     o_ref, st_ref, ob_ref):
    @pl.when(pl.program_id(0) == 0)
    def _():
        st_ref[...] = jnp.zeros_like(st_ref)

    def boundary_rows(b, size):
        return jnp.concatenate(
            [jnp.broadcast_to(b[s + size - 1:s + size], (2 * size, HEAD_DIM))
             for s in range(0, BLOCK, 2 * size)], axis=0)

    def heads(hg, carry):
        hs = [hg * HGRN_UNROLL + u for u in range(HGRN_UNROLL)]
        lf = [lf_ref[h] for h in hs]
        q = [q_ref[h] for h in hs]
        k = [k_ref[h] for h in hs]
        x3 = [jnp.dot(gmat_ref[...], jnp.concatenate(_split3(l), axis=1),
                      preferred_element_type=F32) for l in lf]
        x = [t[:, :HEAD_DIM] + t[:, HEAD_DIM:2 * HEAD_DIM] + t[:, 2 * HEAD_DIM:] for t in x3]
        b = [t[:BLOCK] for t in x]
        b_last = [t[BLOCK - 1:BLOCK] for t in b]
        att = [jnp.zeros((BLOCK, BLOCK), F32) for _ in hs]
        n_row, n_mm = len(_ROW_LEVELS), len(_MM_LEVELS)
        for lv in range(n_row + n_mm + 2):
            for u in range(HGRN_UNROLL):
                if lv < n_row:
                    d = b[u] - boundary_rows(b[u], _ROW_LEVELS[lv])
                    el = jnp.exp2(-jnp.abs(d)).astype(BF16)
                    qt, kt = q[u] * el, k[u] * el
                elif lv < n_row + n_mm:
                    m = lv - n_row + 1
                    el = jnp.exp2(x[u][m * BLOCK:(m + 1) * BLOCK]).astype(BF16)
                    qt, kt = q[u] * el, k[u] * el
                elif lv == n_row + n_mm:
                    qt, kt = q[u] * jnp.exp2(lf[u]).astype(BF16), k[u]
                else:
                    qt, kt = q[u], k[u]
                a = lax.dot_general(qt, kt, _NT, preferred_element_type=F32)
                att[u] = att[u] + mask_ref[lv] * a
        for u, h in enumerate(hs):
            qb = q[u] * jnp.exp2(b[u]).astype(BF16)
            kd = k[u] * jnp.exp2(b_last[u] - b[u]).astype(BF16)
            vt = v_ref[h].astype(F32).T.astype(BF16)
            st = st_ref[h]
            lhs = jnp.concatenate([att[u].astype(BF16), qb], axis=1)
            rhs = jnp.concatenate([vt, st.astype(BF16)], axis=1)
            o = lax.dot_general(lhs, rhs, _NT, preferred_element_type=F32)
            st_ref[h] = st * jnp.exp2(b_last[u]) + jnp.dot(vt, kd, preferred_element_type=F32)
            og = _rms_rows(o) * on_ref[...] * sg_ref[h].astype(F32)
            ob_ref[h] = og.astype(BF16)
        return carry

    lax.fori_loop(0, HEADS // HGRN_UNROLL, heads, 0)
    for h in range(HEADS):
        o_ref[:, h * HEAD_DIM:(h + 1) * HEAD_DIM] = ob_ref[h]


def _hgrn_rec(q, k, lf, v, sg, onorm):
    lp = q.shape[1]
    gmat, masks = _hgrn_consts()
    hm_spec = pl.BlockSpec((HEADS, BLOCK, HEAD_DIM), lambda c: (0, c, 0))
    return pl.pallas_call(
        _hgrn_rec_kernel,
        grid=(lp // BLOCK,),
        in_specs=[hm_spec] * 5 + [
            pl.BlockSpec(gmat.shape, lambda c: (0, 0)),
            pl.BlockSpec(masks.shape, lambda c: (0, 0, 0)),
            pl.BlockSpec((1, HEAD_DIM), lambda c: (0, 0)),
        ],
        out_specs=pl.BlockSpec((BLOCK, D_MODEL), lambda c: (c, 0)),
        out_shape=jax.ShapeDtypeStruct((lp, D_MODEL), BF16),
        scratch_shapes=[pltpu.VMEM((HEADS, HEAD_DIM, HEAD_DIM), F32),
                        pltpu.VMEM((HEADS, BLOCK, HEAD_DIM), BF16)],
        compiler_params=_params("arbitrary"),
        name="hgrn_rec",
    )(q, k, lf, v, sg, gmat, masks, onorm.reshape(1, HEAD_DIM))


def _fox_in_kernel(h_ref, g_ref, wq_ref, wk_ref, wv_ref, wg_ref, wf_ref, bf_ref, qn_ref, kn_ref,
                   q_ref, k_ref, v_ref, sg_ref, lf_ref, xn_ref):
    @pl.when(pl.program_id(1) == 0)
    def _():
        xn = (_rms_rows(h_ref[...]) * g_ref[...]).astype(BF16)
        xn_ref[...] = xn
        z = jnp.dot(xn, wf_ref[...], preferred_element_type=F32) + bf_ref[...]
        lf_ref[...] = jnp.minimum(z, 0.0) - jnp.log(1.0 + jnp.exp(-jnp.abs(z)))

    xn = xn_ref[...]
    qp = jnp.dot(xn, wq_ref[...], preferred_element_type=F32)
    kp = jnp.dot(xn, wk_ref[...], preferred_element_type=F32)
    vp = jnp.dot(xn, wv_ref[...], preferred_element_type=F32)
    gp = jnp.dot(xn, wg_ref[...], preferred_element_type=F32)
    sg = _sigmoid(gp)
    scale = HEAD_DIM ** -0.5 * LOG2E
    for hh in range(PROJ_TILE // HEAD_DIM):
        sl = slice(hh * HEAD_DIM, (hh + 1) * HEAD_DIM)
        q_ref[hh] = (_rms_rows(qp[:, sl]) * (qn_ref[...] * scale)).astype(BF16)
        k_ref[hh] = (_rms_rows(kp[:, sl]) * kn_ref[...]).astype(BF16)
        v_ref[hh] = vp[:, sl].astype(BF16)
        sg_ref[hh] = sg[:, sl].astype(BF16)


def _fox_in(hp, gain, w_in, b_f, qnorm, knorm):
    lp = hp.shape[0]
    nc = D_MODEL // PROJ_TILE
    hpt = PROJ_TILE // HEAD_DIM
    w, wf = _cast_proj_weights(w_in, 4 * D_MODEL)
    wspec = lambda s: pl.BlockSpec((D_MODEL, PROJ_TILE), lambda i, c, s=s: (0, s * nc + c))
    hm_spec = pl.BlockSpec((hpt, ROW_TILE, HEAD_DIM), lambda i, c: (c, i, 0))
    hm = jax.ShapeDtypeStruct((HEADS, lp, HEAD_DIM), BF16)
    vec = lambda n: pl.BlockSpec((1, n), lambda i, c: (0, 0))
    return pl.pallas_call(
        _fox_in_kernel,
        grid=(lp // ROW_TILE, nc),
        in_specs=[
            pl.BlockSpec((ROW_TILE, D_MODEL), lambda i, c: (i, 0)),
            vec(D_MODEL),
            wspec(0), wspec(1), wspec(2), wspec(3),
            pl.BlockSpec((D_MODEL, HEADS), lambda i, c: (0, 0)),
            vec(HEADS), vec(HEAD_DIM), vec(HEAD_DIM),
        ],
        out_specs=[hm_spec] * 4 + [pl.BlockSpec((ROW_TILE, HEADS), lambda i, c: (i, 0))],
        out_shape=[hm, hm, hm, hm, jax.ShapeDtypeStruct((lp, HEADS), F32)],
        scratch_shapes=[pltpu.VMEM((ROW_TILE, D_MODEL), BF16)],
        compiler_params=_params("parallel", "arbitrary"),
        name="fox_in",
    )(hp, gain.reshape(1, D_MODEL), w, w, w, w, wf, b_f.reshape(1, HEADS),
      qnorm.reshape(1, HEAD_DIM), knorm.reshape(1, HEAD_DIM))


def _fox_prep_consts():
    pq = np.zeros((HEADS, LANES, 4 * HEADS), np.float32)
    pk = np.zeros((HEADS, 4 * HEADS, LANES), np.float32)
    one, pad = 3 * HEADS, 3 * HEADS + 1
    for h in range(HEADS):
        for part in range(3):
            pq[h, part, part * HEADS + h] = 1.0
            pq[h, 3 + part, one] = 1.0
            pk[h, one, part] = 1.0
            pk[h, part * HEADS + h, 3 + part] = -1.0
        pk[h, pad, 3] = NEG_BIG
    return jnp.asarray(pq, BF16), jnp.asarray(pk, BF16)


def _fox_prep_kernel(lf_ref, tri_ref, pq_ref, pk_ref, q_ref, k_ref, v_ref,
                     qqt_ref, kk_ref, vt_ref, qn_ref, nmax_ref, carry_ref):
    i = pl.program_id(0)

    @pl.when(i == 0)
    def _():
        carry_ref[...] = jnp.zeros_like(carry_ref)
        nmax_ref[...] = jnp.zeros_like(nmax_ref)

    row16 = i * BLOCK + lax.broadcasted_iota(jnp.int32, (BLOCK, HEADS), 0)
    lf = jnp.where(row16 >= META_PAD, lf_ref[...], 0.0)
    tri = tri_ref[...]
    c = carry_ref[...]
    for part in _split3(lf):
        c = c + jnp.dot(tri, part, preferred_element_type=F32)
    carry_ref[...] = c[BLOCK - 1:BLOCK]
    col16 = lax.broadcasted_iota(jnp.int32, (BLOCK, HEADS), 1)
    extra = jnp.where(col16 == 0, 1.0, jnp.where((col16 == 1) & (row16 < META_PAD), 1.0, 0.0))
    src = jnp.concatenate([p.astype(F32) for p in _split3(c * LOG2E)] + [extra], axis=1)
    src_t = jnp.concatenate([src, jnp.zeros_like(src)], axis=1).T[:4 * HEADS].astype(BF16)
    src = src.astype(BF16)
    ones = jnp.ones((HEAD_DIM, LANES), BF16)
    for h in range(HEADS):
        aq_t = jnp.dot(pq_ref[h], src_t, preferred_element_type=F32)
        ak = jnp.dot(src, pk_ref[h], preferred_element_type=F32)
        qt = q_ref[h].astype(F32).T
        qqt_ref[h, :HEAD_DIM, :] = qt.astype(BF16)
        qqt_ref[h, HEAD_DIM:, :] = aq_t.astype(BF16)
        qn = jnp.sqrt(jnp.sum(qt * qt, axis=0, keepdims=True))
        qn_ref[h] = qn
        kf = k_ref[h].astype(F32)
        kn2 = jnp.dot((kf * kf).astype(BF16), ones, preferred_element_type=F32)
        kn = jnp.sqrt(jnp.max(kn2, axis=0, keepdims=True)) * NORM_SLACK
        nmax_ref[h:h + 1, :] = jnp.maximum(nmax_ref[h:h + 1, :], qn)
        nmax_ref[HEADS + h:HEADS + h + 1, :] = jnp.maximum(nmax_ref[HEADS + h:HEADS + h + 1, :], kn)
        kk_ref[h, :, :HEAD_DIM] = k_ref[h]
        kk_ref[h, :, HEAD_DIM:] = ak.astype(BF16)
        vt_ref[h, :HEAD_DIM, :] = v_ref[h].astype(F32).T.astype(BF16)
        vt_ref[h, HEAD_DIM:, :] = jnp.ones((V_ROWS - HEAD_DIM, BLOCK), BF16)


def _fox_prep(lf, q, k, v):
    lp = lf.shape[0]
    kdim = HEAD_DIM + LANES
    tri = jnp.asarray(np.tril(np.ones((BLOCK, BLOCK), np.float32)), BF16)
    pq, pk = _fox_prep_consts()
    hm_spec = pl.BlockSpec((HEADS, BLOCK, HEAD_DIM), lambda i: (0, i, 0))
    return pl.pallas_call(
        _fox_prep_kernel,
        grid=(lp // BLOCK,),
        in_specs=[pl.BlockSpec((BLOCK, HEADS), lambda i: (i, 0)),
                  pl.BlockSpec((BLOCK, BLOCK), lambda i: (0, 0)),
                  pl.BlockSpec(pq.shape, lambda i: (0, 0, 0)),
                  pl.BlockSpec(pk.shape, lambda i: (0, 0, 0)),
                  hm_spec, hm_spec, hm_spec],
        out_specs=[pl.BlockSpec((HEADS, kdim, BLOCK), lambda i: (0, 0, i)),
                   pl.BlockSpec((HEADS, BLOCK, kdim), lambda i: (0, i, 0)),
                   pl.BlockSpec((HEADS, V_ROWS, BLOCK), lambda i: (0, 0, i)),
                   pl.BlockSpec((HEADS, 1, BLOCK), lambda i: (0, 0, i)),
                   pl.BlockSpec((2 * HEADS, LANES), lambda i: (0, 0))],
        out_shape=[jax.ShapeDtypeStruct((HEADS, kdim, lp), BF16),
                   jax.ShapeDtypeStruct((HEADS, lp, kdim), BF16),
                   jax.ShapeDtypeStruct((HEADS, V_ROWS, lp), BF16),
                   jax.ShapeDtypeStruct((HEADS, 1, lp), F32),
                   jax.ShapeDtypeStruct((2 * HEADS, LANES), F32)],
        scratch_shapes=[pltpu.VMEM((1, HEADS), F32)],
        compiler_params=_params("arbitrary"),
        name="fox_prep",
    )(lf, tri, pq, pk, q, k, v)


def _fox_attn_kernel(safe_ref, kmax_ref, qqt_ref, qn_ref, kk_ref, vt_ref, sg_ref, o_ref,
                     m_ref, acc_ref, ring_ref):
    h = pl.program_id(0)
    i = pl.program_id(1)
    qs = i * ATT_Q
    n_groups = ATT_Q // ATT_GROUP
    n_steps = i * (ATT_Q // ATT_K)

    def scores(item):
        g, ks, nk, _ = item
        gs = slice(g * ATT_GROUP, (g + 1) * ATT_GROUP)
        return jnp.dot(kk_ref[pl.ds(ks, nk), :], qqt_ref[:, gs], preferred_element_type=F32)

    def step_items(j):
        return [(g, pl.multiple_of(j * ATT_K + sum(ATT_KSPLIT[:sub]), ATT_GROUP), nk, False)
                for sub, nk in enumerate(ATT_KSPLIT) for g in range(n_groups)]

    def sweep(fixed_shift):
        acc_ref[...] = jnp.zeros_like(acc_ref)
        if fixed_shift:
            m_ref[...] = qn_ref[...] * kmax_ref[h]
        else:
            m_ref[...] = jnp.full_like(m_ref, -jnp.inf)

        def update(item, s):
            g, ks, nk, masked = item
            gs = slice(g * ATT_GROUP, (g + 1) * ATT_GROUP)
            if masked:
                r = lax.broadcasted_iota(jnp.int32, s.shape, 0)
                c = lax.broadcasted_iota(jnp.int32, s.shape, 1) + g * ATT_GROUP
                s = jnp.where(r <= c, s, NEG_BIG)
            vt = vt_ref[:, pl.ds(ks, nk)]
            if fixed_shift:
                p = jnp.exp2(s - m_ref[:, gs]).astype(BF16)
                acc_ref[:, gs] += jnp.dot(vt, p, preferred_element_type=F32)
            else:
                m_prev = m_ref[:, gs]
                m_new = jnp.maximum(m_prev, jnp.max(s, axis=0, keepdims=True))
                alpha = jnp.exp2(m_prev - m_new)
                p = jnp.exp2((s - m_new).astype(BF16))
                m_ref[:, gs] = m_new
                acc_ref[:, gs] = alpha * acc_ref[:, gs] + jnp.dot(vt, p, preferred_element_type=F32)

        def run(items, first, ahead):
            pending = list(first) if first else [scores(item) for item in items[:ATT_AHEAD]]
            for n, item in enumerate(items):
                if n + ATT_AHEAD < len(items):
                    pending.append(scores(items[n + ATT_AHEAD]))
                else:
                    a = n + ATT_AHEAD - len(items)
                    ring_ref[a] = scores(ahead[a])
                update(item, pending.pop(0))

        run([(g, pl.multiple_of(qs, ATT_Q), (g + 1) * ATT_GROUP, True) for g in range(n_groups)],
            None, step_items(0)[:ATT_AHEAD])

        def body(j, carry):
            run(step_items(j), [ring_ref[a] for a in range(ATT_AHEAD)],
                step_items(jnp.minimum(j + 1, n_steps - 1))[:ATT_AHEAD])
            return carry

        lax.fori_loop(0, n_steps, body, 0)

    @pl.when(safe_ref[h] == 1)
    def _():
        sweep(True)

    @pl.when(safe_ref[h] != 1)
    def _():
        sweep(False)

    o = (acc_ref[:HEAD_DIM, :] / acc_ref[HEAD_DIM:HEAD_DIM + 1, :]).T * sg_ref[...].astype(F32)
    row = qs + lax.broadcasted_iota(jnp.int32, o.shape, 0)
    o_ref[...] = jnp.where(row >= META_PAD, o, 0.0).astype(BF16)


def _fox_attn(qqt, qn, nmax, kk, vt, sg):
    lp = kk.shape[1]
    kdim = kk.shape[2]
    qmax, kmax = jnp.max(nmax[:HEADS], axis=1), jnp.max(nmax[HEADS:], axis=1)
    safe = (2.0 * qmax * kmax <= SAFE_SHIFT_RANGE).astype(jnp.int32)
    grid_spec = pltpu.PrefetchScalarGridSpec(
        num_scalar_prefetch=2,
        grid=(HEADS, lp // ATT_Q),
        in_specs=[pl.BlockSpec((None, kdim, ATT_Q), lambda h, i, *_: (h, 0, i)),
                  pl.BlockSpec((None, 1, ATT_Q), lambda h, i, *_: (h, 0, i)),
                  pl.BlockSpec((None, lp, kdim), lambda h, i, *_: (h, 0, 0)),
                  pl.BlockSpec((None, V_ROWS, lp), lambda h, i, *_: (h, 0, 0)),
                  pl.BlockSpec((None, ATT_Q, HEAD_DIM), lambda h, i, *_: (h, i, 0))],
        out_specs=pl.BlockSpec((ATT_Q, HEAD_DIM), lambda h, i, *_: (i, h)),
        scratch_shapes=[pltpu.VMEM((1, ATT_Q), F32), pltpu.VMEM((V_ROWS, ATT_Q), F32),
                        pltpu.VMEM((ATT_AHEAD, ATT_KSPLIT[0], ATT_GROUP), F32)])
    return pl.pallas_call(
        _fox_attn_kernel,
        grid_spec=grid_spec,
        out_shape=jax.ShapeDtypeStruct((lp, D_MODEL), BF16),
        compiler_params=_params("parallel", "arbitrary"),
        name="fox_attn",
    )(safe, kmax, qqt, qn, kk, vt, sg)


def kernel(x, meta_tokens, norm_g, ffn_w_gu, ffn_w_down, lb_logits, hg_w_in, hg_w_out, hg_onorm,
           fox_w_in, fox_b_f, fox_w_out, fox_qnorm, fox_knorm):
    assert x.shape[0] == 1 and x.shape[2] == D_MODEL
    seq = x.shape[1]
    real = META_PAD + N_META + seq
    lp = -(-real // ATT_Q) * ATT_Q
    assert seq % BLOCK == 0
    head = jnp.concatenate([jnp.zeros((META_PAD, D_MODEL), F32), meta_tokens.astype(F32)], axis=0)
    wgu, wd = _cast_ffn_weights(ffn_w_gu.reshape(2 * DEPTH, D_MODEL, 2 * FFN_DIM),
                                ffn_w_down.reshape(2 * DEPTH, FFN_DIM, D_MODEL))
    hp = x[0]
    for layer in range(DEPTH):
        hp = _ffn(hp, head, norm_g[layer, 0], wgu, wd, 2 * layer, -1 if layer == 0 else 0, lp)
        j = layer // 2
        if layer % 2 == 0:
            q, k, lf, v, sg = _hgrn_in(hp, norm_g[layer, 1], hg_w_in[j], lb_logits, layer)
            og = _hgrn_rec(q, k, lf, v, sg, hg_onorm[j])
            hp = _out_proj(hp, og, hg_w_out[j])
        else:
            q, k, v, sg, lf = _fox_in(hp, norm_g[layer, 1], fox_w_in[j], fox_b_f[j],
                                      fox_qnorm[j], fox_knorm[j])
            qqt, kk, vt, qn, nmax = _fox_prep(lf, q, k, v)
            og = _fox_attn(qqt, qn, nmax, kk, vt, sg)
            hp = _out_proj(hp, og, fox_w_out[j])
        last = layer == DEPTH - 1
        hp = _ffn(hp, head, norm_g[layer, 2], wgu, wd, 2 * layer + 1,
                  1 if last else 0, seq if last else lp)
    return hp[None]
```

```python
import functools

import numpy as np
import jax
import jax.numpy as jnp
from jax import lax
from jax.experimental import pallas as pl
from jax.experimental.pallas import tpu as pltpu

F32 = jnp.float32
BF16 = jnp.bfloat16

D_MODEL = 2048
DEPTH = 2
N_META = 16
BLOCK = 128
META_PAD = (-N_META) % BLOCK
RMS_EPS = 1e-6
HEADS = 16
HEAD_DIM = 128
FFN_DIM = 5504
NEG_BIG = -1e30
LOG2E = 1.4426950408889634

LANES = 128
ROW_TILE = 640
FFN_TILE = 512
FFN_PAD = -(-FFN_DIM // FFN_TILE) * FFN_TILE
PROJ_TILE = 512
CAST_ROWS = 256
ATT_Q = 1280
ATT_K = 1280
ATT_KSPLIT = (512, 768)
ATT_GROUP = 256
ATT_AHEAD = 2
V_ROWS = HEAD_DIM + 16
NORM_SLACK = 1.01
SAFE_SHIFT_RANGE = 80.0
VMEM_LIMIT = 56 * 1024 * 1024

_NT = (((1,), (1,)), ((), ()))


def _params(*sem):
    return pltpu.CompilerParams(dimension_semantics=sem, vmem_limit_bytes=VMEM_LIMIT)


def _sigmoid(x):
    return 1.0 / (1.0 + jnp.exp(-x))


def _rms_rows(x):
    return x * lax.rsqrt(jnp.mean(x * x, axis=-1, keepdims=True) + RMS_EPS)


def _split3(x):
    hi = x.astype(BF16)
    r1 = x - hi.astype(F32)
    mid = r1.astype(BF16)
    lo = (r1 - mid.astype(F32)).astype(BF16)
    return hi, mid, lo


def _cast_wgu_kernel(w_ref, o_ref):
    zeros = jnp.zeros((CAST_ROWS, FFN_PAD - FFN_DIM), BF16)
    o_ref[:, :FFN_DIM] = w_ref[:, :FFN_DIM].astype(BF16)
    o_ref[:, FFN_DIM:FFN_PAD] = zeros
    o_ref[:, FFN_PAD:FFN_PAD + FFN_DIM] = w_ref[:, FFN_DIM:].astype(BF16)
    o_ref[:, FFN_PAD + FFN_DIM:] = zeros


def _cast_wd_kernel(w_ref, o_ref):
    row = pl.program_id(1) * FFN_TILE + lax.broadcasted_iota(jnp.int32, (FFN_TILE, D_MODEL), 0)
    o_ref[...] = jnp.where(row < FFN_DIM, w_ref[...], 0.0).astype(BF16)


def _cast_kernel(w_ref, o_ref):
    o_ref[...] = w_ref[...].astype(BF16)


def _cast_split_kernel(w_ref, o_ref, t_ref):
    n = o_ref.shape[-1]
    o_ref[...] = w_ref[:, :n].astype(BF16)
    t_ref[...] = w_ref[:, n:].astype(BF16)


def _cast_ffn_weights(w_gu, w_down):
    n = w_gu.shape[0]
    wgu = pl.pallas_call(
        _cast_wgu_kernel,
        grid=(n, D_MODEL // CAST_ROWS),
        in_specs=[pl.BlockSpec((None, CAST_ROWS, 2 * FFN_DIM), lambda a, r: (a, r, 0))],
        out_specs=pl.BlockSpec((None, CAST_ROWS, 2 * FFN_PAD), lambda a, r: (a, r, 0)),
        out_shape=jax.ShapeDtypeStruct((n, D_MODEL, 2 * FFN_PAD), BF16),
        compiler_params=_params("parallel", "parallel"),
        name="cast_wgu",
    )(w_gu)
    wd = pl.pallas_call(
        _cast_wd_kernel,
        grid=(n, FFN_PAD // FFN_TILE),
        in_specs=[pl.BlockSpec((None, FFN_TILE, D_MODEL), lambda a, r: (a, r, 0))],
        out_specs=pl.BlockSpec((None, FFN_TILE, D_MODEL), lambda a, r: (a, r, 0)),
        out_shape=jax.ShapeDtypeStruct((n, FFN_PAD, D_MODEL), BF16),
        compiler_params=_params("parallel", "parallel"),
        name="cast_wd",
    )(w_down)
    return wgu, wd


def _cast_proj_weights(w, j, n_main):
    width = w.shape[2]
    in_spec = pl.BlockSpec((None, CAST_ROWS, width), lambda r: (j, r, 0))
    main_spec = pl.BlockSpec((CAST_ROWS, n_main), lambda r: (r, 0))
    main = jax.ShapeDtypeStruct((D_MODEL, n_main), BF16)
    if width == n_main:
        return pl.pallas_call(
            _cast_kernel, grid=(D_MODEL // CAST_ROWS,), in_specs=[in_spec], out_specs=main_spec,
            out_shape=main, compiler_params=_params("parallel"), name="cast_proj")(w)
    return pl.pallas_call(
        _cast_split_kernel, grid=(D_MODEL // CAST_ROWS,), in_specs=[in_spec],
        out_specs=[main_spec, pl.BlockSpec((CAST_ROWS, width - n_main), lambda r: (r, 0))],
        out_shape=[main, jax.ShapeDtypeStruct((D_MODEL, width - n_main), BF16)],
        compiler_params=_params("parallel"), name="cast_proj_split")(w)


ROW_CHUNKS = ROW_TILE // BLOCK


def _ffn_kernel(*refs, shift, n_src):
    n_in = ROW_CHUNKS if shift else 1
    chunk_refs = refs[:n_in]
    head_ref, g_ref, wg_ref, wu_ref, wd_ref, o_ref, xn_ref = refs[n_in:]

    @pl.when(pl.program_id(1) == 0)
    def _():
        parts = []
        for u, ref in enumerate(chunk_refs):
            idx = ROW_CHUNKS * pl.program_id(0) + u + shift
            blk = ref[...]
            if shift < 0:
                blk = jnp.where(idx < 0, head_ref[...], jnp.where(idx >= n_src, 0.0, blk))
            parts.append(blk)
        x = jnp.concatenate(parts, axis=0) if shift else parts[0]
        xn_ref[...] = (_rms_rows(x) * g_ref[...]).astype(BF16)
        o_ref[...] = x

    xn = xn_ref[...]
    a = jnp.dot(xn, wg_ref[...], preferred_element_type=F32)
    b = jnp.dot(xn, wu_ref[...], preferred_element_type=F32)
    act = (a * (0.5 * _sigmoid(a)) * b).astype(BF16)
    o_ref[...] += jnp.dot(act, wd_ref[...], preferred_element_type=F32)


def _ffn(src, head, gain, wgu, wd, a, shift, out_rows):
    n_src = src.shape[0] // BLOCK
    nf = FFN_PAD // FFN_TILE
    chunk = lambda u: pl.BlockSpec(
        (BLOCK, D_MODEL), lambda i, j: (jnp.clip(ROW_CHUNKS * i + u + shift, 0, n_src - 1), 0))
    row_specs = ([chunk(u) for u in range(ROW_CHUNKS)] if shift else
                 [pl.BlockSpec((ROW_TILE, D_MODEL), lambda i, j: (i, 0))])
    return pl.pallas_call(
        functools.partial(_ffn_kernel, shift=shift, n_src=n_src),
        grid=(pl.cdiv(out_rows, ROW_TILE), nf),
        in_specs=row_specs + [
            pl.BlockSpec((BLOCK, D_MODEL), lambda i, j: (0, 0)),
            pl.BlockSpec((1, D_MODEL), lambda i, j: (0, 0)),
            pl.BlockSpec((None, D_MODEL, FFN_TILE), lambda i, j: (a, 0, j)),
            pl.BlockSpec((None, D_MODEL, FFN_TILE), lambda i, j: (a, 0, j + nf)),
            pl.BlockSpec((None, FFN_TILE, D_MODEL), lambda i, j: (a, j, 0)),
        ],
        out_specs=pl.BlockSpec((ROW_TILE, D_MODEL), lambda i, j: (i, 0)),
        out_shape=jax.ShapeDtypeStruct((out_rows, D_MODEL), F32),
        scratch_shapes=[pltpu.VMEM((ROW_TILE, D_MODEL), BF16)],
        compiler_params=_params("parallel", "arbitrary"),
        name="ffn",
    )(*([src] * len(row_specs)), head, gain.reshape(1, D_MODEL), wgu, wgu, wd)


def _out_proj_kernel(h_ref, o_ref, w_ref, out_ref):
    out_ref[...] = h_ref[...] + jnp.dot(o_ref[...], w_ref[...], preferred_element_type=F32)


def _out_proj(hp, og, w_out):
    lp = hp.shape[0]
    return pl.pallas_call(
        _out_proj_kernel,
        grid=(lp // ROW_TILE,),
        in_specs=[
            pl.BlockSpec((ROW_TILE, D_MODEL), lambda i: (i, 0)),
            pl.BlockSpec((ROW_TILE, D_MODEL), lambda i: (i, 0)),
            pl.BlockSpec((D_MODEL, D_MODEL), lambda i: (0, 0)),
        ],
        out_specs=pl.BlockSpec((ROW_TILE, D_MODEL), lambda i: (i, 0)),
        out_shape=jax.ShapeDtypeStruct((lp, D_MODEL), F32),
        compiler_params=_params("parallel"),
        name="out_proj",
    )(hp, og, w_out.astype(BF16))


def _hgrn_in_kernel(h_ref, g_ref, wq_ref, wf_ref, wi_ref, wg_ref, lbl_ref,
                    q_ref, k_ref, lf_ref, v_ref, sg_ref, xn_ref, *, layer):
    @pl.when(pl.program_id(1) == 0)
    def _():
        xn_ref[...] = (_rms_rows(h_ref[...]) * g_ref[...]).astype(BF16)

    xn = xn_ref[...]
    qp = jnp.dot(xn, wq_ref[...], preferred_element_type=F32)
    fp = jnp.dot(xn, wf_ref[...], preferred_element_type=F32)
    ip = jnp.dot(xn, wi_ref[...], preferred_element_type=F32)
    gp = jnp.dot(xn, wg_ref[...], preferred_element_type=F32)
    lg = lbl_ref[...]
    e = jnp.exp(lg - jnp.max(lg, axis=0, keepdims=True))
    lb = jnp.sum(e[:layer + 1], axis=0, keepdims=True) / jnp.sum(e, axis=0, keepdims=True)
    q = qp * _sigmoid(qp)
    fg = lb + (1.0 - lb) * _sigmoid(fp)
    k = 1.0 - fg
    lf = jnp.log(fg) * LOG2E
    sg = _sigmoid(gp)
    for hh in range(PROJ_TILE // HEAD_DIM):
        sl = slice(hh * HEAD_DIM, (hh + 1) * HEAD_DIM)
        q_ref[hh] = q[:, sl].astype(BF16)
        k_ref[hh] = k[:, sl].astype(BF16)
        lf_ref[hh] = lf[:, sl]
        v_ref[hh] = ip[:, sl].astype(BF16)
        sg_ref[hh] = sg[:, sl].astype(BF16)


def _hgrn_in(hp, gain, w_in, j, lb_logits, layer):
    lp = hp.shape[0]
    nc = D_MODEL // PROJ_TILE
    hpt = PROJ_TILE // HEAD_DIM
    w = _cast_proj_weights(w_in, j, 4 * D_MODEL)
    wspec = lambda s: pl.BlockSpec((D_MODEL, PROJ_TILE), lambda i, c, s=s: (0, s * nc + c))
    hm_spec = pl.BlockSpec((hpt, ROW_TILE, HEAD_DIM), lambda i, c: (c, i, 0))
    hm = lambda dt: jax.ShapeDtypeStruct((HEADS, lp, HEAD_DIM), dt)
    return pl.pallas_call(
        functools.partial(_hgrn_in_kernel, layer=layer),
        grid=(lp // ROW_TILE, nc),
        in_specs=[
            pl.BlockSpec((ROW_TILE, D_MODEL), lambda i, c: (i, 0)),
            pl.BlockSpec((1, D_MODEL), lambda i, c: (0, 0)),
            wspec(0), wspec(1), wspec(2), wspec(3),
            pl.BlockSpec((DEPTH + 1, PROJ_TILE), lambda i, c: (0, c)),
        ],
        out_specs=[hm_spec] * 5,
        out_shape=[hm(BF16), hm(BF16), hm(F32), hm(BF16), hm(BF16)],
        scratch_shapes=[pltpu.VMEM((ROW_TILE, D_MODEL), BF16)],
        compiler_params=_params("parallel", "arbitrary"),
        name="hgrn_in",
    )(hp, gain.reshape(1, D_MODEL), w, w, w, w, lb_logits)


_ROW_LEVELS = (64, 32, 16, 8)
_MM_LEVELS = (4, 2)
HGRN_UNROLL = 4


def _hgrn_consts():
    t = np.arange(BLOCK)[:, None]
    j = np.arange(BLOCK)[None, :]
    blocks = [j <= t]
    for b in _MM_LEVELS:
        ref = 2 * b * (t // (2 * b)) + b - 1
        odd = (t // b) % 2 == 1
        blocks.append(np.where(odd, (j > ref) & (j <= t), (j > t) & (j <= ref)))
    g_all = np.concatenate(blocks, axis=0).astype(np.float32)
    masks = []
    for b in _ROW_LEVELS + _MM_LEVELS + (1,):
        masks.append((t // (2 * b) == j // (2 * b)) & ((t // b) % 2 == 1) & ((j // b) % 2 == 0))
    masks.append(t == j)
    return jnp.asarray(g_all, BF16), jnp.asarray(np.stack(masks), BF16)


def _hgrn_rec_kernel(q_ref, k_ref, lf_ref, v_ref, sg_ref, gmat_ref, mask_ref, on_ref,
                     o_ref, st_ref, ob_ref):
    @pl.when(pl.program_id(0) == 0)
    def _():
        st_ref[...] = jnp.zeros_like(st_ref)

    def boundary_exponent(b, size):
        parts = []
        for s in range(0, BLOCK, 2 * size):
            r = b[s + size - 1:s + size]
            parts += [r - b[s:s + size], b[s + size:s + 2 * size] - r]
        return jnp.concatenate(parts, axis=0)

    def heads(hg, carry):
        hs = [hg * HGRN_UNROLL + u for u in range(HGRN_UNROLL)]
        lf = [lf_ref[h] for h in hs]
        q = [q_ref[h] for h in hs]
        k = [k_ref[h] for h in hs]
        x3 = [jnp.dot(gmat_ref[...], jnp.concatenate(_split3(l), axis=1),
                      preferred_element_type=F32) for l in lf]
        x = [t[:, :HEAD_DIM] + t[:, HEAD_DIM:2 * HEAD_DIM] + t[:, 2 * HEAD_DIM:] for t in x3]
        b = [t[:BLOCK] for t in x]
        b_last = [t[BLOCK - 1:BLOCK] for t in b]
        att = [jnp.zeros((BLOCK, BLOCK), BF16) for _ in hs]
        n_row, n_mm = len(_ROW_LEVELS), len(_MM_LEVELS)
        for lv in range(n_row + n_mm + 2):
            for u in range(HGRN_UNROLL):
                if lv < n_row:
                    el = jnp.exp2(boundary_exponent(b[u], _ROW_LEVELS[lv])).astype(BF16)
                    qt, kt = q[u] * el, k[u] * el
                elif lv < n_row + n_mm:
                    m = lv - n_row + 1
                    el = jnp.exp2(x[u][m * BLOCK:(m + 1) * BLOCK]).astype(BF16)
                    qt, kt = q[u] * el, k[u] * el
                elif lv == n_row + n_mm:
                    qt, kt = q[u] * jnp.exp2(lf[u]).astype(BF16), k[u]
                else:
                    qt, kt = q[u], k[u]
                a = lax.dot_general(qt, kt, _NT, preferred_element_type=F32)
                att[u] = att[u] + mask_ref[lv] * a.astype(BF16)
        for u, h in enumerate(hs):
            qb = q[u] * jnp.exp2(b[u]).astype(BF16)
            kd = k[u] * jnp.exp2(b_last[u] - b[u]).astype(BF16)
            vt = v_ref[h].astype(F32).T.astype(BF16)
            st = st_ref[h]
            lhs = jnp.concatenate([att[u], qb], axis=1)
            rhs = jnp.concatenate([vt, st.astype(BF16)], axis=1)
            o = lax.dot_general(lhs, rhs, _NT, preferred_element_type=F32)
            st_ref[h] = st * jnp.exp2(b_last[u]) + jnp.dot(vt, kd, preferred_element_type=F32)
            og = _rms_rows(o) * on_ref[...] * sg_ref[h].astype(F32)
            ob_ref[h] = og.astype(BF16)
        return carry

    lax.fori_loop(0, HEADS // HGRN_UNROLL, heads, 0)
    for h in range(HEADS):
        o_ref[:, h * HEAD_DIM:(h + 1) * HEAD_DIM] = ob_ref[h]


def _hgrn_rec(q, k, lf, v, sg, onorm):
    lp = q.shape[1]
    gmat, masks = _hgrn_consts()
    hm_spec = pl.BlockSpec((HEADS, BLOCK, HEAD_DIM), lambda c: (0, c, 0))
    return pl.pallas_call(
        _hgrn_rec_kernel,
        grid=(lp // BLOCK,),
        in_specs=[hm_spec] * 5 + [
            pl.BlockSpec(gmat.shape, lambda c: (0, 0)),
            pl.BlockSpec(masks.shape, lambda c: (0, 0, 0)),
            pl.BlockSpec((1, HEAD_DIM), lambda c: (0, 0)),
        ],
        out_specs=pl.BlockSpec((BLOCK, D_MODEL), lambda c: (c, 0)),
        out_shape=jax.ShapeDtypeStruct((lp, D_MODEL), BF16),
        scratch_shapes=[pltpu.VMEM((HEADS, HEAD_DIM, HEAD_DIM), F32),
                        pltpu.VMEM((HEADS, BLOCK, HEAD_DIM), BF16)],
        compiler_params=_params("arbitrary"),
        name="hgrn_rec",
    )(q, k, lf, v, sg, gmat, masks, onorm.reshape(1, HEAD_DIM))


def _fox_in_kernel(h_ref, g_ref, wq_ref, wk_ref, wv_ref, wg_ref, wf_ref, bf_ref, qn_ref, kn_ref,
                   q_ref, k_ref, v_ref, sg_ref, lf_ref, xn_ref):
    @pl.when(pl.program_id(1) == 0)
    def _():
        xn = (_rms_rows(h_ref[...]) * g_ref[...]).astype(BF16)
        xn_ref[...] = xn
        z = jnp.dot(xn, wf_ref[...], preferred_element_type=F32) + bf_ref[...]
        lf_ref[...] = jnp.minimum(z, 0.0) - jnp.log(1.0 + jnp.exp(-jnp.abs(z)))

    xn = xn_ref[...]
    qp = jnp.dot(xn, wq_ref[...], preferred_element_type=F32)
    kp = jnp.dot(xn, wk_ref[...], preferred_element_type=F32)
    vp = jnp.dot(xn, wv_ref[...], preferred_element_type=F32)
    gp = jnp.dot(xn, wg_ref[...], preferred_element_type=F32)
    sg = _sigmoid(gp)
    scale = HEAD_DIM ** -0.5 * LOG2E
    for hh in range(PROJ_TILE // HEAD_DIM):
        sl = slice(hh * HEAD_DIM, (hh + 1) * HEAD_DIM)
        q_ref[hh] = (_rms_rows(qp[:, sl]) * (qn_ref[...] * scale)).astype(BF16)
        k_ref[hh] = (_rms_rows(kp[:, sl]) * kn_ref[...]).astype(BF16)
        v_ref[hh] = vp[:, sl].astype(BF16)
        sg_ref[hh] = sg[:, sl].astype(BF16)


def _fox_in(hp, gain, w_in, j, b_f, qnorm, knorm):
    lp = hp.shape[0]
    nc = D_MODEL // PROJ_TILE
    hpt = PROJ_TILE // HEAD_DIM
    w, wf = _cast_proj_weights(w_in, j, 4 * D_MODEL)
    wspec = lambda s: pl.BlockSpec((D_MODEL, PROJ_TILE), lambda i, c, s=s: (0, s * nc + c))
    hm_spec = pl.BlockSpec((hpt, ROW_TILE, HEAD_DIM), lambda i, c: (c, i, 0))
    hm = jax.ShapeDtypeStruct((HEADS, lp, HEAD_DIM), BF16)
    vec = lambda n: pl.BlockSpec((1, n), lambda i, c: (0, 0))
    return pl.pallas_call(
        _fox_in_kernel,
        grid=(lp // ROW_TILE, nc),
        in_specs=[
            pl.BlockSpec((ROW_TILE, D_MODEL), lambda i, c: (i, 0)),
            vec(D_MODEL),
            wspec(0), wspec(1), wspec(2), wspec(3),
            pl.BlockSpec((D_MODEL, HEADS), lambda i, c: (0, 0)),
            vec(HEADS), vec(HEAD_DIM), vec(HEAD_DIM),
        ],
        out_specs=[hm_spec] * 4 + [pl.BlockSpec((ROW_TILE, HEADS), lambda i, c: (i, 0))],
        out_shape=[hm, hm, hm, hm, jax.ShapeDtypeStruct((lp, HEADS), F32)],
        scratch_shapes=[pltpu.VMEM((ROW_TILE, D_MODEL), BF16)],
        compiler_params=_params("parallel", "arbitrary"),
        name="fox_in",
    )(hp, gain.reshape(1, D_MODEL), w, w, w, w, wf, b_f.reshape(1, HEADS),
      qnorm.reshape(1, HEAD_DIM), knorm.reshape(1, HEAD_DIM))


def _fox_prep_consts():
    pq = np.zeros((HEADS, LANES, 4 * HEADS), np.float32)
    pk = np.zeros((HEADS, 4 * HEADS, LANES), np.float32)
    one, pad = 3 * HEADS, 3 * HEADS + 1
    for h in range(HEADS):
        for part in range(3):
            pq[h, part, part * HEADS + h] = 1.0
            pq[h, 3 + part, one] = 1.0
            pk[h, one, part] = 1.0
            pk[h, part * HEADS + h, 3 + part] = -1.0
        pk[h, pad, 3] = NEG_BIG
    return jnp.asarray(pq, BF16), jnp.asarray(pk, BF16)


def _fox_prep_kernel(lf_ref, tri_ref, pq_ref, pk_ref, q_ref, k_ref, v_ref,
                     qqt_ref, kk_ref, vt_ref, qn_ref, nmax_ref, carry_ref):
    i = pl.program_id(0)

    @pl.when(i == 0)
    def _():
        carry_ref[...] = jnp.zeros_like(carry_ref)
        nmax_ref[...] = jnp.zeros_like(nmax_ref)

    row16 = i * BLOCK + lax.broadcasted_iota(jnp.int32, (BLOCK, HEADS), 0)
    lf = jnp.where(row16 >= META_PAD, lf_ref[...], 0.0)
    tri = tri_ref[...]
    c = carry_ref[...]
    for part in _split3(lf):
        c = c + jnp.dot(tri, part, preferred_element_type=F32)
    carry_ref[...] = c[BLOCK - 1:BLOCK]
    col16 = lax.broadcasted_iota(jnp.int32, (BLOCK, HEADS), 1)
    extra = jnp.where(col16 == 0, 1.0, jnp.where((col16 == 1) & (row16 < META_PAD), 1.0, 0.0))
    src = jnp.concatenate([p.astype(F32) for p in _split3(c * LOG2E)] + [extra], axis=1)
    src_t = jnp.concatenate([src, jnp.zeros_like(src)], axis=1).T[:4 * HEADS].astype(BF16)
    src = src.astype(BF16)
    ones = jnp.ones((HEAD_DIM, LANES), BF16)
    for h in range(HEADS):
        aq_t = jnp.dot(pq_ref[h], src_t, preferred_element_type=F32)
        ak = jnp.dot(src, pk_ref[h], preferred_element_type=F32)
        qt = q_ref[h].astype(F32).T
        qqt_ref[h, :HEAD_DIM, :] = qt.astype(BF16)
        qqt_ref[h, HEAD_DIM:, :] = aq_t.astype(BF16)
        qn = jnp.sqrt(jnp.sum(qt * qt, axis=0, keepdims=True))
        qn_ref[h] = qn
        kf = k_ref[h].astype(F32)
        kn2 = jnp.dot((kf * kf).astype(BF16), ones, preferred_element_type=F32)
        kn = jnp.sqrt(jnp.max(kn2, axis=0, keepdims=True)) * NORM_SLACK
        nmax_ref[h:h + 1, :] = jnp.maximum(nmax_ref[h:h + 1, :], qn)
        nmax_ref[HEADS + h:HEADS + h + 1, :] = jnp.maximum(nmax_ref[HEADS + h:HEADS + h + 1, :], kn)
        kk_ref[h, :, :HEAD_DIM] = k_ref[h]
        kk_ref[h, :, HEAD_DIM:] = ak.astype(BF16)
        vt_ref[h, :HEAD_DIM, :] = v_ref[h].astype(F32).T.astype(BF16)
        vt_ref[h, HEAD_DIM:, :] = jnp.ones((V_ROWS - HEAD_DIM, BLOCK), BF16)


def _fox_prep(lf, q, k, v):
    lp = lf.shape[0]
    kdim = HEAD_DIM + LANES
    tri = jnp.asarray(np.tril(np.ones((BLOCK, BLOCK), np.float32)), BF16)
    pq, pk = _fox_prep_consts()
    hm_spec = pl.BlockSpec((HEADS, BLOCK, HEAD_DIM), lambda i: (0, i, 0))
    return pl.pallas_call(
        _fox_prep_kernel,
        grid=(lp // BLOCK,),
        in_specs=[pl.BlockSpec((BLOCK, HEADS), lambda i: (i, 0)),
                  pl.BlockSpec((BLOCK, BLOCK), lambda i: (0, 0)),
                  pl.BlockSpec(pq.shape, lambda i: (0, 0, 0)),
                  pl.BlockSpec(pk.shape, lambda i: (0, 0, 0)),
                  hm_spec, hm_spec, hm_spec],
        out_specs=[pl.BlockSpec((HEADS, kdim, BLOCK), lambda i: (0, 0, i)),
                   pl.BlockSpec((HEADS, BLOCK, kdim), lambda i: (0, i, 0)),
                   pl.BlockSpec((HEADS, V_ROWS, BLOCK), lambda i: (0, 0, i)),
                   pl.BlockSpec((HEADS, 1, BLOCK), lambda i: (0, 0, i)),
                   pl.BlockSpec((2 * HEADS, LANES), lambda i: (0, 0))],
        out_shape=[jax.ShapeDtypeStruct((HEADS, kdim, lp), BF16),
                   jax.ShapeDtypeStruct((HEADS, lp, kdim), BF16),
                   jax.ShapeDtypeStruct((HEADS, V_ROWS, lp), BF16),
                   jax.ShapeDtypeStruct((HEADS, 1, lp), F32),
                   jax.ShapeDtypeStruct((2 * HEADS, LANES), F32)],
        scratch_shapes=[pltpu.VMEM((1, HEADS), F32)],
        compiler_params=_params("arbitrary"),
        name="fox_prep",
    )(lf, tri, pq, pk, q, k, v)


def _fox_attn_kernel(safe_ref, kmax_ref, qqt_ref, qn_ref, kk_ref, vt_ref, sg_ref, o_ref,
                     m_ref, acc_ref, ring_ref):
    h = pl.program_id(0)
    i = pl.program_id(1)
    qs = i * ATT_Q
    n_groups = ATT_Q // ATT_GROUP
    n_steps = i * (ATT_Q // ATT_K)

    def scores(item):
        g, ks, nk, _ = item
        gs = slice(g * ATT_GROUP, (g + 1) * ATT_GROUP)
        return jnp.dot(kk_ref[pl.ds(ks, nk), :], qqt_ref[:, gs], preferred_element_type=F32)

    def step_items(j):
        return [(g, pl.multiple_of(j * ATT_K + sum(ATT_KSPLIT[:sub]), ATT_GROUP), nk, False)
                for sub, nk in enumerate(ATT_KSPLIT) for g in range(n_groups)]

    def sweep(fixed_shift):
        acc_ref[...] = jnp.zeros_like(acc_ref)
        if fixed_shift:
            m_ref[...] = qn_ref[...] * kmax_ref[h]
        else:
            m_ref[...] = jnp.full_like(m_ref, -jnp.inf)

        def update(item, s):
            g, ks, nk, masked = item
            gs = slice(g * ATT_GROUP, (g + 1) * ATT_GROUP)
            if masked:
                r = lax.broadcasted_iota(jnp.int32, s.shape, 0)
                c = lax.broadcasted_iota(jnp.int32, s.shape, 1) + g * ATT_GROUP
                s = jnp.where(r <= c, s, NEG_BIG)
            vt = vt_ref[:, pl.ds(ks, nk)]
            if fixed_shift:
                p = jnp.exp2(s - m_ref[:, gs]).astype(BF16)
                acc_ref[:, gs] += jnp.dot(vt, p, preferred_element_type=F32)
            else:
                m_prev = m_ref[:, gs]
                m_new = jnp.maximum(m_prev, jnp.max(s, axis=0, keepdims=True))
                alpha = jnp.exp2(m_prev - m_new)
                p = jnp.exp2((s - m_new).astype(BF16))
                m_ref[:, gs] = m_new
                acc_ref[:, gs] = alpha * acc_ref[:, gs] + jnp.dot(vt, p, preferred_element_type=F32)

        def run(items, first, ahead):
            pending = list(first) if first else [scores(item) for item in items[:ATT_AHEAD]]
            for n, item in enumerate(items):
                if n + ATT_AHEAD < len(items):
                    pending.append(scores(items[n + ATT_AHEAD]))
                else:
                    a = n + ATT_AHEAD - len(items)
                    ring_ref[a] = scores(ahead[a])
                update(item, pending.pop(0))

        run([(g, pl.multiple_of(qs, ATT_Q), (g + 1) * ATT_GROUP, True) for g in range(n_groups)],
            None, step_items(0)[:ATT_AHEAD])

        def body(j, carry):
            run(step_items(j), [ring_ref[a] for a in range(ATT_AHEAD)],
                step_items(jnp.minimum(j + 1, n_steps - 1))[:ATT_AHEAD])
            return carry

        lax.fori_loop(0, n_steps, body, 0)

    @pl.when(safe_ref[h] == 1)
    def _():
        sweep(True)

    @pl.when(safe_ref[h] != 1)
    def _():
        sweep(False)

    o = (acc_ref[:HEAD_DIM, :] / acc_ref[HEAD_DIM:HEAD_DIM + 1, :]).T * sg_ref[...].astype(F32)
    row = qs + lax.broadcasted_iota(jnp.int32, o.shape, 0)
    o_ref[...] = jnp.where(row >= META_PAD, o, 0.0).astype(BF16)


def _fox_attn(qqt, qn, nmax, kk, vt, sg):
    lp = kk.shape[1]
    kdim = kk.shape[2]
    qmax, kmax = jnp.max(nmax[:HEADS], axis=1), jnp.max(nmax[HEADS:], axis=1)
    safe = (2.0 * qmax * kmax <= SAFE_SHIFT_RANGE).astype(jnp.int32)
    grid_spec = pltpu.PrefetchScalarGridSpec(
        num_scalar_prefetch=2,
        grid=(HEADS, lp // ATT_Q),
        in_specs=[pl.BlockSpec((None, kdim, ATT_Q), lambda h, i, *_: (h, 0, i)),
                  pl.BlockSpec((None, 1, ATT_Q), lambda h, i, *_: (h, 0, i)),
                  pl.BlockSpec((None, lp, kdim), lambda h, i, *_: (h, 0, 0)),
                  pl.BlockSpec((None, V_ROWS, lp), lambda h, i, *_: (h, 0, 0)),
                  pl.BlockSpec((None, ATT_Q, HEAD_DIM), lambda h, i, *_: (h, i, 0))],
        out_specs=pl.BlockSpec((ATT_Q, HEAD_DIM), lambda h, i, *_: (i, h)),
        scratch_shapes=[pltpu.VMEM((1, ATT_Q), F32), pltpu.VMEM((V_ROWS, ATT_Q), F32),
                        pltpu.VMEM((ATT_AHEAD, ATT_KSPLIT[0], ATT_GROUP), F32)])
    return pl.pallas_call(
        _fox_attn_kernel,
        grid_spec=grid_spec,
        out_shape=jax.ShapeDtypeStruct((lp, D_MODEL), BF16),
        compiler_params=_params("parallel", "arbitrary"),
        name="fox_attn",
    )(safe, kmax, qqt, qn, kk, vt, sg)


def kernel(x, meta_tokens, norm_g, ffn_w_gu, ffn_w_down, lb_logits, hg_w_in, hg_w_out, hg_onorm,
           fox_w_in, fox_b_f, fox_w_out, fox_qnorm, fox_knorm):
    assert x.shape[0] == 1 and x.shape[2] == D_MODEL
    seq = x.shape[1]
    real = META_PAD + N_META + seq
    lp = -(-real // ATT_Q) * ATT_Q
    assert seq % BLOCK == 0
    head = jnp.concatenate([jnp.zeros((META_PAD, D_MODEL), F32), meta_tokens.astype(F32)], axis=0)
    wgu, wd = _cast_ffn_weights(ffn_w_gu.reshape(2 * DEPTH, D_MODEL, 2 * FFN_DIM),
                                ffn_w_down.reshape(2 * DEPTH, FFN_DIM, D_MODEL))
    hp = x[0]
    for layer in range(DEPTH):
        hp = _ffn(hp, head, norm_g[layer, 0], wgu, wd, 2 * layer, -1 if layer == 0 else 0, lp)
        j = layer // 2
        if layer % 2 == 0:
            q, k, lf, v, sg = _hgrn_in(hp, norm_g[layer, 1], hg_w_in, j, lb_logits, layer)
            og = _hgrn_rec(q, k, lf, v, sg, hg_onorm[j])
            hp = _out_proj(hp, og, hg_w_out[j])
        else:
            q, k, v, sg, lf = _fox_in(hp, norm_g[layer, 1], fox_w_in, j, fox_b_f[j],
                                      fox_qnorm[j], fox_knorm[j])
            qqt, kk, vt, qn, nmax = _fox_prep(lf, q, k, v)
            og = _fox_attn(qqt, qn, nmax, kk, vt, sg)
            hp = _out_proj(hp, og, fox_w_out[j])
        last = layer == DEPTH - 1
        hp = _ffn(hp, head, norm_g[layer, 2], wgu, wd, 2 * layer + 1,
                  1 if last else 0, seq if last else lp)
    return hp[None]
```

```python
import functools

import numpy as np
import jax
import jax.numpy as jnp
from jax import lax
from jax.experimental import pallas as pl
from jax.experimental.pallas import tpu as pltpu

F32 = jnp.float32
BF16 = jnp.bfloat16

D_MODEL = 2048
DEPTH = 2
N_META = 16
BLOCK = 128
META_PAD = (-N_META) % BLOCK
RMS_EPS = 1e-6
HEADS = 16
HEAD_DIM = 128
FFN_DIM = 5504
NEG_BIG = -1e30
LOG2E = 1.4426950408889634

LANES = 128
ROW_TILE = 640
FFN_TILE = 512
FFN_PAD = -(-FFN_DIM // FFN_TILE) * FFN_TILE
PROJ_TILE = 512
CAST_ROWS = 256
ATT_Q = 1280
ATT_K = 1280
ATT_KSPLIT = (512, 768)
ATT_GROUP = 256
ATT_AHEAD = 2
V_ROWS = HEAD_DIM + 16
NORM_SLACK = 1.01
SAFE_SHIFT_RANGE = 80.0
VMEM_LIMIT = 56 * 1024 * 1024

_NT = (((1,), (1,)), ((), ()))


def _params(*sem):
    return pltpu.CompilerParams(dimension_semantics=sem, vmem_limit_bytes=VMEM_LIMIT)


def _sigmoid(x):
    return 1.0 / (1.0 + jnp.exp(-x))


def _rms_rows(x):
    return x * lax.rsqrt(jnp.mean(x * x, axis=-1, keepdims=True) + RMS_EPS)


def _split3(x):
    hi = x.astype(BF16)
    r1 = x - hi.astype(F32)
    mid = r1.astype(BF16)
    lo = (r1 - mid.astype(F32)).astype(BF16)
    return hi, mid, lo


def _cast_wgu_kernel(w_ref, o_ref):
    zeros = jnp.zeros((CAST_ROWS, FFN_PAD - FFN_DIM), BF16)
    o_ref[:, :FFN_DIM] = w_ref[:, :FFN_DIM].astype(BF16)
    o_ref[:, FFN_DIM:FFN_PAD] = zeros
    o_ref[:, FFN_PAD:FFN_PAD + FFN_DIM] = w_ref[:, FFN_DIM:].astype(BF16)
    o_ref[:, FFN_PAD + FFN_DIM:] = zeros


def _cast_wd_kernel(w_ref, o_ref):
    row = pl.program_id(1) * FFN_TILE + lax.broadcasted_iota(jnp.int32, (FFN_TILE, D_MODEL), 0)
    o_ref[...] = jnp.where(row < FFN_DIM, w_ref[...], 0.0).astype(BF16)


def _cast_kernel(w_ref, o_ref):
    o_ref[...] = w_ref[...].astype(BF16)


def _cast_split_kernel(w_ref, o_ref, t_ref):
    n = o_ref.shape[-1]
    o_ref[...] = w_ref[:, :n].astype(BF16)
    t_ref[...] = w_ref[:, n:].astype(BF16)


def _cast_ffn_weights(w_gu, w_down):
    n = w_gu.shape[0]
    wgu = pl.pallas_call(
        _cast_wgu_kernel,
        grid=(n, D_MODEL // CAST_ROWS),
        in_specs=[pl.BlockSpec((None, CAST_ROWS, 2 * FFN_DIM), lambda a, r: (a, r, 0))],
        out_specs=pl.BlockSpec((None, CAST_ROWS, 2 * FFN_PAD), lambda a, r: (a, r, 0)),
        out_shape=jax.ShapeDtypeStruct((n, D_MODEL, 2 * FFN_PAD), BF16),
        compiler_params=_params("parallel", "parallel"),
        name="cast_wgu",
    )(w_gu)
    wd = pl.pallas_call(
        _cast_wd_kernel,
        grid=(n, FFN_PAD // FFN_TILE),
        in_specs=[pl.BlockSpec((None, FFN_TILE, D_MODEL), lambda a, r: (a, r, 0))],
        out_specs=pl.BlockSpec((None, FFN_TILE, D_MODEL), lambda a, r: (a, r, 0)),
        out_shape=jax.ShapeDtypeStruct((n, FFN_PAD, D_MODEL), BF16),
        compiler_params=_params("parallel", "parallel"),
        name="cast_wd",
    )(w_down)
    return wgu, wd


def _cast_proj_weights(w, j, n_main):
    width = w.shape[2]
    in_spec = pl.BlockSpec((None, CAST_ROWS, width), lambda r: (j, r, 0))
    main_spec = pl.BlockSpec((CAST_ROWS, n_main), lambda r: (r, 0))
    main = jax.ShapeDtypeStruct((D_MODEL, n_main), BF16)
    if width == n_main:
        return pl.pallas_call(
            _cast_kernel, grid=(D_MODEL // CAST_ROWS,), in_specs=[in_spec], out_specs=main_spec,
            out_shape=main, compiler_params=_params("parallel"), name="cast_proj")(w)
    return pl.pallas_call(
        _cast_split_kernel, grid=(D_MODEL // CAST_ROWS,), in_specs=[in_spec],
        out_specs=[main_spec, pl.BlockSpec((CAST_ROWS, width - n_main), lambda r: (r, 0))],
        out_shape=[main, jax.ShapeDtypeStruct((D_MODEL, width - n_main), BF16)],
        compiler_params=_params("parallel"), name="cast_proj_split")(w)


ROW_CHUNKS = ROW_TILE // BLOCK


def _ffn_kernel(*refs, shift, n_src):
    n_in = ROW_CHUNKS if shift else 1
    chunk_refs = refs[:n_in]
    head_ref, g_ref, wg_ref, wu_ref, wd_ref, o_ref, xn_ref = refs[n_in:]

    j = pl.program_id(1)

    def step(xn):
        a = jnp.dot(xn, wg_ref[...], preferred_element_type=F32)
        b = jnp.dot(xn, wu_ref[...], preferred_element_type=F32)
        act = (a * (0.5 * _sigmoid(a)) * b).astype(BF16)
        return jnp.dot(act, wd_ref[...], preferred_element_type=F32)

    @pl.when(j == 0)
    def _():
        parts = []
        for u, ref in enumerate(chunk_refs):
            idx = ROW_CHUNKS * pl.program_id(0) + u + shift
            blk = ref[...]
            if shift < 0:
                blk = jnp.where(idx < 0, head_ref[...], jnp.where(idx >= n_src, 0.0, blk))
            parts.append(blk)
        x = jnp.concatenate(parts, axis=0) if shift else parts[0]
        xn = (_rms_rows(x) * g_ref[...]).astype(BF16)
        xn_ref[...] = xn
        o_ref[...] = x + step(xn)

    @pl.when(j > 0)
    def _():
        o_ref[...] += step(xn_ref[...])


def _ffn(src, head, gain, wgu, wd, a, shift, out_rows):
    n_src = src.shape[0] // BLOCK
    nf = FFN_PAD // FFN_TILE
    chunk = lambda u: pl.BlockSpec(
        (BLOCK, D_MODEL), lambda i, j: (jnp.clip(ROW_CHUNKS * i + u + shift, 0, n_src - 1), 0))
    row_specs = ([chunk(u) for u in range(ROW_CHUNKS)] if shift else
                 [pl.BlockSpec((ROW_TILE, D_MODEL), lambda i, j: (i, 0))])
    return pl.pallas_call(
        functools.partial(_ffn_kernel, shift=shift, n_src=n_src),
        grid=(pl.cdiv(out_rows, ROW_TILE), nf),
        in_specs=row_specs + [
            pl.BlockSpec((BLOCK, D_MODEL), lambda i, j: (0, 0)),
            pl.BlockSpec((1, D_MODEL), lambda i, j: (0, 0)),
            pl.BlockSpec((None, D_MODEL, FFN_TILE), lambda i, j: (a, 0, j)),
            pl.BlockSpec((None, D_MODEL, FFN_TILE), lambda i, j: (a, 0, j + nf)),
            pl.BlockSpec((None, FFN_TILE, D_MODEL), lambda i, j: (a, j, 0)),
        ],
        out_specs=pl.BlockSpec((ROW_TILE, D_MODEL), lambda i, j: (i, 0)),
        out_shape=jax.ShapeDtypeStruct((out_rows, D_MODEL), F32),
        scratch_shapes=[pltpu.VMEM((ROW_TILE, D_MODEL), BF16)],
        compiler_params=_params("parallel", "arbitrary"),
        name="ffn",
    )(*([src] * len(row_specs)), head, gain.reshape(1, D_MODEL), wgu, wgu, wd)


def _out_proj_kernel(h_ref, o_ref, w_ref, out_ref):
    out_ref[...] = h_ref[...] + jnp.dot(o_ref[...], w_ref[...], preferred_element_type=F32)


def _out_proj(hp, og, w_out):
    lp = hp.shape[0]
    return pl.pallas_call(
        _out_proj_kernel,
        grid=(lp // ROW_TILE,),
        in_specs=[
            pl.BlockSpec((ROW_TILE, D_MODEL), lambda i: (i, 0)),
            pl.BlockSpec((ROW_TILE, D_MODEL), lambda i: (i, 0)),
            pl.BlockSpec((D_MODEL, D_MODEL), lambda i: (0, 0)),
        ],
        out_specs=pl.BlockSpec((ROW_TILE, D_MODEL), lambda i: (i, 0)),
        out_shape=jax.ShapeDtypeStruct((lp, D_MODEL), F32),
        compiler_params=_params("parallel"),
        name="out_proj",
    )(hp, og, w_out.astype(BF16))


def _hgrn_in_kernel(h_ref, g_ref, wq_ref, wf_ref, wi_ref, wg_ref, lbl_ref,
                    q_ref, k_ref, lf_ref, v_ref, sg_ref, xn_ref, *, layer):
    def project(xn):
        qp = jnp.dot(xn, wq_ref[...], preferred_element_type=F32)
        fp = jnp.dot(xn, wf_ref[...], preferred_element_type=F32)
        ip = jnp.dot(xn, wi_ref[...], preferred_element_type=F32)
        gp = jnp.dot(xn, wg_ref[...], preferred_element_type=F32)
        lg = lbl_ref[...]
        e = jnp.exp(lg - jnp.max(lg, axis=0, keepdims=True))
        lb = jnp.sum(e[:layer + 1], axis=0, keepdims=True) / jnp.sum(e, axis=0, keepdims=True)
        q = qp * _sigmoid(qp)
        fg = lb + (1.0 - lb) * _sigmoid(fp)
        k = 1.0 - fg
        lf = jnp.log(fg) * LOG2E
        sg = _sigmoid(gp)
        for hh in range(PROJ_TILE // HEAD_DIM):
            sl = slice(hh * HEAD_DIM, (hh + 1) * HEAD_DIM)
            q_ref[hh] = q[:, sl].astype(BF16)
            k_ref[hh] = k[:, sl].astype(BF16)
            lf_ref[hh] = lf[:, sl]
            v_ref[hh] = ip[:, sl].astype(BF16)
            sg_ref[hh] = sg[:, sl].astype(BF16)

    @pl.when(pl.program_id(1) == 0)
    def _():
        xn = (_rms_rows(h_ref[...]) * g_ref[...]).astype(BF16)
        xn_ref[...] = xn
        project(xn)

    @pl.when(pl.program_id(1) > 0)
    def _():
        project(xn_ref[...])


def _hgrn_in(hp, gain, w_in, j, lb_logits, layer):
    lp = hp.shape[0]
    nc = D_MODEL // PROJ_TILE
    hpt = PROJ_TILE // HEAD_DIM
    w = _cast_proj_weights(w_in, j, 4 * D_MODEL)
    wspec = lambda s: pl.BlockSpec((D_MODEL, PROJ_TILE), lambda i, c, s=s: (0, s * nc + c))
    hm_spec = pl.BlockSpec((hpt, ROW_TILE, HEAD_DIM), lambda i, c: (c, i, 0))
    hm = lambda dt: jax.ShapeDtypeStruct((HEADS, lp, HEAD_DIM), dt)
    return pl.pallas_call(
        functools.partial(_hgrn_in_kernel, layer=layer),
        grid=(lp // ROW_TILE, nc),
        in_specs=[
            pl.BlockSpec((ROW_TILE, D_MODEL), lambda i, c: (i, 0)),
            pl.BlockSpec((1, D_MODEL), lambda i, c: (0, 0)),
            wspec(0), wspec(1), wspec(2), wspec(3),
            pl.BlockSpec((DEPTH + 1, PROJ_TILE), lambda i, c: (0, c)),
        ],
        out_specs=[hm_spec] * 5,
        out_shape=[hm(BF16), hm(BF16), hm(F32), hm(BF16), hm(BF16)],
        scratch_shapes=[pltpu.VMEM((ROW_TILE, D_MODEL), BF16)],
        compiler_params=_params("parallel", "arbitrary"),
        name="hgrn_in",
    )(hp, gain.reshape(1, D_MODEL), w, w, w, w, lb_logits)


_ROW_LEVELS = (64, 32, 16, 8)
_MM_LEVELS = (4, 2)
HGRN_UNROLL = 4


def _hgrn_consts():
    t = np.arange(BLOCK)[:, None]
    j = np.arange(BLOCK)[None, :]
    blocks = [j <= t]
    for b in _MM_LEVELS:
        ref = 2 * b * (t // (2 * b)) + b - 1
        odd = (t // b) % 2 == 1
        blocks.append(np.where(odd, (j > ref) & (j <= t), (j > t) & (j <= ref)))
    g_all = np.concatenate(blocks, axis=0).astype(np.float32)
    masks = []
    for b in _ROW_LEVELS + _MM_LEVELS + (1,):
        masks.append((t // (2 * b) == j // (2 * b)) & ((t // b) % 2 == 1) & ((j // b) % 2 == 0))
    masks.append(t == j)
    return jnp.asarray(g_all, BF16), jnp.asarray(np.stack(masks).astype(np.float32))


def _hgrn_rec_kernel(q_ref, k_ref, lf_ref, v_ref, sg_ref, gmat_ref, mask_ref, on_ref,
                     o_ref, st_ref, ob_ref):
    @pl.when(pl.program_id(0) == 0)
    def _():
        st_ref[...] = jnp.zeros_like(st_ref)

    def boundary_exponent(b, size):
        parts = []
        for s in range(0, BLOCK, 2 * size):
            r = b[s + size - 1:s + size]
            parts += [r - b[s:s + size], b[s + size:s + 2 * size] - r]
        return jnp.concatenate(parts, axis=0)

    def heads(hg, carry):
        hs = [hg * HGRN_UNROLL + u for u in range(HGRN_UNROLL)]
        lf = [lf_ref[h] for h in hs]
        q = [q_ref[h] for h in hs]
        k = [k_ref[h] for h in hs]
        x3 = [jnp.dot(gmat_ref[...], jnp.concatenate(_split3(l), axis=1),
                      preferred_element_type=F32) for l in lf]
        x = [t[:, :HEAD_DIM] + t[:, HEAD_DIM:2 * HEAD_DIM] + t[:, 2 * HEAD_DIM:] for t in x3]
        b = [t[:BLOCK] for t in x]
        b_last = [t[BLOCK - 1:BLOCK] for t in b]
        att = [jnp.zeros((BLOCK, BLOCK), F32) for _ in hs]
        n_row, n_mm = len(_ROW_LEVELS), len(_MM_LEVELS)
        for lv in range(n_row + n_mm + 2):
            for u in range(HGRN_UNROLL):
                if lv < n_row:
                    el = jnp.exp2(boundary_exponent(b[u], _ROW_LEVELS[lv])).astype(BF16)
                    qt, kt = q[u] * el, k[u] * el
                elif lv < n_row + n_mm:
                    m = lv - n_row + 1
                    el = jnp.exp2(x[u][m * BLOCK:(m + 1) * BLOCK]).astype(BF16)
                    qt, kt = q[u] * el, k[u] * el
                elif lv == n_row + n_mm:
                    qt, kt = q[u] * jnp.exp2(lf[u]).astype(BF16), k[u]
                else:
                    qt, kt = q[u], k[u]
                a = lax.dot_general(qt, kt, _NT, preferred_element_type=F32)
                att[u] = att[u] + mask_ref[lv] * a
        for u, h in enumerate(hs):
            qb = q[u] * jnp.exp2(b[u]).astype(BF16)
            kd = k[u] * jnp.exp2(b_last[u] - b[u]).astype(BF16)
            vt = v_ref[h].astype(F32).T.astype(BF16)
            st = st_ref[h]
            lhs = jnp.concatenate([att[u].astype(BF16), qb], axis=1)
            rhs = jnp.concatenate([vt, st.astype(BF16)], axis=1)
            o = lax.dot_general(lhs, rhs, _NT, preferred_element_type=F32)
            st_ref[h] = st * jnp.exp2(b_last[u]) + jnp.dot(vt, kd, preferred_element_type=F32)
            og = _rms_rows(o) * on_ref[...] * sg_ref[h].astype(F32)
            ob_ref[h] = og.astype(BF16)
        return carry

    lax.fori_loop(0, HEADS // HGRN_UNROLL, heads, 0)
    for h in range(HEADS):
        o_ref[:, h * HEAD_DIM:(h + 1) * HEAD_DIM] = ob_ref[h]


def _hgrn_rec(q, k, lf, v, sg, onorm):
    lp = q.shape[1]
    gmat, masks = _hgrn_consts()
    hm_spec = pl.BlockSpec((HEADS, BLOCK, HEAD_DIM), lambda c: (0, c, 0))
    return pl.pallas_call(
        _hgrn_rec_kernel,
        grid=(lp // BLOCK,),
        in_specs=[hm_spec] * 5 + [
            pl.BlockSpec(gmat.shape, lambda c: (0, 0)),
            pl.BlockSpec(masks.shape, lambda c: (0, 0, 0)),
            pl.BlockSpec((1, HEAD_DIM), lambda c: (0, 0)),
        ],
        out_specs=pl.BlockSpec((BLOCK, D_MODEL), lambda c: (c, 0)),
        out_shape=jax.ShapeDtypeStruct((lp, D_MODEL), BF16),
        scratch_shapes=[pltpu.VMEM((HEADS, HEAD_DIM, HEAD_DIM), F32),
                        pltpu.VMEM((HEADS, BLOCK, HEAD_DIM), BF16)],
        compiler_params=_params("arbitrary"),
        name="hgrn_rec",
    )(q, k, lf, v, sg, gmat, masks, onorm.reshape(1, HEAD_DIM))


def _fox_in_kernel(h_ref, g_ref, wq_ref, wk_ref, wv_ref, wg_ref, wf_ref, bf_ref, qn_ref, kn_ref,
                   q_ref, k_ref, v_ref, sg_ref, lf_ref, xn_ref):
    def project(xn):
        qp = jnp.dot(xn, wq_ref[...], preferred_element_type=F32)
        kp = jnp.dot(xn, wk_ref[...], preferred_element_type=F32)
        vp = jnp.dot(xn, wv_ref[...], preferred_element_type=F32)
        gp = jnp.dot(xn, wg_ref[...], preferred_element_type=F32)
        sg = _sigmoid(gp)
        scale = HEAD_DIM ** -0.5 * LOG2E
        for hh in range(PROJ_TILE // HEAD_DIM):
            sl = slice(hh * HEAD_DIM, (hh + 1) * HEAD_DIM)
            q_ref[hh] = (_rms_rows(qp[:, sl]) * (qn_ref[...] * scale)).astype(BF16)
            k_ref[hh] = (_rms_rows(kp[:, sl]) * kn_ref[...]).astype(BF16)
            v_ref[hh] = vp[:, sl].astype(BF16)
            sg_ref[hh] = sg[:, sl].astype(BF16)

    @pl.when(pl.program_id(1) == 0)
    def _():
        xn = (_rms_rows(h_ref[...]) * g_ref[...]).astype(BF16)
        xn_ref[...] = xn
        z = jnp.dot(xn, wf_ref[...], preferred_element_type=F32) + bf_ref[...]
        lf_ref[...] = jnp.minimum(z, 0.0) - jnp.log(1.0 + jnp.exp(-jnp.abs(z)))
        project(xn)

    @pl.when(pl.program_id(1) > 0)
    def _():
        project(xn_ref[...])


def _fox_in(hp, gain, w_in, j, b_f, qnorm, knorm):
    lp = hp.shape[0]
    nc = D_MODEL // PROJ_TILE
    hpt = PROJ_TILE // HEAD_DIM
    w, wf = _cast_proj_weights(w_in, j, 4 * D_MODEL)
    wspec = lambda s: pl.BlockSpec((D_MODEL, PROJ_TILE), lambda i, c, s=s: (0, s * nc + c))
    hm_spec = pl.BlockSpec((hpt, ROW_TILE, HEAD_DIM), lambda i, c: (c, i, 0))
    hm = jax.ShapeDtypeStruct((HEADS, lp, HEAD_DIM), BF16)
    vec = lambda n: pl.BlockSpec((1, n), lambda i, c: (0, 0))
    return pl.pallas_call(
        _fox_in_kernel,
        grid=(lp // ROW_TILE, nc),
        in_specs=[
            pl.BlockSpec((ROW_TILE, D_MODEL), lambda i, c: (i, 0)),
            vec(D_MODEL),
            wspec(0), wspec(1), wspec(2), wspec(3),
            pl.BlockSpec((D_MODEL, HEADS), lambda i, c: (0, 0)),
            vec(HEADS), vec(HEAD_DIM), vec(HEAD_DIM),
        ],
        out_specs=[hm_spec] * 4 + [pl.BlockSpec((ROW_TILE, HEADS), lambda i, c: (i, 0))],
        out_shape=[hm, hm, hm, hm, jax.ShapeDtypeStruct((lp, HEADS), F32)],
        scratch_shapes=[pltpu.VMEM((ROW_TILE, D_MODEL), BF16)],
        compiler_params=_params("parallel", "arbitrary"),
        name="fox_in",
    )(hp, gain.reshape(1, D_MODEL), w, w, w, w, wf, b_f.reshape(1, HEADS),
      qnorm.reshape(1, HEAD_DIM), knorm.reshape(1, HEAD_DIM))


def _fox_prep_consts():
    pq = np.zeros((HEADS, LANES, 4 * HEADS), np.float32)
    pk = np.zeros((HEADS, 4 * HEADS, LANES), np.float32)
    one, pad = 3 * HEADS, 3 * HEADS + 1
    for h in range(HEADS):
        for part in range(3):
            pq[h, part, part * HEADS + h] = 1.0
            pq[h, 3 + part, one] = 1.0
            pk[h, one, part] = 1.0
            pk[h, part * HEADS + h, 3 + part] = -1.0
        pk[h, pad, 3] = NEG_BIG
    return jnp.asarray(pq, BF16), jnp.asarray(pk, BF16)


def _fox_prep_kernel(lf_ref, tri_ref, pq_ref, pk_ref, q_ref, k_ref, v_ref,
                     qqt_ref, kk_ref, vt_ref, qn_ref, nmax_ref, carry_ref):
    i = pl.program_id(0)

    @pl.when(i == 0)
    def _():
        carry_ref[...] = jnp.zeros_like(carry_ref)
        nmax_ref[...] = jnp.zeros_like(nmax_ref)

    row16 = i * BLOCK + lax.broadcasted_iota(jnp.int32, (BLOCK, HEADS), 0)
    lf = jnp.where(row16 >= META_PAD, lf_ref[...], 0.0)
    tri = tri_ref[...]
    c = carry_ref[...]
    for part in _split3(lf):
        c = c + jnp.dot(tri, part, preferred_element_type=F32)
    carry_ref[...] = c[BLOCK - 1:BLOCK]
    col16 = lax.broadcasted_iota(jnp.int32, (BLOCK, HEADS), 1)
    extra = jnp.where(col16 == 0, 1.0, jnp.where((col16 == 1) & (row16 < META_PAD), 1.0, 0.0))
    src = jnp.concatenate([p.astype(F32) for p in _split3(c * LOG2E)] + [extra], axis=1)
    src_t = jnp.concatenate([src, jnp.zeros_like(src)], axis=1).T[:4 * HEADS].astype(BF16)
    src = src.astype(BF16)
    ones = jnp.ones((HEAD_DIM, LANES), BF16)
    for h in range(HEADS):
        aq_t = jnp.dot(pq_ref[h], src_t, preferred_element_type=F32)
        ak = jnp.dot(src, pk_ref[h], preferred_element_type=F32)
        qt = q_ref[h].astype(F32).T
        qqt_ref[h, :HEAD_DIM, :] = qt.astype(BF16)
        qqt_ref[h, HEAD_DIM:, :] = aq_t.astype(BF16)
        qn = jnp.sqrt(jnp.sum(qt * qt, axis=0, keepdims=True))
        qn_ref[h] = qn
        kf = k_ref[h].astype(F32)
        kn2 = jnp.dot((kf * kf).astype(BF16), ones, preferred_element_type=F32)
        kn = jnp.sqrt(jnp.max(kn2, axis=0, keepdims=True)) * NORM_SLACK
        nmax_ref[h:h + 1, :] = jnp.maximum(nmax_ref[h:h + 1, :], qn)
        nmax_ref[HEADS + h:HEADS + h + 1, :] = jnp.maximum(nmax_ref[HEADS + h:HEADS + h + 1, :], kn)
        kk_ref[h, :, :HEAD_DIM] = k_ref[h]
        kk_ref[h, :, HEAD_DIM:] = ak.astype(BF16)
        vt_ref[h, :HEAD_DIM, :] = v_ref[h].astype(F32).T.astype(BF16)
        vt_ref[h, HEAD_DIM:, :] = jnp.ones((V_ROWS - HEAD_DIM, BLOCK), BF16)


def _fox_prep(lf, q, k, v):
    lp = lf.shape[0]
    kdim = HEAD_DIM + LANES
    tri = jnp.asarray(np.tril(np.ones((BLOCK, BLOCK), np.float32)), BF16)
    pq, pk = _fox_prep_consts()
    hm_spec = pl.BlockSpec((HEADS, BLOCK, HEAD_DIM), lambda i: (0, i, 0))
    return pl.pallas_call(
        _fox_prep_kernel,
        grid=(lp // BLOCK,),
        in_specs=[pl.BlockSpec((BLOCK, HEADS), lambda i: (i, 0)),
                  pl.BlockSpec((BLOCK, BLOCK), lambda i: (0, 0)),
                  pl.BlockSpec(pq.shape, lambda i: (0, 0, 0)),
                  pl.BlockSpec(pk.shape, lambda i: (0, 0, 0)),
                  hm_spec, hm_spec, hm_spec],
        out_specs=[pl.BlockSpec((HEADS, kdim, BLOCK), lambda i: (0, 0, i)),
                   pl.BlockSpec((HEADS, BLOCK, kdim), lambda i: (0, i, 0)),
                   pl.BlockSpec((HEADS, V_ROWS, BLOCK), lambda i: (0, 0, i)),
                   pl.BlockSpec((HEADS, 1, BLOCK), lambda i: (0, 0, i)),
                   pl.BlockSpec((2 * HEADS, LANES), lambda i: (0, 0))],
        out_shape=[jax.ShapeDtypeStruct((HEADS, kdim, lp), BF16),
                   jax.ShapeDtypeStruct((HEADS, lp, kdim), BF16),
                   jax.ShapeDtypeStruct((HEADS, V_ROWS, lp), BF16),
                   jax.ShapeDtypeStruct((HEADS, 1, lp), F32),
                   jax.ShapeDtypeStruct((2 * HEADS, LANES), F32)],
        scratch_shapes=[pltpu.VMEM((1, HEADS), F32)],
        compiler_params=_params("arbitrary"),
        name="fox_prep",
    )(lf, tri, pq, pk, q, k, v)


def _fox_attn_kernel(safe_ref, kmax_ref, qqt_ref, qn_ref, kk_ref, vt_ref, sg_ref, o_ref,
                     m_ref, acc_ref, ring_ref):
    h = pl.program_id(0)
    i = pl.program_id(1)
    qs = i * ATT_Q
    n_groups = ATT_Q // ATT_GROUP
    n_steps = i * (ATT_Q // ATT_K)

    def scores(item):
        g, ks, nk, _ = item
        gs = slice(g * ATT_GROUP, (g + 1) * ATT_GROUP)
        return jnp.dot(kk_ref[pl.ds(ks, nk), :], qqt_ref[:, gs], preferred_element_type=F32)

    def step_items(j):
        return [(g, pl.multiple_of(j * ATT_K + sum(ATT_KSPLIT[:sub]), ATT_GROUP), nk, False)
                for sub, nk in enumerate(ATT_KSPLIT) for g in range(n_groups)]

    def sweep(fixed_shift):
        acc_ref[...] = jnp.zeros_like(acc_ref)
        if fixed_shift:
            m_ref[...] = qn_ref[...] * kmax_ref[h]
        else:
            m_ref[...] = jnp.full_like(m_ref, -jnp.inf)

        def update(item, s):
            g, ks, nk, masked = item
            gs = slice(g * ATT_GROUP, (g + 1) * ATT_GROUP)
            if masked:
                r = lax.broadcasted_iota(jnp.int32, s.shape, 0)
                c = lax.broadcasted_iota(jnp.int32, s.shape, 1) + g * ATT_GROUP
                s = jnp.where(r <= c, s, NEG_BIG)
            vt = vt_ref[:, pl.ds(ks, nk)]
            if fixed_shift:
                p = jnp.exp2(s - m_ref[:, gs]).astype(BF16)
                acc_ref[:, gs] += jnp.dot(vt, p, preferred_element_type=F32)
            else:
                m_prev = m_ref[:, gs]
                m_new = jnp.maximum(m_prev, jnp.max(s, axis=0, keepdims=True))
                alpha = jnp.exp2(m_prev - m_new)
                p = jnp.exp2((s - m_new).astype(BF16))
                m_ref[:, gs] = m_new
                acc_ref[:, gs] = alpha * acc_ref[:, gs] + jnp.dot(vt, p, preferred_element_type=F32)

        def run(items, first, ahead):
            pending = list(first) if first else [scores(item) for item in items[:ATT_AHEAD]]
            for n, item in enumerate(items):
                if n + ATT_AHEAD < len(items):
                    pending.append(scores(items[n + ATT_AHEAD]))
                else:
                    a = n + ATT_AHEAD - len(items)
                    ring_ref[a] = scores(ahead[a])
                update(item, pending.pop(0))

        run([(g, pl.multiple_of(qs, ATT_Q), (g + 1) * ATT_GROUP, True) for g in range(n_groups)],
            None, step_items(0)[:ATT_AHEAD])

        def body(j, carry):
            run(step_items(j), [ring_ref[a] for a in range(ATT_AHEAD)],
                step_items(jnp.minimum(j + 1, n_steps - 1))[:ATT_AHEAD])
            return carry

        lax.fori_loop(0, n_steps, body, 0)

    @pl.when(safe_ref[h] == 1)
    def _():
        sweep(True)

    @pl.when(safe_ref[h] != 1)
    def _():
        sweep(False)

    o = (acc_ref[:HEAD_DIM, :] / acc_ref[HEAD_DIM:HEAD_DIM + 1, :]).T * sg_ref[...].astype(F32)
    row = qs + lax.broadcasted_iota(jnp.int32, o.shape, 0)
    o_ref[...] = jnp.where(row >= META_PAD, o, 0.0).astype(BF16)


def _fox_attn(qqt, qn, nmax, kk, vt, sg):
    lp = kk.shape[1]
    kdim = kk.shape[2]
    qmax, kmax = jnp.max(nmax[:HEADS], axis=1), jnp.max(nmax[HEADS:], axis=1)
    safe = (2.0 * qmax * kmax <= SAFE_SHIFT_RANGE).astype(jnp.int32)
    grid_spec = pltpu.PrefetchScalarGridSpec(
        num_scalar_prefetch=2,
        grid=(HEADS, lp // ATT_Q),
        in_specs=[pl.BlockSpec((None, kdim, ATT_Q), lambda h, i, *_: (h, 0, i)),
                  pl.BlockSpec((None, 1, ATT_Q), lambda h, i, *_: (h, 0, i)),
                  pl.BlockSpec((None, lp, kdim), lambda h, i, *_: (h, 0, 0)),
                  pl.BlockSpec((None, V_ROWS, lp), lambda h, i, *_: (h, 0, 0)),
                  pl.BlockSpec((None, ATT_Q, HEAD_DIM), lambda h, i, *_: (h, i, 0))],
        out_specs=pl.BlockSpec((ATT_Q, HEAD_DIM), lambda h, i, *_: (i, h)),
        scratch_shapes=[pltpu.VMEM((1, ATT_Q), F32), pltpu.VMEM((V_ROWS, ATT_Q), F32),
                        pltpu.VMEM((ATT_AHEAD, ATT_KSPLIT[0], ATT_GROUP), F32)])
    return pl.pallas_call(
        _fox_attn_kernel,
        grid_spec=grid_spec,
        out_shape=jax.ShapeDtypeStruct((lp, D_MODEL), BF16),
        compiler_params=_params("parallel", "arbitrary"),
        name="fox_attn",
    )(safe, kmax, qqt, qn, kk, vt, sg)


def kernel(x, meta_tokens, norm_g, ffn_w_gu, ffn_w_down, lb_logits, hg_w_in, hg_w_out, hg_onorm,
           fox_w_in, fox_b_f, fox_w_out, fox_qnorm, fox_knorm):
    assert x.shape[0] == 1 and x.shape[2] == D_MODEL
    seq = x.shape[1]
    real = META_PAD + N_META + seq
    lp = -(-real // ATT_Q) * ATT_Q
    assert seq % BLOCK == 0
    head = jnp.concatenate([jnp.zeros((META_PAD, D_MODEL), F32), meta_tokens.astype(F32)], axis=0)
    wgu, wd = _cast_ffn_weights(ffn_w_gu.reshape(2 * DEPTH, D_MODEL, 2 * FFN_DIM),
                                ffn_w_down.reshape(2 * DEPTH, FFN_DIM, D_MODEL))
    hp = x[0]
    for layer in range(DEPTH):
        hp = _ffn(hp, head, norm_g[layer, 0], wgu, wd, 2 * layer, -1 if layer == 0 else 0, lp)
        j = layer // 2
        if layer % 2 == 0:
            q, k, lf, v, sg = _hgrn_in(hp, norm_g[layer, 1], hg_w_in, j, lb_logits, layer)
            og = _hgrn_rec(q, k, lf, v, sg, hg_onorm[j])
            hp = _out_proj(hp, og, hg_w_out[j])
        else:
            q, k, v, sg, lf = _fox_in(hp, norm_g[layer, 1], fox_w_in, j, fox_b_f[j],
                                      fox_qnorm[j], fox_knorm[j])
            qqt, kk, vt, qn, nmax = _fox_prep(lf, q, k, v)
            og = _fox_attn(qqt, qn, nmax, kk, vt, sg)
            hp = _out_proj(hp, og, fox_w_out[j])
        last = layer == DEPTH - 1
        hp = _ffn(hp, head, norm_g[layer, 2], wgu, wd, 2 * layer + 1,
                  1 if last else 0, seq if last else lp)
    return hp[None]
```

```python
import functools

import numpy as np
import jax
import jax.numpy as jnp
from jax import lax
from jax.experimental import pallas as pl
from jax.experimental.pallas import tpu as pltpu

F32 = jnp.float32
BF16 = jnp.bfloat16

D_MODEL = 2048
DEPTH = 2
N_META = 16
BLOCK = 128
META_PAD = (-N_META) % BLOCK
RMS_EPS = 1e-6
HEADS = 16
HEAD_DIM = 128
FFN_DIM = 5504
NEG_BIG = -1e30
LOG2E = 1.4426950408889634

LANES = 128
ROW_TILE = 640
FFN_TILE = 512
FFN_PAD = -(-FFN_DIM // FFN_TILE) * FFN_TILE
PROJ_TILE = 512
CAST_ROWS = 256
ATT_Q = 1280
ATT_K = 1280
ATT_KSPLIT = (512, 768)
ATT_GROUP = 256
ATT_AHEAD = 2
V_ROWS = HEAD_DIM + 16
NORM_SLACK = 1.01
SAFE_SHIFT_RANGE = 80.0
VMEM_LIMIT = 56 * 1024 * 1024

_NT = (((1,), (1,)), ((), ()))


def _params(*sem):
    return pltpu.CompilerParams(dimension_semantics=sem, vmem_limit_bytes=VMEM_LIMIT)


def _sigmoid(x):
    return 1.0 / (1.0 + jnp.exp(-x))


def _rms_rows(x):
    return x * lax.rsqrt(jnp.mean(x * x, axis=-1, keepdims=True) + RMS_EPS)


def _split3(x):
    hi = x.astype(BF16)
    r1 = x - hi.astype(F32)
    mid = r1.astype(BF16)
    lo = (r1 - mid.astype(F32)).astype(BF16)
    return hi, mid, lo


def _cast_wgu_kernel(w_ref, o_ref):
    zeros = jnp.zeros((CAST_ROWS, FFN_PAD - FFN_DIM), BF16)
    o_ref[:, :FFN_DIM] = w_ref[:, :FFN_DIM].astype(BF16)
    o_ref[:, FFN_DIM:FFN_PAD] = zeros
    o_ref[:, FFN_PAD:FFN_PAD + FFN_DIM] = w_ref[:, FFN_DIM:].astype(BF16)
    o_ref[:, FFN_PAD + FFN_DIM:] = zeros


def _cast_wd_kernel(w_ref, o_ref):
    row = pl.program_id(1) * FFN_TILE + lax.broadcasted_iota(jnp.int32, (FFN_TILE, D_MODEL), 0)
    o_ref[...] = jnp.where(row < FFN_DIM, w_ref[...], 0.0).astype(BF16)


def _cast_kernel(w_ref, o_ref):
    o_ref[...] = w_ref[...].astype(BF16)


def _cast_transposed_kernel(wt_ref, tail_ref, o_ref, t_ref):
    o_ref[...] = wt_ref[...].T.astype(BF16)
    t_ref[...] = tail_ref[...].astype(BF16)


def _cast_ffn_weights(w_gu, w_down):
    n = w_gu.shape[0]
    wgu = pl.pallas_call(
        _cast_wgu_kernel,
        grid=(n, D_MODEL // CAST_ROWS),
        in_specs=[pl.BlockSpec((None, CAST_ROWS, 2 * FFN_DIM), lambda a, r: (a, r, 0))],
        out_specs=pl.BlockSpec((None, CAST_ROWS, 2 * FFN_PAD), lambda a, r: (a, r, 0)),
        out_shape=jax.ShapeDtypeStruct((n, D_MODEL, 2 * FFN_PAD), BF16),
        compiler_params=_params("parallel", "parallel"),
        name="cast_wgu",
    )(w_gu)
    wd = pl.pallas_call(
        _cast_wd_kernel,
        grid=(n, FFN_PAD // FFN_TILE),
        in_specs=[pl.BlockSpec((None, FFN_TILE, D_MODEL), lambda a, r: (a, r, 0))],
        out_specs=pl.BlockSpec((None, FFN_TILE, D_MODEL), lambda a, r: (a, r, 0)),
        out_shape=jax.ShapeDtypeStruct((n, FFN_PAD, D_MODEL), BF16),
        compiler_params=_params("parallel", "parallel"),
        name="cast_wd",
    )(w_down)
    return wgu, wd


def _cast_proj_weights(w, j, n_main):
    width = w.shape[2]
    main = jax.ShapeDtypeStruct((D_MODEL, n_main), BF16)
    if width == n_main:
        return pl.pallas_call(
            _cast_kernel, grid=(D_MODEL // CAST_ROWS,),
            in_specs=[pl.BlockSpec((None, CAST_ROWS, width), lambda r: (j, r, 0))],
            out_specs=pl.BlockSpec((CAST_ROWS, n_main), lambda r: (r, 0)),
            out_shape=main, compiler_params=_params("parallel"), name="cast_proj")(w)
    tail = width - n_main
    assert n_main % tail == 0
    wt = jnp.swapaxes(w, 1, 2)
    return pl.pallas_call(
        _cast_transposed_kernel, grid=(n_main // CAST_ROWS,),
        in_specs=[pl.BlockSpec((None, CAST_ROWS, D_MODEL), lambda r: (j, r, 0)),
                  pl.BlockSpec((None, tail, D_MODEL), lambda r: (j, n_main // tail, 0))],
        out_specs=[pl.BlockSpec((D_MODEL, CAST_ROWS), lambda r: (0, r)),
                   pl.BlockSpec((tail, D_MODEL), lambda r: (0, 0))],
        out_shape=[main, jax.ShapeDtypeStruct((tail, D_MODEL), BF16)],
        compiler_params=_params("arbitrary"), name="cast_proj_t")(wt, wt)


ROW_CHUNKS = ROW_TILE // BLOCK


def _ffn_kernel(*refs, shift, n_src):
    n_in = ROW_CHUNKS if shift else 1
    chunk_refs = refs[:n_in]
    head_ref, g_ref, wg_ref, wu_ref, wd_ref, o_ref, xn_ref = refs[n_in:]

    j = pl.program_id(1)

    def step(xn):
        a = jnp.dot(xn, wg_ref[...], preferred_element_type=F32)
        b = jnp.dot(xn, wu_ref[...], preferred_element_type=F32)
        act = (a * (0.5 * _sigmoid(a)) * b).astype(BF16)
        return jnp.dot(act, wd_ref[...], preferred_element_type=F32)

    @pl.when(j == 0)
    def _():
        parts = []
        for u, ref in enumerate(chunk_refs):
            idx = ROW_CHUNKS * pl.program_id(0) + u + shift
            blk = ref[...]
            if shift < 0:
                blk = jnp.where(idx < 0, head_ref[...], jnp.where(idx >= n_src, 0.0, blk))
            parts.append(blk)
        x = jnp.concatenate(parts, axis=0) if shift else parts[0]
        xn = (_rms_rows(x) * g_ref[...]).astype(BF16)
        xn_ref[...] = xn
        o_ref[...] = x + step(xn)

    @pl.when(j > 0)
    def _():
        o_ref[...] += step(xn_ref[...])


def _ffn(src, head, gain, wgu, wd, a, shift, out_rows):
    n_src = src.shape[0] // BLOCK
    nf = FFN_PAD // FFN_TILE
    chunk = lambda u: pl.BlockSpec(
        (BLOCK, D_MODEL), lambda i, j: (jnp.clip(ROW_CHUNKS * i + u + shift, 0, n_src - 1), 0))
    row_specs = ([chunk(u) for u in range(ROW_CHUNKS)] if shift else
                 [pl.BlockSpec((ROW_TILE, D_MODEL), lambda i, j: (i, 0))])
    return pl.pallas_call(
        functools.partial(_ffn_kernel, shift=shift, n_src=n_src),
        grid=(pl.cdiv(out_rows, ROW_TILE), nf),
        in_specs=row_specs + [
            pl.BlockSpec((BLOCK, D_MODEL), lambda i, j: (0, 0)),
            pl.BlockSpec((1, D_MODEL), lambda i, j: (0, 0)),
            pl.BlockSpec((None, D_MODEL, FFN_TILE), lambda i, j: (a, 0, j)),
            pl.BlockSpec((None, D_MODEL, FFN_TILE), lambda i, j: (a, 0, j + nf)),
            pl.BlockSpec((None, FFN_TILE, D_MODEL), lambda i, j: (a, j, 0)),
        ],
        out_specs=pl.BlockSpec((ROW_TILE, D_MODEL), lambda i, j: (i, 0)),
        out_shape=jax.ShapeDtypeStruct((out_rows, D_MODEL), F32),
        scratch_shapes=[pltpu.VMEM((ROW_TILE, D_MODEL), BF16)],
        compiler_params=_params("parallel", "arbitrary"),
        name="ffn",
    )(*([src] * len(row_specs)), head, gain.reshape(1, D_MODEL), wgu, wgu, wd)


def _out_proj_kernel(h_ref, o_ref, w_ref, out_ref):
    out_ref[...] = h_ref[...] + jnp.dot(o_ref[...], w_ref[...], preferred_element_type=F32)


def _out_proj(hp, og, w_out):
    lp = hp.shape[0]
    return pl.pallas_call(
        _out_proj_kernel,
        grid=(lp // ROW_TILE,),
        in_specs=[
            pl.BlockSpec((ROW_TILE, D_MODEL), lambda i: (i, 0)),
            pl.BlockSpec((ROW_TILE, D_MODEL), lambda i: (i, 0)),
            pl.BlockSpec((D_MODEL, D_MODEL), lambda i: (0, 0)),
        ],
        out_specs=pl.BlockSpec((ROW_TILE, D_MODEL), lambda i: (i, 0)),
        out_shape=jax.ShapeDtypeStruct((lp, D_MODEL), F32),
        compiler_params=_params("parallel"),
        name="out_proj",
    )(hp, og, w_out.astype(BF16))


def _hgrn_in_kernel(h_ref, g_ref, wq_ref, wf_ref, wi_ref, wg_ref, lbl_ref,
                    q_ref, k_ref, lf_ref, v_ref, sg_ref, xn_ref, *, layer):
    def project(xn):
        qp = jnp.dot(xn, wq_ref[...], preferred_element_type=F32)
        fp = jnp.dot(xn, wf_ref[...], preferred_element_type=F32)
        ip = jnp.dot(xn, wi_ref[...], preferred_element_type=F32)
        gp = jnp.dot(xn, wg_ref[...], preferred_element_type=F32)
        lg = lbl_ref[...]
        e = jnp.exp(lg - jnp.max(lg, axis=0, keepdims=True))
        lb = jnp.sum(e[:layer + 1], axis=0, keepdims=True) / jnp.sum(e, axis=0, keepdims=True)
        q = qp * _sigmoid(qp)
        fg = lb + (1.0 - lb) * _sigmoid(fp)
        k = 1.0 - fg
        lf = jnp.log(fg) * LOG2E
        sg = _sigmoid(gp)
        for hh in range(PROJ_TILE // HEAD_DIM):
            sl = slice(hh * HEAD_DIM, (hh + 1) * HEAD_DIM)
            q_ref[hh] = q[:, sl].astype(BF16)
            k_ref[hh] = k[:, sl].astype(BF16)
            lf_ref[hh] = lf[:, sl]
            v_ref[hh] = ip[:, sl].astype(BF16)
            sg_ref[hh] = sg[:, sl].astype(BF16)

    @pl.when(pl.program_id(1) == 0)
    def _():
        xn = (_rms_rows(h_ref[...]) * g_ref[...]).astype(BF16)
        xn_ref[...] = xn
        project(xn)

    @pl.when(pl.program_id(1) > 0)
    def _():
        project(xn_ref[...])


def _hgrn_in(hp, gain, w_in, j, lb_logits, layer):
    lp = hp.shape[0]
    nc = D_MODEL // PROJ_TILE
    hpt = PROJ_TILE // HEAD_DIM
    w = _cast_proj_weights(w_in, j, 4 * D_MODEL)
    wspec = lambda s: pl.BlockSpec((D_MODEL, PROJ_TILE), lambda i, c, s=s: (0, s * nc + c))
    hm_spec = pl.BlockSpec((hpt, ROW_TILE, HEAD_DIM), lambda i, c: (c, i, 0))
    hm = lambda dt: jax.ShapeDtypeStruct((HEADS, lp, HEAD_DIM), dt)
    return pl.pallas_call(
        functools.partial(_hgrn_in_kernel, layer=layer),
        grid=(lp // ROW_TILE, nc),
        in_specs=[
            pl.BlockSpec((ROW_TILE, D_MODEL), lambda i, c: (i, 0)),
            pl.BlockSpec((1, D_MODEL), lambda i, c: (0, 0)),
            wspec(0), wspec(1), wspec(2), wspec(3),
            pl.BlockSpec((DEPTH + 1, PROJ_TILE), lambda i, c: (0, c)),
        ],
        out_specs=[hm_spec] * 5,
        out_shape=[hm(BF16), hm(BF16), hm(F32), hm(BF16), hm(BF16)],
        scratch_shapes=[pltpu.VMEM((ROW_TILE, D_MODEL), BF16)],
        compiler_params=_params("parallel", "arbitrary"),
        name="hgrn_in",
    )(hp, gain.reshape(1, D_MODEL), w, w, w, w, lb_logits)


_ROW_LEVELS = (64, 32, 16, 8)
_MM_LEVELS = (4, 2)
HGRN_UNROLL = 4


def _hgrn_consts():
    t = np.arange(BLOCK)[:, None]
    j = np.arange(BLOCK)[None, :]
    blocks = [j <= t]
    for b in _MM_LEVELS:
        ref = 2 * b * (t // (2 * b)) + b - 1
        odd = (t // b) % 2 == 1
        blocks.append(np.where(odd, (j > ref) & (j <= t), (j > t) & (j <= ref)))
    g_all = np.concatenate(blocks, axis=0).astype(np.float32)
    masks = []
    for b in _ROW_LEVELS + _MM_LEVELS + (1,):
        masks.append((t // (2 * b) == j // (2 * b)) & ((t // b) % 2 == 1) & ((j // b) % 2 == 0))
    masks.append(t == j)
    return jnp.asarray(g_all, BF16), jnp.asarray(np.stack(masks).astype(np.float32))


def _hgrn_rec_kernel(q_ref, k_ref, lf_ref, v_ref, sg_ref, gmat_ref, mask_ref, on_ref,
                     o_ref, st_ref, ob_ref):
    @pl.when(pl.program_id(0) == 0)
    def _():
        st_ref[...] = jnp.zeros_like(st_ref)

    def boundary_exponent(b, size):
        parts = []
        for s in range(0, BLOCK, 2 * size):
            r = b[s + size - 1:s + size]
            parts += [r - b[s:s + size], b[s + size:s + 2 * size] - r]
        return jnp.concatenate(parts, axis=0)

    def heads(hg, carry):
        hs = [hg * HGRN_UNROLL + u for u in range(HGRN_UNROLL)]
        lf = [lf_ref[h] for h in hs]
        q = [q_ref[h] for h in hs]
        k = [k_ref[h] for h in hs]
        x3 = [jnp.dot(gmat_ref[...], jnp.concatenate(_split3(l), axis=1),
                      preferred_element_type=F32) for l in lf]
        x = [t[:, :HEAD_DIM] + t[:, HEAD_DIM:2 * HEAD_DIM] + t[:, 2 * HEAD_DIM:] for t in x3]
        b = [t[:BLOCK] for t in x]
        b_last = [t[BLOCK - 1:BLOCK] for t in b]
        att = [jnp.zeros((BLOCK, BLOCK), F32) for _ in hs]
        n_row, n_mm = len(_ROW_LEVELS), len(_MM_LEVELS)
        for lv in range(n_row + n_mm + 2):
            for u in range(HGRN_UNROLL):
                if lv < n_row:
                    el = jnp.exp2(boundary_exponent(b[u], _ROW_LEVELS[lv])).astype(BF16)
                    qt, kt = q[u] * el, k[u] * el
                elif lv < n_row + n_mm:
                    m = lv - n_row + 1
                    el = jnp.exp2(x[u][m * BLOCK:(m + 1) * BLOCK]).astype(BF16)
                    qt, kt = q[u] * el, k[u] * el
                elif lv == n_row + n_mm:
                    qt, kt = q[u] * jnp.exp2(lf[u]).astype(BF16), k[u]
                else:
                    qt, kt = q[u], k[u]
                a = lax.dot_general(qt, kt, _NT, preferred_element_type=F32)
                att[u] = att[u] + mask_ref[lv] * a
        for u, h in enumerate(hs):
            qb = q[u] * jnp.exp2(b[u]).astype(BF16)
            kd = k[u] * jnp.exp2(b_last[u] - b[u]).astype(BF16)
            vt = v_ref[h].astype(F32).T.astype(BF16)
            st = st_ref[h]
            lhs = jnp.concatenate([att[u].astype(BF16), qb], axis=1)
            rhs = jnp.concatenate([vt, st.astype(BF16)], axis=1)
            o = lax.dot_general(lhs, rhs, _NT, preferred_element_type=F32)
            st_ref[h] = st * jnp.exp2(b_last[u]) + jnp.dot(vt, kd, preferred_element_type=F32)
            og = _rms_rows(o) * on_ref[...] * sg_ref[h].astype(F32)
            ob_ref[h] = og.astype(BF16)
        return carry

    lax.fori_loop(0, HEADS // HGRN_UNROLL, heads, 0)
    for h in range(HEADS):
        o_ref[:, h * HEAD_DIM:(h + 1) * HEAD_DIM] = ob_ref[h]


def _hgrn_rec(q, k, lf, v, sg, onorm):
    lp = q.shape[1]
    gmat, masks = _hgrn_consts()
    hm_spec = pl.BlockSpec((HEADS, BLOCK, HEAD_DIM), lambda c: (0, c, 0))
    return pl.pallas_call(
        _hgrn_rec_kernel,
        grid=(lp // BLOCK,),
        in_specs=[hm_spec] * 5 + [
            pl.BlockSpec(gmat.shape, lambda c: (0, 0)),
            pl.BlockSpec(masks.shape, lambda c: (0, 0, 0)),
            pl.BlockSpec((1, HEAD_DIM), lambda c: (0, 0)),
        ],
        out_specs=pl.BlockSpec((BLOCK, D_MODEL), lambda c: (c, 0)),
        out_shape=jax.ShapeDtypeStruct((lp, D_MODEL), BF16),
        scratch_shapes=[pltpu.VMEM((HEADS, HEAD_DIM, HEAD_DIM), F32),
                        pltpu.VMEM((HEADS, BLOCK, HEAD_DIM), BF16)],
        compiler_params=_params("arbitrary"),
        name="hgrn_rec",
    )(q, k, lf, v, sg, gmat, masks, onorm.reshape(1, HEAD_DIM))


def _fox_in_kernel(h_ref, g_ref, wq_ref, wk_ref, wv_ref, wg_ref, wf_ref, bf_ref, qn_ref, kn_ref,
                   q_ref, k_ref, v_ref, sg_ref, lf_ref, xn_ref):
    def project(xn):
        qp = jnp.dot(xn, wq_ref[...], preferred_element_type=F32)
        kp = jnp.dot(xn, wk_ref[...], preferred_element_type=F32)
        vp = jnp.dot(xn, wv_ref[...], preferred_element_type=F32)
        gp = jnp.dot(xn, wg_ref[...], preferred_element_type=F32)
        sg = _sigmoid(gp)
        scale = HEAD_DIM ** -0.5 * LOG2E
        for hh in range(PROJ_TILE // HEAD_DIM):
            sl = slice(hh * HEAD_DIM, (hh + 1) * HEAD_DIM)
            q_ref[hh] = (_rms_rows(qp[:, sl]) * (qn_ref[...] * scale)).astype(BF16)
            k_ref[hh] = (_rms_rows(kp[:, sl]) * kn_ref[...]).astype(BF16)
            v_ref[hh] = vp[:, sl].astype(BF16)
            sg_ref[hh] = sg[:, sl].astype(BF16)

    @pl.when(pl.program_id(1) == 0)
    def _():
        xn = (_rms_rows(h_ref[...]) * g_ref[...]).astype(BF16)
        xn_ref[...] = xn
        z = lax.dot_general(xn, wf_ref[...], _NT, preferred_element_type=F32) + bf_ref[...]
        lf_ref[...] = jnp.minimum(z, 0.0) - jnp.log(1.0 + jnp.exp(-jnp.abs(z)))
        project(xn)

    @pl.when(pl.program_id(1) > 0)
    def _():
        project(xn_ref[...])


def _fox_in(hp, gain, w_in, j, b_f, qnorm, knorm):
    lp = hp.shape[0]
    nc = D_MODEL // PROJ_TILE
    hpt = PROJ_TILE // HEAD_DIM
    w, wf = _cast_proj_weights(w_in, j, 4 * D_MODEL)
    wspec = lambda s: pl.BlockSpec((D_MODEL, PROJ_TILE), lambda i, c, s=s: (0, s * nc + c))
    hm_spec = pl.BlockSpec((hpt, ROW_TILE, HEAD_DIM), lambda i, c: (c, i, 0))
    hm = jax.ShapeDtypeStruct((HEADS, lp, HEAD_DIM), BF16)
    vec = lambda n: pl.BlockSpec((1, n), lambda i, c: (0, 0))
    return pl.pallas_call(
        _fox_in_kernel,
        grid=(lp // ROW_TILE, nc),
        in_specs=[
            pl.BlockSpec((ROW_TILE, D_MODEL), lambda i, c: (i, 0)),
            vec(D_MODEL),
            wspec(0), wspec(1), wspec(2), wspec(3),
            pl.BlockSpec((HEADS, D_MODEL), lambda i, c: (0, 0)),
            vec(HEADS), vec(HEAD_DIM), vec(HEAD_DIM),
        ],
        out_specs=[hm_spec] * 4 + [pl.BlockSpec((ROW_TILE, HEADS), lambda i, c: (i, 0))],
        out_shape=[hm, hm, hm, hm, jax.ShapeDtypeStruct((lp, HEADS), F32)],
        scratch_shapes=[pltpu.VMEM((ROW_TILE, D_MODEL), BF16)],
        compiler_params=_params("parallel", "arbitrary"),
        name="fox_in",
    )(hp, gain.reshape(1, D_MODEL), w, w, w, w, wf, b_f.reshape(1, HEADS),
      qnorm.reshape(1, HEAD_DIM), knorm.reshape(1, HEAD_DIM))


def _fox_prep_consts():
    pq = np.zeros((HEADS, LANES, 4 * HEADS), np.float32)
    pk = np.zeros((HEADS, 4 * HEADS, LANES), np.float32)
    one, pad = 3 * HEADS, 3 * HEADS + 1
    for h in range(HEADS):
        for part in range(3):
            pq[h, part, part * HEADS + h] = 1.0
            pq[h, 3 + part, one] = 1.0
            pk[h, one, part] = 1.0
            pk[h, part * HEADS + h, 3 + part] = -1.0
        pk[h, pad, 3] = NEG_BIG
    return jnp.asarray(pq, BF16), jnp.asarray(pk, BF16)


def _fox_prep_kernel(lf_ref, tri_ref, pq_ref, pk_ref, q_ref, k_ref, v_ref,
                     qqt_ref, kk_ref, vt_ref, qn_ref, nmax_ref, carry_ref):
    i = pl.program_id(0)

    @pl.when(i == 0)
    def _():
        carry_ref[...] = jnp.zeros_like(carry_ref)
        nmax_ref[...] = jnp.zeros_like(nmax_ref)

    row16 = i * BLOCK + lax.broadcasted_iota(jnp.int32, (BLOCK, HEADS), 0)
    lf = jnp.where(row16 >= META_PAD, lf_ref[...], 0.0)
    tri = tri_ref[...]
    c = carry_ref[...]
    for part in _split3(lf):
        c = c + jnp.dot(tri, part, preferred_element_type=F32)
    carry_ref[...] = c[BLOCK - 1:BLOCK]
    col16 = lax.broadcasted_iota(jnp.int32, (BLOCK, HEADS), 1)
    extra = jnp.where(col16 == 0, 1.0, jnp.where((col16 == 1) & (row16 < META_PAD), 1.0, 0.0))
    src = jnp.concatenate([p.astype(F32) for p in _split3(c * LOG2E)] + [extra], axis=1)
    src_t = jnp.concatenate([src, jnp.zeros_like(src)], axis=1).T[:4 * HEADS].astype(BF16)
    src = src.astype(BF16)
    ones = jnp.ones((HEAD_DIM, LANES), BF16)
    for h in range(HEADS):
        aq_t = jnp.dot(pq_ref[h], src_t, preferred_element_type=F32)
        ak = jnp.dot(src, pk_ref[h], preferred_element_type=F32)
        qt = q_ref[h].astype(F32).T
        qqt_ref[h, :HEAD_DIM, :] = qt.astype(BF16)
        qqt_ref[h, HEAD_DIM:, :] = aq_t.astype(BF16)
        qn = jnp.sqrt(jnp.sum(qt * qt, axis=0, keepdims=True))
        qn_ref[h] = qn
        kf = k_ref[h].astype(F32)
        kn2 = jnp.dot((kf * kf).astype(BF16), ones, preferred_element_type=F32)
        kn = jnp.sqrt(jnp.max(kn2, axis=0, keepdims=True)) * NORM_SLACK
        nmax_ref[h:h + 1, :] = jnp.maximum(nmax_ref[h:h + 1, :], qn)
        nmax_ref[HEADS + h:HEADS + h + 1, :] = jnp.maximum(nmax_ref[HEADS + h:HEADS + h + 1, :], kn)
        kk_ref[h, :, :HEAD_DIM] = k_ref[h]
        kk_ref[h, :, HEAD_DIM:] = ak.astype(BF16)
        vt_ref[h, :HEAD_DIM, :] = v_ref[h].astype(F32).T.astype(BF16)
        vt_ref[h, HEAD_DIM:, :] = jnp.ones((V_ROWS - HEAD_DIM, BLOCK), BF16)


def _fox_prep(lf, q, k, v):
    lp = lf.shape[0]
    kdim = HEAD_DIM + LANES
    tri = jnp.asarray(np.tril(np.ones((BLOCK, BLOCK), np.float32)), BF16)
    pq, pk = _fox_prep_consts()
    hm_spec = pl.BlockSpec((HEADS, BLOCK, HEAD_DIM), lambda i: (0, i, 0))
    return pl.pallas_call(
        _fox_prep_kernel,
        grid=(lp // BLOCK,),
        in_specs=[pl.BlockSpec((BLOCK, HEADS), lambda i: (i, 0)),
                  pl.BlockSpec((BLOCK, BLOCK), lambda i: (0, 0)),
                  pl.BlockSpec(pq.shape, lambda i: (0, 0, 0)),
                  pl.BlockSpec(pk.shape, lambda i: (0, 0, 0)),
                  hm_spec, hm_spec, hm_spec],
        out_specs=[pl.BlockSpec((HEADS, kdim, BLOCK), lambda i: (0, 0, i)),
                   pl.BlockSpec((HEADS, BLOCK, kdim), lambda i: (0, i, 0)),
                   pl.BlockSpec((HEADS, V_ROWS, BLOCK), lambda i: (0, 0, i)),
                   pl.BlockSpec((HEADS, 1, BLOCK), lambda i: (0, 0, i)),
                   pl.BlockSpec((2 * HEADS, LANES), lambda i: (0, 0))],
        out_shape=[jax.ShapeDtypeStruct((HEADS, kdim, lp), BF16),
                   jax.ShapeDtypeStruct((HEADS, lp, kdim), BF16),
                   jax.ShapeDtypeStruct((HEADS, V_ROWS, lp), BF16),
                   jax.ShapeDtypeStruct((HEADS, 1, lp), F32),
                   jax.ShapeDtypeStruct((2 * HEADS, LANES), F32)],
        scratch_shapes=[pltpu.VMEM((1, HEADS), F32)],
        compiler_params=_params("arbitrary"),
        name="fox_prep",
    )(lf, tri, pq, pk, q, k, v)


def _fox_attn_kernel(safe_ref, kmax_ref, qqt_ref, qn_ref, kk_ref, vt_ref, sg_ref, o_ref,
                     m_ref, acc_ref, ring_ref):
    h = pl.program_id(0)
    i = pl.program_id(1)
    qs = i * ATT_Q
    n_groups = ATT_Q // ATT_GROUP
    n_steps = i * (ATT_Q // ATT_K)

    def scores(item):
        g, ks, nk, _ = item
        gs = slice(g * ATT_GROUP, (g + 1) * ATT_GROUP)
        return jnp.dot(kk_ref[pl.ds(ks, nk), :], qqt_ref[:, gs], preferred_element_type=F32)

    def step_items(j):
        return [(g, pl.multiple_of(j * ATT_K + sum(ATT_KSPLIT[:sub]), ATT_GROUP), nk, False)
                for sub, nk in enumerate(ATT_KSPLIT) for g in range(n_groups)]

    def sweep(fixed_shift):
        acc_ref[...] = jnp.zeros_like(acc_ref)
        if fixed_shift:
            m_ref[...] = qn_ref[...] * kmax_ref[h]
        else:
            m_ref[...] = jnp.full_like(m_ref, -jnp.inf)

        def update(item, s):
            g, ks, nk, masked = item
            gs = slice(g * ATT_GROUP, (g + 1) * ATT_GROUP)
            if masked:
                r = lax.broadcasted_iota(jnp.int32, s.shape, 0)
                c = lax.broadcasted_iota(jnp.int32, s.shape, 1) + g * ATT_GROUP
                s = jnp.where(r <= c, s, NEG_BIG)
            if fixed_shift:
                p = jnp.exp2(s - m_ref[:, gs])
                acc_ref[:HEAD_DIM, gs] += jnp.dot(vt_ref[:HEAD_DIM, pl.ds(ks, nk)], p.astype(BF16),
                                                  preferred_element_type=F32)
                acc_ref[HEAD_DIM:HEAD_DIM + 1, gs] += jnp.sum(p, axis=0, keepdims=True)
            else:
                vt = vt_ref[:, pl.ds(ks, nk)]
                m_prev = m_ref[:, gs]
                m_new = jnp.maximum(m_prev, jnp.max(s, axis=0, keepdims=True))
                alpha = jnp.exp2(m_prev - m_new)
                p = jnp.exp2((s - m_new).astype(BF16))
                m_ref[:, gs] = m_new
                acc_ref[:, gs] = alpha * acc_ref[:, gs] + jnp.dot(vt, p, preferred_element_type=F32)

        def run(items, first, ahead):
            pending = list(first) if first else [scores(item) for item in items[:ATT_AHEAD]]
            for n, item in enumerate(items):
                if n + ATT_AHEAD < len(items):
                    pending.append(scores(items[n + ATT_AHEAD]))
                else:
                    a = n + ATT_AHEAD - len(items)
                    ring_ref[a] = scores(ahead[a])
                update(item, pending.pop(0))

        run([(g, pl.multiple_of(qs, ATT_Q), (g + 1) * ATT_GROUP, True) for g in range(n_groups)],
            None, step_items(0)[:ATT_AHEAD])

        def body(j, carry):
            run(step_items(j), [ring_ref[a] for a in range(ATT_AHEAD)],
                step_items(jnp.minimum(j + 1, n_steps - 1))[:ATT_AHEAD])
            return carry

        lax.fori_loop(0, n_steps, body, 0)

    @pl.when(safe_ref[h] == 1)
    def _():
        sweep(True)

    @pl.when(safe_ref[h] != 1)
    def _():
        sweep(False)

    o = (acc_ref[:HEAD_DIM, :] / acc_ref[HEAD_DIM:HEAD_DIM + 1, :]).T * sg_ref[...].astype(F32)
    row = qs + lax.broadcasted_iota(jnp.int32, o.shape, 0)
    o_ref[...] = jnp.where(row >= META_PAD, o, 0.0).astype(BF16)


def _fox_attn(qqt, qn, nmax, kk, vt, sg):
    lp = kk.shape[1]
    kdim = kk.shape[2]
    qmax, kmax = jnp.max(nmax[:HEADS], axis=1), jnp.max(nmax[HEADS:], axis=1)
    safe = (2.0 * qmax * kmax <= SAFE_SHIFT_RANGE).astype(jnp.int32)
    grid_spec = pltpu.PrefetchScalarGridSpec(
        num_scalar_prefetch=2,
        grid=(HEADS, lp // ATT_Q),
        in_specs=[pl.BlockSpec((None, kdim, ATT_Q), lambda h, i, *_: (h, 0, i)),
                  pl.BlockSpec((None, 1, ATT_Q), lambda h, i, *_: (h, 0, i)),
                  pl.BlockSpec((None, lp, kdim), lambda h, i, *_: (h, 0, 0)),
                  pl.BlockSpec((None, V_ROWS, lp), lambda h, i, *_: (h, 0, 0)),
                  pl.BlockSpec((None, ATT_Q, HEAD_DIM), lambda h, i, *_: (h, i, 0))],
        out_specs=pl.BlockSpec((ATT_Q, HEAD_DIM), lambda h, i, *_: (i, h)),
        scratch_shapes=[pltpu.VMEM((1, ATT_Q), F32), pltpu.VMEM((V_ROWS, ATT_Q), F32),
                        pltpu.VMEM((ATT_AHEAD, ATT_KSPLIT[0], ATT_GROUP), F32)])
    return pl.pallas_call(
        _fox_attn_kernel,
        grid_spec=grid_spec,
        out_shape=jax.ShapeDtypeStruct((lp, D_MODEL), BF16),
        compiler_params=_params("parallel", "arbitrary"),
        name="fox_attn",
    )(safe, kmax, qqt, qn, kk, vt, sg)


def kernel(x, meta_tokens, norm_g, ffn_w_gu, ffn_w_down, lb_logits, hg_w_in, hg_w_out, hg_onorm,
           fox_w_in, fox_b_f, fox_w_out, fox_qnorm, fox_knorm):
    assert x.shape[0] == 1 and x.shape[2] == D_MODEL
    seq = x.shape[1]
    real = META_PAD + N_META + seq
    lp = -(-real // ATT_Q) * ATT_Q
    assert seq % BLOCK == 0
    head = jnp.concatenate([jnp.zeros((META_PAD, D_MODEL), F32), meta_tokens.astype(F32)], axis=0)
    wgu, wd = _cast_ffn_weights(ffn_w_gu.reshape(2 * DEPTH, D_MODEL, 2 * FFN_DIM),
                                ffn_w_down.reshape(2 * DEPTH, FFN_DIM, D_MODEL))
    hp = x[0]
    for layer in range(DEPTH):
        hp = _ffn(hp, head, norm_g[layer, 0], wgu, wd, 2 * layer, -1 if layer == 0 else 0, lp)
        j = layer // 2
        if layer % 2 == 0:
            q, k, lf, v, sg = _hgrn_in(hp, norm_g[layer, 1], hg_w_in, j, lb_logits, layer)
            og = _hgrn_rec(q, k, lf, v, sg, hg_onorm[j])
            hp = _out_proj(hp, og, hg_w_out[j])
        else:
            q, k, v, sg, lf = _fox_in(hp, norm_g[layer, 1], fox_w_in, j, fox_b_f[j],
                                      fox_qnorm[j], fox_knorm[j])
            qqt, kk, vt, qn, nmax = _fox_prep(lf, q, k, v)
            og = _fox_attn(qqt, qn, nmax, kk, vt, sg)
            hp = _out_proj(hp, og, fox_w_out[j])
        last = layer == DEPTH - 1
        hp = _ffn(hp, head, norm_g[layer, 2], wgu, wd, 2 * layer + 1,
                  1 if last else 0, seq if last else lp)
    return hp[None]
```

```python
import functools

import numpy as np
import jax
import jax.numpy as jnp
from jax import lax
from jax.experimental import pallas as pl
from jax.experimental.pallas import tpu as pltpu

F32 = jnp.float32
BF16 = jnp.bfloat16

D_MODEL = 2048
DEPTH = 2
N_META = 16
BLOCK = 128
META_PAD = (-N_META) % BLOCK
RMS_EPS = 1e-6
HEADS = 16
HEAD_DIM = 128
FFN_DIM = 5504
NEG_BIG = -1e30
LOG2E = 1.4426950408889634

LANES = 128
ROW_TILE = 640
FFN_TILE = 512
FFN_PAD = -(-FFN_DIM // FFN_TILE) * FFN_TILE
PROJ_TILE = 512
CAST_ROWS = 256
ATT_Q = 1280
ATT_K = 1280
ATT_KSPLIT = (512, 768)
ATT_GROUP = 256
ATT_AHEAD = 2
V_ROWS = HEAD_DIM + 16
NORM_SLACK = 1.01
SAFE_SHIFT_RANGE = 80.0
VMEM_LIMIT = 56 * 1024 * 1024

_NT = (((1,), (1,)), ((), ()))


def _params(*sem):
    return pltpu.CompilerParams(dimension_semantics=sem, vmem_limit_bytes=VMEM_LIMIT)


def _sigmoid(x):
    return 1.0 / (1.0 + jnp.exp(-x))


def _rms_rows(x):
    return x * lax.rsqrt(jnp.mean(x * x, axis=-1, keepdims=True) + RMS_EPS)


def _split3(x):
    hi = x.astype(BF16)
    r1 = x - hi.astype(F32)
    mid = r1.astype(BF16)
    lo = (r1 - mid.astype(F32)).astype(BF16)
    return hi, mid, lo


def _cast_wgu_kernel(w_ref, o_ref):
    zeros = jnp.zeros((CAST_ROWS, FFN_PAD - FFN_DIM), BF16)
    o_ref[:, :FFN_DIM] = w_ref[:, :FFN_DIM].astype(BF16)
    o_ref[:, FFN_DIM:FFN_PAD] = zeros
    o_ref[:, FFN_PAD:FFN_PAD + FFN_DIM] = w_ref[:, FFN_DIM:].astype(BF16)
    o_ref[:, FFN_PAD + FFN_DIM:] = zeros


def _cast_wd_kernel(w_ref, o_ref):
    row = pl.program_id(1) * FFN_TILE + lax.broadcasted_iota(jnp.int32, (FFN_TILE, D_MODEL), 0)
    o_ref[...] = jnp.where(row < FFN_DIM, w_ref[...], 0.0).astype(BF16)


def _cast_kernel(w_ref, o_ref):
    o_ref[...] = w_ref[...].astype(BF16)


def _cast_transposed_kernel(wt_ref, tail_ref, o_ref, t_ref):
    o_ref[...] = wt_ref[...].T.astype(BF16)
    t_ref[...] = tail_ref[...].astype(BF16)


def _cast_ffn_weights(w_gu, w_down):
    n = w_gu.shape[0]
    wgu = pl.pallas_call(
        _cast_wgu_kernel,
        grid=(n, D_MODEL // CAST_ROWS),
        in_specs=[pl.BlockSpec((None, CAST_ROWS, 2 * FFN_DIM), lambda a, r: (a, r, 0))],
        out_specs=pl.BlockSpec((None, CAST_ROWS, 2 * FFN_PAD), lambda a, r: (a, r, 0)),
        out_shape=jax.ShapeDtypeStruct((n, D_MODEL, 2 * FFN_PAD), BF16),
        compiler_params=_params("parallel", "parallel"),
        name="cast_wgu",
    )(w_gu)
    wd = pl.pallas_call(
        _cast_wd_kernel,
        grid=(n, FFN_PAD // FFN_TILE),
        in_specs=[pl.BlockSpec((None, FFN_TILE, D_MODEL), lambda a, r: (a, r, 0))],
        out_specs=pl.BlockSpec((None, FFN_TILE, D_MODEL), lambda a, r: (a, r, 0)),
        out_shape=jax.ShapeDtypeStruct((n, FFN_PAD, D_MODEL), BF16),
        compiler_params=_params("parallel", "parallel"),
        name="cast_wd",
    )(w_down)
    return wgu, wd


def _cast_proj_weights(w, j, n_main):
    width = w.shape[2]
    main = jax.ShapeDtypeStruct((D_MODEL, n_main), BF16)
    if width == n_main:
        return pl.pallas_call(
            _cast_kernel, grid=(D_MODEL // CAST_ROWS,),
            in_specs=[pl.BlockSpec((None, CAST_ROWS, width), lambda r: (j, r, 0))],
            out_specs=pl.BlockSpec((CAST_ROWS, n_main), lambda r: (r, 0)),
            out_shape=main, compiler_params=_params("parallel"), name="cast_proj")(w)
    tail = width - n_main
    assert n_main % tail == 0
    wt = jnp.swapaxes(w, 1, 2)
    return pl.pallas_call(
        _cast_transposed_kernel, grid=(n_main // CAST_ROWS,),
        in_specs=[pl.BlockSpec((None, CAST_ROWS, D_MODEL), lambda r: (j, r, 0)),
                  pl.BlockSpec((None, tail, D_MODEL), lambda r: (j, n_main // tail, 0))],
        out_specs=[pl.BlockSpec((D_MODEL, CAST_ROWS), lambda r: (0, r)),
                   pl.BlockSpec((tail, D_MODEL), lambda r: (0, 0))],
        out_shape=[main, jax.ShapeDtypeStruct((tail, D_MODEL), BF16)],
        compiler_params=_params("arbitrary"), name="cast_proj_t")(wt, wt)


ROW_CHUNKS = ROW_TILE // BLOCK


def _ffn_kernel(*refs, shift, n_src):
    n_in = ROW_CHUNKS if shift else 1
    chunk_refs = refs[:n_in]
    head_ref, g_ref, wg_ref, wu_ref, wd_ref, o_ref, xn_ref = refs[n_in:]

    j = pl.program_id(1)

    def step(xn):
        a = jnp.dot(xn, wg_ref[...], preferred_element_type=F32)
        b = jnp.dot(xn, wu_ref[...], preferred_element_type=F32)
        act = (a * (0.5 * _sigmoid(a)) * b).astype(BF16)
        return jnp.dot(act, wd_ref[...], preferred_element_type=F32)

    @pl.when(j == 0)
    def _():
        parts = []
        for u, ref in enumerate(chunk_refs):
            idx = ROW_CHUNKS * pl.program_id(0) + u + shift
            blk = ref[...]
            if shift < 0:
                blk = jnp.where(idx < 0, head_ref[...], jnp.where(idx >= n_src, 0.0, blk))
            parts.append(blk)
        x = jnp.concatenate(parts, axis=0) if shift else parts[0]
        xn = (_rms_rows(x) * g_ref[...]).astype(BF16)
        xn_ref[...] = xn
        o_ref[...] = x + step(xn)

    @pl.when(j > 0)
    def _():
        o_ref[...] += step(xn_ref[...])


def _ffn(src, head, gain, wgu, wd, a, shift, out_rows):
    n_src = src.shape[0] // BLOCK
    nf = FFN_PAD // FFN_TILE
    chunk = lambda u: pl.BlockSpec(
        (BLOCK, D_MODEL), lambda i, j: (jnp.clip(ROW_CHUNKS * i + u + shift, 0, n_src - 1), 0))
    row_specs = ([chunk(u) for u in range(ROW_CHUNKS)] if shift else
                 [pl.BlockSpec((ROW_TILE, D_MODEL), lambda i, j: (i, 0))])
    return pl.pallas_call(
        functools.partial(_ffn_kernel, shift=shift, n_src=n_src),
        grid=(pl.cdiv(out_rows, ROW_TILE), nf),
        in_specs=row_specs + [
            pl.BlockSpec((BLOCK, D_MODEL), lambda i, j: (0, 0)),
            pl.BlockSpec((1, D_MODEL), lambda i, j: (0, 0)),
            pl.BlockSpec((None, D_MODEL, FFN_TILE), lambda i, j: (a, 0, j)),
            pl.BlockSpec((None, D_MODEL, FFN_TILE), lambda i, j: (a, 0, j + nf)),
            pl.BlockSpec((None, FFN_TILE, D_MODEL), lambda i, j: (a, j, 0)),
        ],
        out_specs=pl.BlockSpec((ROW_TILE, D_MODEL), lambda i, j: (i, 0)),
        out_shape=jax.ShapeDtypeStruct((out_rows, D_MODEL), F32),
        scratch_shapes=[pltpu.VMEM((ROW_TILE, D_MODEL), BF16)],
        compiler_params=_params("parallel", "arbitrary"),
        name="ffn",
    )(*([src] * len(row_specs)), head, gain.reshape(1, D_MODEL), wgu, wgu, wd)


def _out_proj_kernel(h_ref, o_ref, w_ref, out_ref):
    out_ref[...] = h_ref[...] + jnp.dot(o_ref[...], w_ref[...], preferred_element_type=F32)


def _out_proj(hp, og, w_out):
    lp = hp.shape[0]
    return pl.pallas_call(
        _out_proj_kernel,
        grid=(lp // ROW_TILE,),
        in_specs=[
            pl.BlockSpec((ROW_TILE, D_MODEL), lambda i: (i, 0)),
            pl.BlockSpec((ROW_TILE, D_MODEL), lambda i: (i, 0)),
            pl.BlockSpec((D_MODEL, D_MODEL), lambda i: (0, 0)),
        ],
        out_specs=pl.BlockSpec((ROW_TILE, D_MODEL), lambda i: (i, 0)),
        out_shape=jax.ShapeDtypeStruct((lp, D_MODEL), F32),
        compiler_params=_params("parallel"),
        name="out_proj",
    )(hp, og, w_out.astype(BF16))


def _hgrn_in_kernel(h_ref, g_ref, wq_ref, wf_ref, wi_ref, wg_ref, lbl_ref,
                    q_ref, k_ref, lf_ref, v_ref, sg_ref, xn_ref, *, layer):
    def project(xn):
        fp = jnp.dot(xn, wf_ref[...], preferred_element_type=F32)
        gp = jnp.dot(xn, wg_ref[...], preferred_element_type=F32)
        qp = jnp.dot(xn, wq_ref[...], preferred_element_type=F32)
        ip = jnp.dot(xn, wi_ref[...], preferred_element_type=F32)
        lg = lbl_ref[...]
        e = jnp.exp(lg - jnp.max(lg, axis=0, keepdims=True))
        lb = jnp.sum(e[:layer + 1], axis=0, keepdims=True) / jnp.sum(e, axis=0, keepdims=True)
        q = qp * _sigmoid(qp)
        fg = lb + (1.0 - lb) * _sigmoid(fp)
        k = 1.0 - fg
        lf = jnp.log(fg) * LOG2E
        sg = _sigmoid(gp)
        for hh in range(PROJ_TILE // HEAD_DIM):
            sl = slice(hh * HEAD_DIM, (hh + 1) * HEAD_DIM)
            q_ref[hh] = q[:, sl].astype(BF16)
            k_ref[hh] = k[:, sl].astype(BF16)
            lf_ref[hh] = lf[:, sl]
            v_ref[hh] = ip[:, sl].astype(BF16)
            sg_ref[hh] = sg[:, sl].astype(BF16)

    @pl.when(pl.program_id(1) == 0)
    def _():
        xn = (_rms_rows(h_ref[...]) * g_ref[...]).astype(BF16)
        xn_ref[...] = xn
        project(xn)

    @pl.when(pl.program_id(1) > 0)
    def _():
        project(xn_ref[...])


def _hgrn_in(hp, gain, w_in, j, lb_logits, layer):
    lp = hp.shape[0]
    nc = D_MODEL // PROJ_TILE
    hpt = PROJ_TILE // HEAD_DIM
    w = _cast_proj_weights(w_in, j, 4 * D_MODEL)
    wspec = lambda s: pl.BlockSpec((D_MODEL, PROJ_TILE), lambda i, c, s=s: (0, s * nc + c))
    hm_spec = pl.BlockSpec((hpt, ROW_TILE, HEAD_DIM), lambda i, c: (c, i, 0))
    hm = lambda dt: jax.ShapeDtypeStruct((HEADS, lp, HEAD_DIM), dt)
    return pl.pallas_call(
        functools.partial(_hgrn_in_kernel, layer=layer),
        grid=(lp // ROW_TILE, nc),
        in_specs=[
            pl.BlockSpec((ROW_TILE, D_MODEL), lambda i, c: (i, 0)),
            pl.BlockSpec((1, D_MODEL), lambda i, c: (0, 0)),
            wspec(0), wspec(1), wspec(2), wspec(3),
            pl.BlockSpec((DEPTH + 1, PROJ_TILE), lambda i, c: (0, c)),
        ],
        out_specs=[hm_spec] * 5,
        out_shape=[hm(BF16), hm(BF16), hm(F32), hm(BF16), hm(BF16)],
        scratch_shapes=[pltpu.VMEM((ROW_TILE, D_MODEL), BF16)],
        compiler_params=_params("parallel", "arbitrary"),
        name="hgrn_in",
    )(hp, gain.reshape(1, D_MODEL), w, w, w, w, lb_logits)


_ROW_LEVELS = (64, 32, 16, 8)
_MM_LEVELS = (4, 2)


def _hgrn_consts():
    t = np.arange(BLOCK)[:, None]
    j = np.arange(BLOCK)[None, :]
    blocks = [j <= t]
    for b in _MM_LEVELS:
        ref = 2 * b * (t // (2 * b)) + b - 1
        odd = (t // b) % 2 == 1
        blocks.append(np.where(odd, (j > ref) & (j <= t), (j > t) & (j <= ref)))
    g_all = np.concatenate(blocks, axis=0).astype(np.float32)
    masks = []
    for b in _ROW_LEVELS + _MM_LEVELS + (1,):
        masks.append((t // (2 * b) == j // (2 * b)) & ((t // b) % 2 == 1) & ((j // b) % 2 == 0))
    masks.append(t == j)
    return jnp.asarray(g_all, BF16), jnp.asarray(np.stack(masks).astype(np.float32))


def _hgrn_rec_kernel(q_ref, k_ref, lf_ref, v_ref, sg_ref, gmat_ref, mask_ref, on_ref, h_ref, w_ref,
                     out_ref, st_ref, og_ref):
    c = pl.program_id(0)

    @pl.when(c == 0)
    def _():
        st_ref[...] = jnp.zeros_like(st_ref)

    def boundary_exponent(b, size):
        parts = []
        for s in range(0, BLOCK, 2 * size):
            r = b[s + size - 1:s + size]
            parts += [r - b[s:s + size], b[s + size:s + 2 * size] - r]
        return jnp.concatenate(parts, axis=0)

    def step(project):
        hs = range(HEADS)
        lf = [lf_ref[h] for h in hs]
        q = [q_ref[h] for h in hs]
        k = [k_ref[h] for h in hs]
        x3 = [jnp.dot(gmat_ref[...], jnp.concatenate(_split3(l), axis=1),
                      preferred_element_type=F32) for l in lf]
        x = [t[:, :HEAD_DIM] + t[:, HEAD_DIM:2 * HEAD_DIM] + t[:, 2 * HEAD_DIM:] for t in x3]
        b = [t[:BLOCK] for t in x]
        b_last = [t[BLOCK - 1:BLOCK] for t in b]
        att = [jnp.zeros((BLOCK, BLOCK), F32) for _ in hs]
        n_row, n_mm = len(_ROW_LEVELS), len(_MM_LEVELS)
        n_lv = n_row + n_mm + 2
        width = D_MODEL // n_lv
        for lv in range(n_lv):
            for u in hs:
                if lv < n_row:
                    el = jnp.exp2(boundary_exponent(b[u], _ROW_LEVELS[lv])).astype(BF16)
                    qt, kt = q[u] * el, k[u] * el
                elif lv < n_row + n_mm:
                    m = lv - n_row + 1
                    el = jnp.exp2(x[u][m * BLOCK:(m + 1) * BLOCK]).astype(BF16)
                    qt, kt = q[u] * el, k[u] * el
                elif lv == n_row + n_mm:
                    qt, kt = q[u] * jnp.exp2(lf[u]).astype(BF16), k[u]
                else:
                    qt, kt = q[u], k[u]
                a = lax.dot_general(qt, kt, _NT, preferred_element_type=F32)
                att[u] = att[u] + mask_ref[lv] * a
            if project:
                sl = slice(lv * width, (lv + 1) * width)
                out_ref[:, sl] = h_ref[:, sl] + jnp.dot(og_ref[...], w_ref[:, sl],
                                                        preferred_element_type=F32)
        rows = pl.ds(pl.multiple_of((c % 2) * BLOCK, BLOCK), BLOCK)
        for h in hs:
            qb = q[h] * jnp.exp2(b[h]).astype(BF16)
            kd = k[h] * jnp.exp2(b_last[h] - b[h]).astype(BF16)
            vt = v_ref[h].astype(F32).T.astype(BF16)
            st = st_ref[h]
            lhs = jnp.concatenate([att[h].astype(BF16), qb], axis=1)
            rhs = jnp.concatenate([vt, st.astype(BF16)], axis=1)
            o = lax.dot_general(lhs, rhs, _NT, preferred_element_type=F32)
            st_ref[h] = st * jnp.exp2(b_last[h]) + jnp.dot(vt, kd, preferred_element_type=F32)
            og = _rms_rows(o) * on_ref[...] * sg_ref[h].astype(F32)
            og_ref[rows, h * HEAD_DIM:(h + 1) * HEAD_DIM] = og.astype(BF16)

    project_now = (c % 2 == 0) & (c > 0)

    @pl.when(project_now)
    def _():
        step(True)

    @pl.when(jnp.logical_not(project_now))
    def _():
        step(False)


def _hgrn_rec(q, k, lf, v, sg, onorm, hp, w_out):
    lp = q.shape[1]
    n = lp // BLOCK
    assert n % 2 == 0
    gmat, masks = _hgrn_consts()
    assert D_MODEL % masks.shape[0] == 0
    hm_spec = pl.BlockSpec((HEADS, BLOCK, HEAD_DIM), lambda c: (0, jnp.minimum(c, n - 1), 0))
    row_spec = pl.BlockSpec((2 * BLOCK, D_MODEL), lambda c: (jnp.maximum(c // 2 - 1, 0), 0))
    return pl.pallas_call(
        _hgrn_rec_kernel,
        grid=(n + 1,),
        in_specs=[hm_spec] * 5 + [
            pl.BlockSpec(gmat.shape, lambda c: (0, 0)),
            pl.BlockSpec(masks.shape, lambda c: (0, 0, 0)),
            pl.BlockSpec((1, HEAD_DIM), lambda c: (0, 0)),
            row_spec,
            pl.BlockSpec((D_MODEL, D_MODEL), lambda c: (0, 0)),
        ],
        out_specs=row_spec,
        out_shape=jax.ShapeDtypeStruct((lp, D_MODEL), F32),
        scratch_shapes=[pltpu.VMEM((HEADS, HEAD_DIM, HEAD_DIM), F32),
                        pltpu.VMEM((2 * BLOCK, D_MODEL), BF16)],
        compiler_params=_params("arbitrary"),
        name="hgrn_rec",
    )(q, k, lf, v, sg, gmat, masks, onorm.reshape(1, HEAD_DIM), hp, w_out.astype(BF16))


def _fox_consts():
    pq = np.zeros((HEADS, LANES, 4 * HEADS), np.float32)
    pk = np.zeros((HEADS, 4 * HEADS, LANES), np.float32)
    one, pad = 3 * HEADS, 3 * HEADS + 1
    for h in range(HEADS):
        for part in range(3):
            pq[h, part, part * HEADS + h] = 1.0
            pq[h, 3 + part, one] = 1.0
            pk[h, one, part] = 1.0
            pk[h, part * HEADS + h, 3 + part] = -1.0
        pk[h, pad, 3] = NEG_BIG
    tri = np.tril(np.ones((ROW_TILE, ROW_TILE), np.float32))
    return jnp.asarray(pq, BF16), jnp.asarray(pk, BF16), jnp.asarray(tri, BF16)


def _fox_in_kernel(h_ref, g_ref, wq_ref, wk_ref, wv_ref, wg_ref, wf_ref, bf_ref, gq_ref, gk_ref,
                   tri_ref, pq_ref, pk_ref,
                   qqt_ref, kk_ref, vt_ref, sg_ref, qn_ref, qmax_ref, kmax_ref,
                   xn_ref, src_ref, srct_ref, carry_ref):
    i = pl.program_id(0)
    c = pl.program_id(1)
    hpt = PROJ_TILE // HEAD_DIM

    def project(xn):
        gp = jnp.dot(xn, wg_ref[...], preferred_element_type=F32)
        qp = jnp.dot(xn, wq_ref[...], preferred_element_type=F32)
        kp = jnp.dot(xn, wk_ref[...], preferred_element_type=F32)
        vp = jnp.dot(xn, wv_ref[...], preferred_element_type=F32)
        sg = _sigmoid(gp)
        scale = HEAD_DIM ** -0.5 * LOG2E
        ones = jnp.ones((HEAD_DIM, LANES), BF16)
        for hh in range(hpt):
            sl = slice(hh * HEAD_DIM, (hh + 1) * HEAD_DIM)
            head = hpt * c + hh
            q = (_rms_rows(qp[:, sl]) * (gq_ref[...] * scale)).astype(BF16)
            k = (_rms_rows(kp[:, sl]) * gk_ref[...]).astype(BF16)
            qt = q.astype(F32).T
            qqt_ref[hh, :HEAD_DIM, :] = qt.astype(BF16)
            qqt_ref[hh, HEAD_DIM:, :] = jnp.dot(
                pq_ref[hh], srct_ref[...], preferred_element_type=F32).astype(BF16)
            kk_ref[hh, :, :HEAD_DIM] = k
            kk_ref[hh, :, HEAD_DIM:] = jnp.dot(
                src_ref[...], pk_ref[hh], preferred_element_type=F32).astype(BF16)
            vt_ref[hh, :HEAD_DIM, :] = vp[:, sl].T.astype(BF16)
            vt_ref[hh, HEAD_DIM:, :] = jnp.ones((V_ROWS - HEAD_DIM, ROW_TILE), BF16)
            sg_ref[hh] = sg[:, sl].astype(BF16)
            qn = jnp.sqrt(jnp.sum(qt * qt, axis=0, keepdims=True))
            qn_ref[hh] = qn
            kf = k.astype(F32)
            kn2 = jnp.dot((kf * kf).astype(BF16), ones, preferred_element_type=F32)
            kn = jnp.sqrt(jnp.max(kn2, axis=0, keepdims=True)) * NORM_SLACK
            qmax_ref[head] = jnp.maximum(qmax_ref[head], qn)
            kmax_ref[head] = jnp.maximum(kmax_ref[head], kn)

    @pl.when((i == 0) & (c == 0))
    def _():
        carry_ref[...] = jnp.zeros_like(carry_ref)
        qmax_ref[...] = jnp.zeros_like(qmax_ref)
        kmax_ref[...] = jnp.zeros_like(kmax_ref)

    @pl.when(c == 0)
    def _():
        xn = (_rms_rows(h_ref[...]) * g_ref[...]).astype(BF16)
        xn_ref[...] = xn
        z = lax.dot_general(xn, wf_ref[...], _NT, preferred_element_type=F32) + bf_ref[...]
        row = i * ROW_TILE + lax.broadcasted_iota(jnp.int32, (ROW_TILE, HEADS), 0)
        lf = jnp.minimum(z, 0.0) - jnp.log(1.0 + jnp.exp(-jnp.abs(z)))
        lf = jnp.where(row >= META_PAD, lf, 0.0)
        cum = carry_ref[...]
        for part in _split3(lf):
            cum = cum + jnp.dot(tri_ref[...], part, preferred_element_type=F32)
        carry_ref[...] = cum[ROW_TILE - 1:ROW_TILE]
        col = lax.broadcasted_iota(jnp.int32, (ROW_TILE, HEADS), 1)
        extra = jnp.where(col == 0, 1.0, jnp.where((col == 1) & (row < META_PAD), 1.0, 0.0))
        src = jnp.concatenate([p.astype(F32) for p in _split3(cum * LOG2E)] + [extra], axis=1)
        src_ref[...] = src.astype(BF16)
        srct_ref[...] = jnp.concatenate([src, jnp.zeros_like(src)], axis=1).T[:4 * HEADS].astype(BF16)
        project(xn)

    @pl.when(c > 0)
    def _():
        project(xn_ref[...])


def _fox_in(hp, gain, w_in, j, b_f, qnorm, knorm):
    lp = hp.shape[0]
    nc = D_MODEL // PROJ_TILE
    hpt = PROJ_TILE // HEAD_DIM
    kdim = HEAD_DIM + LANES
    w, wf = _cast_proj_weights(w_in, j, 4 * D_MODEL)
    pq, pk, tri = _fox_consts()
    wspec = lambda s: pl.BlockSpec((D_MODEL, PROJ_TILE), lambda i, c, s=s: (0, s * nc + c))
    vec = lambda n: pl.BlockSpec((1, n), lambda i, c: (0, 0))
    return pl.pallas_call(
        _fox_in_kernel,
        grid=(lp // ROW_TILE, nc),
        in_specs=[
            pl.BlockSpec((ROW_TILE, D_MODEL), lambda i, c: (i, 0)),
            vec(D_MODEL),
            wspec(0), wspec(1), wspec(2), wspec(3),
            pl.BlockSpec((HEADS, D_MODEL), lambda i, c: (0, 0)),
            vec(HEADS), vec(HEAD_DIM), vec(HEAD_DIM),
            pl.BlockSpec((ROW_TILE, ROW_TILE), lambda i, c: (0, 0)),
            pl.BlockSpec((hpt, LANES, 4 * HEADS), lambda i, c: (c, 0, 0)),
            pl.BlockSpec((hpt, 4 * HEADS, LANES), lambda i, c: (c, 0, 0)),
        ],
        out_specs=[pl.BlockSpec((hpt, kdim, ROW_TILE), lambda i, c: (c, 0, i)),
                   pl.BlockSpec((hpt, ROW_TILE, kdim), lambda i, c: (c, i, 0)),
                   pl.BlockSpec((hpt, V_ROWS, ROW_TILE), lambda i, c: (c, 0, i)),
                   pl.BlockSpec((hpt, ROW_TILE, HEAD_DIM), lambda i, c: (c, i, 0)),
                   pl.BlockSpec((hpt, 1, ROW_TILE), lambda i, c: (c, 0, i)),
                   pl.BlockSpec((HEADS, 1, ROW_TILE), lambda i, c: (0, 0, 0)),
                   pl.BlockSpec((HEADS, 1, LANES), lambda i, c: (0, 0, 0))],
        out_shape=[jax.ShapeDtypeStruct((HEADS, kdim, lp), BF16),
                   jax.ShapeDtypeStruct((HEADS, lp, kdim), BF16),
                   jax.ShapeDtypeStruct((HEADS, V_ROWS, lp), BF16),
                   jax.ShapeDtypeStruct((HEADS, lp, HEAD_DIM), BF16),
                   jax.ShapeDtypeStruct((HEADS, 1, lp), F32),
                   jax.ShapeDtypeStruct((HEADS, 1, ROW_TILE), F32),
                   jax.ShapeDtypeStruct((HEADS, 1, LANES), F32)],
        scratch_shapes=[pltpu.VMEM((ROW_TILE, D_MODEL), BF16),
                        pltpu.VMEM((ROW_TILE, 4 * HEADS), BF16),
                        pltpu.VMEM((4 * HEADS, ROW_TILE), BF16),
                        pltpu.VMEM((1, HEADS), F32)],
        compiler_params=_params("arbitrary", "arbitrary"),
        name="fox_in",
    )(hp, gain.reshape(1, D_MODEL), w, w, w, w, wf, b_f.reshape(1, HEADS),
      qnorm.reshape(1, HEAD_DIM), knorm.reshape(1, HEAD_DIM), tri, pq, pk)


def _fox_attn_kernel(safe_ref, kmax_ref, qqt_ref, qn_ref, kk_ref, vt_ref, sg_ref, o_ref,
                     m_ref, acc_ref, ring_ref):
    h = pl.program_id(0)
    i = pl.program_id(1)
    qs = i * ATT_Q
    n_groups = ATT_Q // ATT_GROUP
    n_steps = i * (ATT_Q // ATT_K)

    def scores(item):
        g, ks, nk, _ = item
        gs = slice(g * ATT_GROUP, (g + 1) * ATT_GROUP)
        return jnp.dot(kk_ref[pl.ds(ks, nk), :], qqt_ref[:, gs], preferred_element_type=F32)

    def step_items(j):
        return [(g, pl.multiple_of(j * ATT_K + sum(ATT_KSPLIT[:sub]), ATT_GROUP), nk, False)
                for sub, nk in enumerate(ATT_KSPLIT) for g in range(n_groups)]

    def sweep(fixed_shift):
        acc_ref[...] = jnp.zeros_like(acc_ref)
        if fixed_shift:
            m_ref[...] = qn_ref[...] * kmax_ref[h]
        else:
            m_ref[...] = jnp.full_like(m_ref, -jnp.inf)

        def update(item, s):
            g, ks, nk, masked = item
            gs = slice(g * ATT_GROUP, (g + 1) * ATT_GROUP)
            if masked:
                r = lax.broadcasted_iota(jnp.int32, s.shape, 0)
                c = lax.broadcasted_iota(jnp.int32, s.shape, 1) + g * ATT_GROUP
                s = jnp.where(r <= c, s, NEG_BIG)
            if fixed_shift:
                p = jnp.exp2(s - m_ref[:, gs])
                acc_ref[:HEAD_DIM, gs] += jnp.dot(vt_ref[:HEAD_DIM, pl.ds(ks, nk)], p.astype(BF16),
                                                  preferred_element_type=F32)
                acc_ref[HEAD_DIM:HEAD_DIM + 1, gs] += jnp.sum(p, axis=0, keepdims=True)
            else:
                vt = vt_ref[:, pl.ds(ks, nk)]
                m_prev = m_ref[:, gs]
                m_new = jnp.maximum(m_prev, jnp.max(s, axis=0, keepdims=True))
                alpha = jnp.exp2(m_prev - m_new)
                p = jnp.exp2((s - m_new).astype(BF16))
                m_ref[:, gs] = m_new
                acc_ref[:, gs] = alpha * acc_ref[:, gs] + jnp.dot(vt, p, preferred_element_type=F32)

        def run(items, first, ahead):
            pending = list(first) if first else [scores(item) for item in items[:ATT_AHEAD]]
            for n, item in enumerate(items):
                if n + ATT_AHEAD < len(items):
                    pending.append(scores(items[n + ATT_AHEAD]))
                else:
                    a = n + ATT_AHEAD - len(items)
                    ring_ref[a] = scores(ahead[a])
                update(item, pending.pop(0))

        run([(g, pl.multiple_of(qs, ATT_Q), (g + 1) * ATT_GROUP, True) for g in range(n_groups)],
            None, step_items(0)[:ATT_AHEAD])

        def body(j, carry):
            run(step_items(j), [ring_ref[a] for a in range(ATT_AHEAD)],
                step_items(jnp.minimum(j + 1, n_steps - 1))[:ATT_AHEAD])
            return carry

        lax.fori_loop(0, n_steps, body, 0)

    @pl.when(safe_ref[h] == 1)
    def _():
        sweep(True)

    @pl.when(safe_ref[h] != 1)
    def _():
        sweep(False)

    o = (acc_ref[:HEAD_DIM, :] / acc_ref[HEAD_DIM:HEAD_DIM + 1, :]).T * sg_ref[...].astype(F32)
    row = qs + lax.broadcasted_iota(jnp.int32, o.shape, 0)
    o_ref[...] = jnp.where(row >= META_PAD, o, 0.0).astype(BF16)


def _fox_attn(qqt, qn, qmax, kmax, kk, vt, sg):
    lp = kk.shape[1]
    kdim = kk.shape[2]
    qmax, kmax = jnp.max(qmax, axis=(1, 2)), jnp.max(kmax, axis=(1, 2))
    safe = (2.0 * qmax * kmax <= SAFE_SHIFT_RANGE).astype(jnp.int32)
    grid_spec = pltpu.PrefetchScalarGridSpec(
        num_scalar_prefetch=2,
        grid=(HEADS, lp // ATT_Q),
        in_specs=[pl.BlockSpec((None, kdim, ATT_Q), lambda h, i, *_: (h, 0, i)),
                  pl.BlockSpec((None, 1, ATT_Q), lambda h, i, *_: (h, 0, i)),
                  pl.BlockSpec((None, lp, kdim), lambda h, i, *_: (h, 0, 0)),
                  pl.BlockSpec((None, V_ROWS, lp), lambda h, i, *_: (h, 0, 0)),
                  pl.BlockSpec((None, ATT_Q, HEAD_DIM), lambda h, i, *_: (h, i, 0))],
        out_specs=pl.BlockSpec((ATT_Q, HEAD_DIM), lambda h, i, *_: (i, h)),
        scratch_shapes=[pltpu.VMEM((1, ATT_Q), F32), pltpu.VMEM((V_ROWS, ATT_Q), F32),
                        pltpu.VMEM((ATT_AHEAD, ATT_KSPLIT[0], ATT_GROUP), F32)])
    return pl.pallas_call(
        _fox_attn_kernel,
        grid_spec=grid_spec,
        out_shape=jax.ShapeDtypeStruct((lp, D_MODEL), BF16),
        compiler_params=_params("parallel", "arbitrary"),
        name="fox_attn",
    )(safe, kmax, qqt, qn, kk, vt, sg)


def kernel(x, meta_tokens, norm_g, ffn_w_gu, ffn_w_down, lb_logits, hg_w_in, hg_w_out, hg_onorm,
           fox_w_in, fox_b_f, fox_w_out, fox_qnorm, fox_knorm):
    assert x.shape[0] == 1 and x.shape[2] == D_MODEL
    seq = x.shape[1]
    real = META_PAD + N_META + seq
    lp = -(-real // ATT_Q) * ATT_Q
    assert seq % BLOCK == 0
    head = jnp.concatenate([jnp.zeros((META_PAD, D_MODEL), F32), meta_tokens.astype(F32)], axis=0)
    wgu, wd = _cast_ffn_weights(ffn_w_gu.reshape(2 * DEPTH, D_MODEL, 2 * FFN_DIM),
                                ffn_w_down.reshape(2 * DEPTH, FFN_DIM, D_MODEL))
    hp = x[0]
    for layer in range(DEPTH):
        hp = _ffn(hp, head, norm_g[layer, 0], wgu, wd, 2 * layer, -1 if layer == 0 else 0, lp)
        j = layer // 2
        if layer % 2 == 0:
            q, k, lf, v, sg = _hgrn_in(hp, norm_g[layer, 1], hg_w_in, j, lb_logits, layer)
            hp = _hgrn_rec(q, k, lf, v, sg, hg_onorm[j], hp, hg_w_out[j])
        else:
            qqt, kk, vt, sg, qn, qmax, kmax = _fox_in(hp, norm_g[layer, 1], fox_w_in, j, fox_b_f[j],
                                                      fox_qnorm[j], fox_knorm[j])
            og = _fox_attn(qqt, qn, qmax, kmax, kk, vt, sg)
            hp = _out_proj(hp, og, fox_w_out[j])
        last = layer == DEPTH - 1
        hp = _ffn(hp, head, norm_g[layer, 2], wgu, wd, 2 * layer + 1,
                  1 if last else 0, seq if last else lp)
    return hp[None]
```

```python
import functools

import numpy as np
import jax
import jax.numpy as jnp
from jax import lax
from jax.experimental import pallas as pl
from jax.experimental.pallas import tpu as pltpu

F32 = jnp.float32
BF16 = jnp.bfloat16

D_MODEL = 2048
DEPTH = 2
N_META = 16
BLOCK = 128
META_PAD = (-N_META) % BLOCK
RMS_EPS = 1e-6
HEADS = 16
HEAD_DIM = 128
FFN_DIM = 5504
NEG_BIG = -1e30
LOG2E = 1.4426950408889634

LANES = 128
ROW_TILE = 640
FFN_TILE = 512
FFN_PAD = -(-FFN_DIM // FFN_TILE) * FFN_TILE
PROJ_TILE = 512
CAST_ROWS = 256
ATT_Q = 1280
ATT_K = 1280
ATT_KSPLIT = (512, 768)
ATT_GROUP = 256
ATT_AHEAD = 2
V_ROWS = HEAD_DIM + 16
NORM_SLACK = 1.01
SAFE_SHIFT_RANGE = 80.0
VMEM_LIMIT = 56 * 1024 * 1024

_NT = (((1,), (1,)), ((), ()))


def _params(*sem):
    return pltpu.CompilerParams(dimension_semantics=sem, vmem_limit_bytes=VMEM_LIMIT)


def _sigmoid(x):
    return 1.0 / (1.0 + jnp.exp(-x))


def _rms_rows(x):
    return x * lax.rsqrt(jnp.mean(x * x, axis=-1, keepdims=True) + RMS_EPS)


def _split3(x):
    hi = x.astype(BF16)
    r1 = x - hi.astype(F32)
    mid = r1.astype(BF16)
    lo = (r1 - mid.astype(F32)).astype(BF16)
    return hi, mid, lo


def _cast_wgu_kernel(w_ref, o_ref):
    zeros = jnp.zeros((CAST_ROWS, FFN_PAD - FFN_DIM), BF16)
    o_ref[:, :FFN_DIM] = w_ref[:, :FFN_DIM].astype(BF16)
    o_ref[:, FFN_DIM:FFN_PAD] = zeros
    o_ref[:, FFN_PAD:FFN_PAD + FFN_DIM] = w_ref[:, FFN_DIM:].astype(BF16)
    o_ref[:, FFN_PAD + FFN_DIM:] = zeros


def _cast_kernel(w_ref, o_ref):
    o_ref[...] = w_ref[...].astype(BF16)


def _cast_transposed_kernel(wt_ref, tail_ref, o_ref, t_ref):
    o_ref[...] = wt_ref[...].T.astype(BF16)
    t_ref[...] = tail_ref[...].astype(BF16)


def _cast_ffn_weights(w_gu):
    n = w_gu.shape[0]
    return pl.pallas_call(
        _cast_wgu_kernel,
        grid=(n, D_MODEL // CAST_ROWS),
        in_specs=[pl.BlockSpec((None, CAST_ROWS, 2 * FFN_DIM), lambda a, r: (a, r, 0))],
        out_specs=pl.BlockSpec((None, CAST_ROWS, 2 * FFN_PAD), lambda a, r: (a, r, 0)),
        out_shape=jax.ShapeDtypeStruct((n, D_MODEL, 2 * FFN_PAD), BF16),
        compiler_params=_params("parallel", "parallel"),
        name="cast_wgu",
    )(w_gu)


def _cast_proj_weights(w, j, n_main):
    width = w.shape[2]
    main = jax.ShapeDtypeStruct((D_MODEL, n_main), BF16)
    if width == n_main:
        return pl.pallas_call(
            _cast_kernel, grid=(D_MODEL // CAST_ROWS,),
            in_specs=[pl.BlockSpec((None, CAST_ROWS, width), lambda r: (j, r, 0))],
            out_specs=pl.BlockSpec((CAST_ROWS, n_main), lambda r: (r, 0)),
            out_shape=main, compiler_params=_params("parallel"), name="cast_proj")(w)
    tail = width - n_main
    assert n_main % tail == 0
    wt = jnp.swapaxes(w, 1, 2)
    return pl.pallas_call(
        _cast_transposed_kernel, grid=(n_main // CAST_ROWS,),
        in_specs=[pl.BlockSpec((None, CAST_ROWS, D_MODEL), lambda r: (j, r, 0)),
                  pl.BlockSpec((None, tail, D_MODEL), lambda r: (j, n_main // tail, 0))],
        out_specs=[pl.BlockSpec((D_MODEL, CAST_ROWS), lambda r: (0, r)),
                   pl.BlockSpec((tail, D_MODEL), lambda r: (0, 0))],
        out_shape=[main, jax.ShapeDtypeStruct((tail, D_MODEL), BF16)],
        compiler_params=_params("arbitrary"), name="cast_proj_t")(wt, wt)


ROW_CHUNKS = ROW_TILE // BLOCK


def _ffn_kernel(*refs, shift, n_src):
    n_in = ROW_CHUNKS if shift else 1
    chunk_refs = refs[:n_in]
    head_ref, g_ref, wg_ref, wu_ref, wd_ref, o_ref, xn_ref = refs[n_in:]

    j = pl.program_id(1)

    def step(xn):
        a = jnp.dot(xn, wg_ref[...], preferred_element_type=F32)
        b = jnp.dot(xn, wu_ref[...], preferred_element_type=F32)
        act = (a * (0.5 * _sigmoid(a)) * b).astype(BF16)
        row = j * FFN_TILE + lax.broadcasted_iota(jnp.int32, (FFN_TILE, D_MODEL), 0)
        wd = jnp.where(row < FFN_DIM, wd_ref[...], 0.0).astype(BF16)
        return jnp.dot(act, wd, preferred_element_type=F32)

    @pl.when(j == 0)
    def _():
        parts = []
        for u, ref in enumerate(chunk_refs):
            idx = ROW_CHUNKS * pl.program_id(0) + u + shift
            blk = ref[...]
            if shift < 0:
                blk = jnp.where(idx < 0, head_ref[...], jnp.where(idx >= n_src, 0.0, blk))
            parts.append(blk)
        x = jnp.concatenate(parts, axis=0) if shift else parts[0]
        xn = (_rms_rows(x) * g_ref[...]).astype(BF16)
        xn_ref[...] = xn
        o_ref[...] = x + step(xn)

    @pl.when(j > 0)
    def _():
        o_ref[...] += step(xn_ref[...])


def _ffn(src, head, gain, wgu, wd, a, shift, out_rows):
    n_src = src.shape[0] // BLOCK
    nf = FFN_PAD // FFN_TILE
    chunk = lambda u: pl.BlockSpec(
        (BLOCK, D_MODEL), lambda i, j: (jnp.clip(ROW_CHUNKS * i + u + shift, 0, n_src - 1), 0))
    row_specs = ([chunk(u) for u in range(ROW_CHUNKS)] if shift else
                 [pl.BlockSpec((ROW_TILE, D_MODEL), lambda i, j: (i, 0))])
    return pl.pallas_call(
        functools.partial(_ffn_kernel, shift=shift, n_src=n_src),
        grid=(pl.cdiv(out_rows, ROW_TILE), nf),
        in_specs=row_specs + [
            pl.BlockSpec((BLOCK, D_MODEL), lambda i, j: (0, 0)),
            pl.BlockSpec((1, D_MODEL), lambda i, j: (0, 0)),
            pl.BlockSpec((None, D_MODEL, FFN_TILE), lambda i, j: (a, 0, j)),
            pl.BlockSpec((None, D_MODEL, FFN_TILE), lambda i, j: (a, 0, j + nf)),
            pl.BlockSpec((None, FFN_TILE, D_MODEL), lambda i, j: (a, j, 0)),
        ],
        out_specs=pl.BlockSpec((ROW_TILE, D_MODEL), lambda i, j: (i, 0)),
        out_shape=jax.ShapeDtypeStruct((out_rows, D_MODEL), F32),
        scratch_shapes=[pltpu.VMEM((ROW_TILE, D_MODEL), BF16)],
        compiler_params=_params("parallel", "arbitrary"),
        name="ffn",
    )(*([src] * len(row_specs)), head, gain.reshape(1, D_MODEL), wgu, wgu, wd)


def _out_proj_kernel(h_ref, o_ref, w_ref, out_ref):
    out_ref[...] = h_ref[...] + jnp.dot(o_ref[...], w_ref[...], preferred_element_type=F32)


def _out_proj(hp, og, w_out):
    lp = hp.shape[0]
    return pl.pallas_call(
        _out_proj_kernel,
        grid=(lp // ROW_TILE,),
        in_specs=[
            pl.BlockSpec((ROW_TILE, D_MODEL), lambda i: (i, 0)),
            pl.BlockSpec((ROW_TILE, D_MODEL), lambda i: (i, 0)),
            pl.BlockSpec((D_MODEL, D_MODEL), lambda i: (0, 0)),
        ],
        out_specs=pl.BlockSpec((ROW_TILE, D_MODEL), lambda i: (i, 0)),
        out_shape=jax.ShapeDtypeStruct((lp, D_MODEL), F32),
        compiler_params=_params("parallel"),
        name="out_proj",
    )(hp, og, w_out.astype(BF16))


def _hgrn_in_kernel(h_ref, g_ref, wq_ref, wf_ref, wi_ref, wg_ref, lbl_ref,
                    q_ref, k_ref, lf_ref, v_ref, sg_ref, xn_ref, *, layer):
    def project(xn):
        fp = jnp.dot(xn, wf_ref[...], preferred_element_type=F32)
        gp = jnp.dot(xn, wg_ref[...], preferred_element_type=F32)
        qp = jnp.dot(xn, wq_ref[...], preferred_element_type=F32)
        ip = jnp.dot(xn, wi_ref[...], preferred_element_type=F32)
        lg = lbl_ref[...]
        e = jnp.exp(lg - jnp.max(lg, axis=0, keepdims=True))
        lb = jnp.sum(e[:layer + 1], axis=0, keepdims=True) / jnp.sum(e, axis=0, keepdims=True)
        q = qp * _sigmoid(qp)
        fg = lb + (1.0 - lb) * _sigmoid(fp)
        k = 1.0 - fg
        lf = jnp.log(fg) * LOG2E
        sg = _sigmoid(gp)
        for hh in range(PROJ_TILE // HEAD_DIM):
            sl = slice(hh * HEAD_DIM, (hh + 1) * HEAD_DIM)
            q_ref[hh] = q[:, sl].astype(BF16)
            k_ref[hh] = k[:, sl].astype(BF16)
            lf_ref[hh] = lf[:, sl]
            v_ref[hh] = ip[:, sl].astype(BF16)
            sg_ref[hh] = sg[:, sl].astype(BF16)

    @pl.when(pl.program_id(1) == 0)
    def _():
        xn = (_rms_rows(h_ref[...]) * g_ref[...]).astype(BF16)
        xn_ref[...] = xn
        project(xn)

    @pl.when(pl.program_id(1) > 0)
    def _():
        project(xn_ref[...])


def _hgrn_in(hp, gain, w_in, j, lb_logits, layer):
    lp = hp.shape[0]
    nc = D_MODEL // PROJ_TILE
    hpt = PROJ_TILE // HEAD_DIM
    w = _cast_proj_weights(w_in, j, 4 * D_MODEL)
    wspec = lambda s: pl.BlockSpec((D_MODEL, PROJ_TILE), lambda i, c, s=s: (0, s * nc + c))
    hm_spec = pl.BlockSpec((hpt, ROW_TILE, HEAD_DIM), lambda i, c: (c, i, 0))
    hm = lambda dt: jax.ShapeDtypeStruct((HEADS, lp, HEAD_DIM), dt)
    return pl.pallas_call(
        functools.partial(_hgrn_in_kernel, layer=layer),
        grid=(lp // ROW_TILE, nc),
        in_specs=[
            pl.BlockSpec((ROW_TILE, D_MODEL), lambda i, c: (i, 0)),
            pl.BlockSpec((1, D_MODEL), lambda i, c: (0, 0)),
            wspec(0), wspec(1), wspec(2), wspec(3),
            pl.BlockSpec((DEPTH + 1, PROJ_TILE), lambda i, c: (0, c)),
        ],
        out_specs=[hm_spec] * 5,
        out_shape=[hm(BF16), hm(BF16), hm(F32), hm(BF16), hm(BF16)],
        scratch_shapes=[pltpu.VMEM((ROW_TILE, D_MODEL), BF16)],
        compiler_params=_params("parallel", "arbitrary"),
        name="hgrn_in",
    )(hp, gain.reshape(1, D_MODEL), w, w, w, w, lb_logits)


_ROW_LEVELS = (64, 32, 16, 8)
_MM_LEVELS = (4, 2)


def _hgrn_consts():
    t = np.arange(BLOCK)[:, None]
    j = np.arange(BLOCK)[None, :]
    blocks = [j <= t]
    for b in _MM_LEVELS:
        ref = 2 * b * (t // (2 * b)) + b - 1
        odd = (t // b) % 2 == 1
        blocks.append(np.where(odd, (j > ref) & (j <= t), (j > t) & (j <= ref)))
    g_all = np.concatenate(blocks, axis=0).astype(np.float32)
    masks = []
    for b in _ROW_LEVELS + _MM_LEVELS + (1,):
        masks.append((t // (2 * b) == j // (2 * b)) & ((t // b) % 2 == 1) & ((j // b) % 2 == 0))
    masks.append(t == j)
    return jnp.asarray(g_all, BF16), jnp.asarray(np.stack(masks).astype(np.float32))


def _hgrn_rec_kernel(q_ref, k_ref, lf_ref, v_ref, sg_ref, gmat_ref, mask_ref, on_ref, h_ref, w_ref,
                     out_ref, st_ref, og_ref):
    c = pl.program_id(0)

    @pl.when(c == 0)
    def _():
        st_ref[...] = jnp.zeros_like(st_ref)

    def boundary_exponent(b, size):
        parts = []
        for s in range(0, BLOCK, 2 * size):
            r = b[s + size - 1:s + size]
            parts += [r - b[s:s + size], b[s + size:s + 2 * size] - r]
        return jnp.concatenate(parts, axis=0)

    def step(project):
        hs = range(HEADS)
        lf = [lf_ref[h] for h in hs]
        q = [q_ref[h] for h in hs]
        k = [k_ref[h] for h in hs]
        x3 = [jnp.dot(gmat_ref[...], jnp.concatenate(_split3(l), axis=1),
                      preferred_element_type=F32) for l in lf]
        x = [t[:, :HEAD_DIM] + t[:, HEAD_DIM:2 * HEAD_DIM] + t[:, 2 * HEAD_DIM:] for t in x3]
        b = [t[:BLOCK] for t in x]
        b_last = [t[BLOCK - 1:BLOCK] for t in b]
        att = [jnp.zeros((BLOCK, BLOCK), F32) for _ in hs]
        n_row, n_mm = len(_ROW_LEVELS), len(_MM_LEVELS)
        n_lv = n_row + n_mm + 2
        width = D_MODEL // n_lv
        for lv in range(n_lv):
            for u in hs:
                if lv < n_row:
                    el = jnp.exp2(boundary_exponent(b[u], _ROW_LEVELS[lv])).astype(BF16)
                    qt, kt = q[u] * el, k[u] * el
                elif lv < n_row + n_mm:
                    m = lv - n_row + 1
                    el = jnp.exp2(x[u][m * BLOCK:(m + 1) * BLOCK]).astype(BF16)
                    qt, kt = q[u] * el, k[u] * el
                elif lv == n_row + n_mm:
                    qt, kt = q[u] * jnp.exp2(lf[u]).astype(BF16), k[u]
                else:
                    qt, kt = q[u], k[u]
                a = lax.dot_general(qt, kt, _NT, preferred_element_type=F32)
                att[u] = att[u] + mask_ref[lv] * a
            if project:
                sl = slice(lv * width, (lv + 1) * width)
                out_ref[:, sl] = h_ref[:, sl] + jnp.dot(og_ref[...], w_ref[:, sl],
                                                        preferred_element_type=F32)
        rows = pl.ds(pl.multiple_of((c % 2) * BLOCK, BLOCK), BLOCK)
        for h in hs:
            qb = q[h] * jnp.exp2(b[h]).astype(BF16)
            kd = k[h] * jnp.exp2(b_last[h] - b[h]).astype(BF16)
            vt = v_ref[h].astype(F32).T.astype(BF16)
            st = st_ref[h]
            lhs = jnp.concatenate([att[h].astype(BF16), qb], axis=1)
            rhs = jnp.concatenate([vt, st.astype(BF16)], axis=1)
            o = lax.dot_general(lhs, rhs, _NT, preferred_element_type=F32)
            st_ref[h] = st * jnp.exp2(b_last[h]) + jnp.dot(vt, kd, preferred_element_type=F32)
            og = _rms_rows(o) * on_ref[...] * sg_ref[h].astype(F32)
            og_ref[rows, h * HEAD_DIM:(h + 1) * HEAD_DIM] = og.astype(BF16)

    project_now = (c % 2 == 0) & (c > 0)

    @pl.when(project_now)
    def _():
        step(True)

    @pl.when(jnp.logical_not(project_now))
    def _():
        step(False)


def _hgrn_rec(q, k, lf, v, sg, onorm, hp, w_out):
    lp = q.shape[1]
    n = lp // BLOCK
    assert n % 2 == 0
    gmat, masks = _hgrn_consts()
    assert D_MODEL % masks.shape[0] == 0
    hm_spec = pl.BlockSpec((HEADS, BLOCK, HEAD_DIM), lambda c: (0, jnp.minimum(c, n - 1), 0))
    row_spec = pl.BlockSpec((2 * BLOCK, D_MODEL), lambda c: (jnp.maximum(c // 2 - 1, 0), 0))
    return pl.pallas_call(
        _hgrn_rec_kernel,
        grid=(n + 1,),
        in_specs=[hm_spec] * 5 + [
            pl.BlockSpec(gmat.shape, lambda c: (0, 0)),
            pl.BlockSpec(masks.shape, lambda c: (0, 0, 0)),
            pl.BlockSpec((1, HEAD_DIM), lambda c: (0, 0)),
            row_spec,
            pl.BlockSpec((D_MODEL, D_MODEL), lambda c: (0, 0)),
        ],
        out_specs=row_spec,
        out_shape=jax.ShapeDtypeStruct((lp, D_MODEL), F32),
        scratch_shapes=[pltpu.VMEM((HEADS, HEAD_DIM, HEAD_DIM), F32),
                        pltpu.VMEM((2 * BLOCK, D_MODEL), BF16)],
        compiler_params=_params("arbitrary"),
        name="hgrn_rec",
    )(q, k, lf, v, sg, gmat, masks, onorm.reshape(1, HEAD_DIM), hp, w_out.astype(BF16))


def _fox_consts():
    pq = np.zeros((HEADS, LANES, 4 * HEADS), np.float32)
    pk = np.zeros((HEADS, 4 * HEADS, LANES), np.float32)
    one, pad = 3 * HEADS, 3 * HEADS + 1
    for h in range(HEADS):
        for part in range(3):
            pq[h, part, part * HEADS + h] = 1.0
            pq[h, 3 + part, one] = 1.0
            pk[h, one, part] = 1.0
            pk[h, part * HEADS + h, 3 + part] = -1.0
        pk[h, pad, 3] = NEG_BIG
    tri = np.tril(np.ones((ROW_TILE, ROW_TILE), np.float32))
    return jnp.asarray(pq, BF16), jnp.asarray(pk, BF16), jnp.asarray(tri, BF16)


def _fox_in_kernel(h_ref, g_ref, wq_ref, wk_ref, wv_ref, wg_ref, wf_ref, bf_ref, gq_ref, gk_ref,
                   tri_ref, pq_ref, pk_ref,
                   qqt_ref, kk_ref, vt_ref, sg_ref, qn_ref, qmax_ref, kmax_ref,
                   xn_ref, src_ref, srct_ref, carry_ref):
    i = pl.program_id(0)
    c = pl.program_id(1)
    hpt = PROJ_TILE // HEAD_DIM

    def project(xn):
        gp = jnp.dot(xn, wg_ref[...], preferred_element_type=F32)
        qp = jnp.dot(xn, wq_ref[...], preferred_element_type=F32)
        kp = jnp.dot(xn, wk_ref[...], preferred_element_type=F32)
        vp = jnp.dot(xn, wv_ref[...], preferred_element_type=F32)
        sg = _sigmoid(gp)
        scale = HEAD_DIM ** -0.5 * LOG2E
        ones = jnp.ones((HEAD_DIM, LANES), BF16)
        for hh in range(hpt):
            sl = slice(hh * HEAD_DIM, (hh + 1) * HEAD_DIM)
            head = hpt * c + hh
            q = (_rms_rows(qp[:, sl]) * (gq_ref[...] * scale)).astype(BF16)
            k = (_rms_rows(kp[:, sl]) * gk_ref[...]).astype(BF16)
            qt = q.astype(F32).T
            qqt_ref[hh, :HEAD_DIM, :] = qt.astype(BF16)
            qqt_ref[hh, HEAD_DIM:, :] = jnp.dot(
                pq_ref[hh], srct_ref[...], preferred_element_type=F32).astype(BF16)
            kk_ref[hh, :, :HEAD_DIM] = k
            kk_ref[hh, :, HEAD_DIM:] = jnp.dot(
                src_ref[...], pk_ref[hh], preferred_element_type=F32).astype(BF16)
            vt_ref[hh, :HEAD_DIM, :] = vp[:, sl].T.astype(BF16)
            vt_ref[hh, HEAD_DIM:, :] = jnp.ones((V_ROWS - HEAD_DIM, ROW_TILE), BF16)
            sg_ref[hh] = sg[:, sl].astype(BF16)
            qn = jnp.sqrt(jnp.sum(qt * qt, axis=0, keepdims=True))
            qn_ref[hh] = qn
            kf = k.astype(F32)
            kn2 = jnp.dot((kf * kf).astype(BF16), ones, preferred_element_type=F32)
            kn = jnp.sqrt(jnp.max(kn2, axis=0, keepdims=True)) * NORM_SLACK
            qmax_ref[head] = jnp.maximum(qmax_ref[head], qn)
            kmax_ref[head] = jnp.maximum(kmax_ref[head], kn)

    @pl.when((i == 0) & (c == 0))
    def _():
        carry_ref[...] = jnp.zeros_like(carry_ref)
        qmax_ref[...] = jnp.zeros_like(qmax_ref)
        kmax_ref[...] = jnp.zeros_like(kmax_ref)

    @pl.when(c == 0)
    def _():
        xn = (_rms_rows(h_ref[...]) * g_ref[...]).astype(BF16)
        xn_ref[...] = xn
        z = lax.dot_general(xn, wf_ref[...], _NT, preferred_element_type=F32) + bf_ref[...]
        row = i * ROW_TILE + lax.broadcasted_iota(jnp.int32, (ROW_TILE, HEADS), 0)
        lf = jnp.minimum(z, 0.0) - jnp.log(1.0 + jnp.exp(-jnp.abs(z)))
        lf = jnp.where(row >= META_PAD, lf, 0.0)
        cum = carry_ref[...]
        for part in _split3(lf):
            cum = cum + jnp.dot(tri_ref[...], part, preferred_element_type=F32)
        carry_ref[...] = cum[ROW_TILE - 1:ROW_TILE]
        col = lax.broadcasted_iota(jnp.int32, (ROW_TILE, HEADS), 1)
        extra = jnp.where(col == 0, 1.0, jnp.where((col == 1) & (row < META_PAD), 1.0, 0.0))
        src = jnp.concatenate([p.astype(F32) for p in _split3(cum * LOG2E)] + [extra], axis=1)
        src_ref[...] = src.astype(BF16)
        srct_ref[...] = jnp.concatenate([src, jnp.zeros_like(src)], axis=1).T[:4 * HEADS].astype(BF16)
        project(xn)

    @pl.when(c > 0)
    def _():
        project(xn_ref[...])


def _fox_in(hp, gain, w_in, j, b_f, qnorm, knorm):
    lp = hp.shape[0]
    nc = D_MODEL // PROJ_TILE
    hpt = PROJ_TILE // HEAD_DIM
    kdim = HEAD_DIM + LANES
    w, wf = _cast_proj_weights(w_in, j, 4 * D_MODEL)
    pq, pk, tri = _fox_consts()
    wspec = lambda s: pl.BlockSpec((D_MODEL, PROJ_TILE), lambda i, c, s=s: (0, s * nc + c))
    vec = lambda n: pl.BlockSpec((1, n), lambda i, c: (0, 0))
    return pl.pallas_call(
        _fox_in_kernel,
        grid=(lp // ROW_TILE, nc),
        in_specs=[
            pl.BlockSpec((ROW_TILE, D_MODEL), lambda i, c: (i, 0)),
            vec(D_MODEL),
            wspec(0), wspec(1), wspec(2), wspec(3),
            pl.BlockSpec((HEADS, D_MODEL), lambda i, c: (0, 0)),
            vec(HEADS), vec(HEAD_DIM), vec(HEAD_DIM),
            pl.BlockSpec((ROW_TILE, ROW_TILE), lambda i, c: (0, 0)),
            pl.BlockSpec((hpt, LANES, 4 * HEADS), lambda i, c: (c, 0, 0)),
            pl.BlockSpec((hpt, 4 * HEADS, LANES), lambda i, c: (c, 0, 0)),
        ],
        out_specs=[pl.BlockSpec((hpt, kdim, ROW_TILE), lambda i, c: (c, 0, i)),
                   pl.BlockSpec((hpt, ROW_TILE, kdim), lambda i, c: (c, i, 0)),
                   pl.BlockSpec((hpt, V_ROWS, ROW_TILE), lambda i, c: (c, 0, i)),
                   pl.BlockSpec((hpt, ROW_TILE, HEAD_DIM), lambda i, c: (c, i, 0)),
                   pl.BlockSpec((hpt, 1, ROW_TILE), lambda i, c: (c, 0, i)),
                   pl.BlockSpec((HEADS, 1, ROW_TILE), lambda i, c: (0, 0, 0)),
                   pl.BlockSpec((HEADS, 1, LANES), lambda i, c: (0, 0, 0))],
        out_shape=[jax.ShapeDtypeStruct((HEADS, kdim, lp), BF16),
                   jax.ShapeDtypeStruct((HEADS, lp, kdim), BF16),
                   jax.ShapeDtypeStruct((HEADS, V_ROWS, lp), BF16),
                   jax.ShapeDtypeStruct((HEADS, lp, HEAD_DIM), BF16),
                   jax.ShapeDtypeStruct((HEADS, 1, lp), F32),
                   jax.ShapeDtypeStruct((HEADS, 1, ROW_TILE), F32),
                   jax.ShapeDtypeStruct((HEADS, 1, LANES), F32)],
        scratch_shapes=[pltpu.VMEM((ROW_TILE, D_MODEL), BF16),
                        pltpu.VMEM((ROW_TILE, 4 * HEADS), BF16),
                        pltpu.VMEM((4 * HEADS, ROW_TILE), BF16),
                        pltpu.VMEM((1, HEADS), F32)],
        compiler_params=_params("arbitrary", "arbitrary"),
        name="fox_in",
    )(hp, gain.reshape(1, D_MODEL), w, w, w, w, wf, b_f.reshape(1, HEADS),
      qnorm.reshape(1, HEAD_DIM), knorm.reshape(1, HEAD_DIM), tri, pq, pk)


def _fox_attn_kernel(safe_ref, kmax_ref, qqt_ref, qn_ref, kk_ref, vt_ref, sg_ref, o_ref,
                     m_ref, acc_ref, ring_ref):
    h = pl.program_id(0)
    i = pl.program_id(1)
    qs = i * ATT_Q
    n_groups = ATT_Q // ATT_GROUP
    n_steps = i * (ATT_Q // ATT_K)

    def scores(item):
        g, ks, nk, _ = item
        gs = slice(g * ATT_GROUP, (g + 1) * ATT_GROUP)
        return jnp.dot(kk_ref[pl.ds(ks, nk), :], qqt_ref[:, gs], preferred_element_type=F32)

    def step_items(j):
        return [(g, pl.multiple_of(j * ATT_K + sum(ATT_KSPLIT[:sub]), ATT_GROUP), nk, False)
                for sub, nk in enumerate(ATT_KSPLIT) for g in range(n_groups)]

    def sweep(fixed_shift):
        acc_ref[...] = jnp.zeros_like(acc_ref)
        if fixed_shift:
            m_ref[...] = qn_ref[...] * kmax_ref[h]
        else:
            m_ref[...] = jnp.full_like(m_ref, -jnp.inf)

        def update(item, s):
            g, ks, nk, masked = item
            gs = slice(g * ATT_GROUP, (g + 1) * ATT_GROUP)
            if masked:
                r = lax.broadcasted_iota(jnp.int32, s.shape, 0)
                c = lax.broadcasted_iota(jnp.int32, s.shape, 1) + g * ATT_GROUP
                s = jnp.where(r <= c, s, NEG_BIG)
            if fixed_shift:
                p = jnp.exp2(s - m_ref[:, gs])
                acc_ref[:HEAD_DIM, gs] += jnp.dot(vt_ref[:HEAD_DIM, pl.ds(ks, nk)], p.astype(BF16),
                                                  preferred_element_type=F32)
                acc_ref[HEAD_DIM:HEAD_DIM + 1, gs] += jnp.sum(p, axis=0, keepdims=True)
            else:
                vt = vt_ref[:, pl.ds(ks, nk)]
                m_prev = m_ref[:, gs]
                m_new = jnp.maximum(m_prev, jnp.max(s, axis=0, keepdims=True))
                alpha = jnp.exp2(m_prev - m_new)
                p = jnp.exp2((s - m_new).astype(BF16))
                m_ref[:, gs] = m_new
                acc_ref[:, gs] = alpha * acc_ref[:, gs] + jnp.dot(vt, p, preferred_element_type=F32)

        def run(items, first, ahead):
            pending = list(first) if first else [scores(item) for item in items[:ATT_AHEAD]]
            for n, item in enumerate(items):
                if n + ATT_AHEAD < len(items):
                    pending.append(scores(items[n + ATT_AHEAD]))
                else:
                    a = n + ATT_AHEAD - len(items)
                    ring_ref[a] = scores(ahead[a])
                update(item, pending.pop(0))

        run([(g, pl.multiple_of(qs, ATT_Q), (g + 1) * ATT_GROUP, True) for g in range(n_groups)],
            None, step_items(0)[:ATT_AHEAD])

        def body(j, carry):
            run(step_items(j), [ring_ref[a] for a in range(ATT_AHEAD)],
                step_items(jnp.minimum(j + 1, n_steps - 1))[:ATT_AHEAD])
            return carry

        lax.fori_loop(0, n_steps, body, 0)

    @pl.when(safe_ref[h] == 1)
    def _():
        sweep(True)

    @pl.when(safe_ref[h] != 1)
    def _():
        sweep(False)

    o = (acc_ref[:HEAD_DIM, :] / acc_ref[HEAD_DIM:HEAD_DIM + 1, :]).T * sg_ref[...].astype(F32)
    row = qs + lax.broadcasted_iota(jnp.int32, o.shape, 0)
    o_ref[...] = jnp.where(row >= META_PAD, o, 0.0).astype(BF16)


def _fox_attn(qqt, qn, qmax, kmax, kk, vt, sg):
    lp = kk.shape[1]
    kdim = kk.shape[2]
    qmax, kmax = jnp.max(qmax, axis=(1, 2)), jnp.max(kmax, axis=(1, 2))
    safe = (2.0 * qmax * kmax <= SAFE_SHIFT_RANGE).astype(jnp.int32)
    grid_spec = pltpu.PrefetchScalarGridSpec(
        num_scalar_prefetch=2,
        grid=(HEADS, lp // ATT_Q),
        in_specs=[pl.BlockSpec((None, kdim, ATT_Q), lambda h, i, *_: (h, 0, i)),
                  pl.BlockSpec((None, 1, ATT_Q), lambda h, i, *_: (h, 0, i)),
                  pl.BlockSpec((None, lp, kdim), lambda h, i, *_: (h, 0, 0)),
                  pl.BlockSpec((None, V_ROWS, lp), lambda h, i, *_: (h, 0, 0)),
                  pl.BlockSpec((None, ATT_Q, HEAD_DIM), lambda h, i, *_: (h, i, 0))],
        out_specs=pl.BlockSpec((ATT_Q, HEAD_DIM), lambda h, i, *_: (i, h)),
        scratch_shapes=[pltpu.VMEM((1, ATT_Q), F32), pltpu.VMEM((V_ROWS, ATT_Q), F32),
                        pltpu.VMEM((ATT_AHEAD, ATT_KSPLIT[0], ATT_GROUP), F32)])
    return pl.pallas_call(
        _fox_attn_kernel,
        grid_spec=grid_spec,
        out_shape=jax.ShapeDtypeStruct((lp, D_MODEL), BF16),
        compiler_params=_params("parallel", "arbitrary"),
        name="fox_attn",
    )(safe, kmax, qqt, qn, kk, vt, sg)


def kernel(x, meta_tokens, norm_g, ffn_w_gu, ffn_w_down, lb_logits, hg_w_in, hg_w_out, hg_onorm,
           fox_w_in, fox_b_f, fox_w_out, fox_qnorm, fox_knorm):
    assert x.shape[0] == 1 and x.shape[2] == D_MODEL
    seq = x.shape[1]
    real = META_PAD + N_META + seq
    lp = -(-real // ATT_Q) * ATT_Q
    assert seq % BLOCK == 0
    head = jnp.concatenate([jnp.zeros((META_PAD, D_MODEL), F32), meta_tokens.astype(F32)], axis=0)
    wgu = _cast_ffn_weights(ffn_w_gu.reshape(2 * DEPTH, D_MODEL, 2 * FFN_DIM))
    wd = ffn_w_down.reshape(2 * DEPTH, FFN_DIM, D_MODEL)
    hp = x[0]
    for layer in range(DEPTH):
        hp = _ffn(hp, head, norm_g[layer, 0], wgu, wd, 2 * layer, -1 if layer == 0 else 0, lp)
        j = layer // 2
        if layer % 2 == 0:
            q, k, lf, v, sg = _hgrn_in(hp, norm_g[layer, 1], hg_w_in, j, lb_logits, layer)
            hp = _hgrn_rec(q, k, lf, v, sg, hg_onorm[j], hp, hg_w_out[j])
        else:
            qqt, kk, vt, sg, qn, qmax, kmax = _fox_in(hp, norm_g[layer, 1], fox_w_in, j, fox_b_f[j],
                                                      fox_qnorm[j], fox_knorm[j])
            og = _fox_attn(qqt, qn, qmax, kmax, kk, vt, sg)
            hp = _out_proj(hp, og, fox_w_out[j])
        last = layer == DEPTH - 1
        hp = _ffn(hp, head, norm_g[layer, 2], wgu, wd, 2 * layer + 1,
                  1 if last else 0, seq if last else lp)
    return hp[None]
```

```python
import functools

import numpy as np
import jax
import jax.numpy as jnp
from jax import lax
from jax.experimental import pallas as pl
from jax.experimental.pallas import tpu as pltpu

F32 = jnp.float32
BF16 = jnp.bfloat16

D_MODEL = 2048
DEPTH = 2
N_META = 16
BLOCK = 128
META_PAD = (-N_META) % BLOCK
RMS_EPS = 1e-6
HEADS = 16
HEAD_DIM = 128
FFN_DIM = 5504
NEG_BIG = -1e30
LOG2E = 1.4426950408889634

LANES = 128
ROW_TILE = 640
FFN_TILE = 512
FFN_PAD = -(-FFN_DIM // FFN_TILE) * FFN_TILE
PROJ_TILE = 512
CAST_ROWS = 256
ATT_Q = 1280
ATT_K = 1280
ATT_KSPLIT = (512, 768)
ATT_GROUP = 256
ATT_AHEAD = 2
V_ROWS = HEAD_DIM + 16
NORM_SLACK = 1.01
SAFE_SHIFT_RANGE = 80.0
VMEM_LIMIT = 56 * 1024 * 1024

_NT = (((1,), (1,)), ((), ()))


def _params(*sem):
    return pltpu.CompilerParams(dimension_semantics=sem, vmem_limit_bytes=VMEM_LIMIT)


def _sigmoid(x):
    return 1.0 / (1.0 + jnp.exp(-x))


def _rms_rows(x):
    return x * lax.rsqrt(jnp.mean(x * x, axis=-1, keepdims=True) + RMS_EPS)


def _split3(x):
    hi = x.astype(BF16)
    r1 = x - hi.astype(F32)
    mid = r1.astype(BF16)
    lo = (r1 - mid.astype(F32)).astype(BF16)
    return hi, mid, lo


def _cast_wgu_kernel(w_ref, o_ref):
    zeros = jnp.zeros((CAST_ROWS, FFN_PAD - FFN_DIM), BF16)
    o_ref[:, :FFN_DIM] = w_ref[:, :FFN_DIM].astype(BF16)
    o_ref[:, FFN_DIM:FFN_PAD] = zeros
    o_ref[:, FFN_PAD:FFN_PAD + FFN_DIM] = w_ref[:, FFN_DIM:].astype(BF16)
    o_ref[:, FFN_PAD + FFN_DIM:] = zeros


def _cast_wd_kernel(w_ref, o_ref):
    row = pl.program_id(1) * FFN_TILE + lax.broadcasted_iota(jnp.int32, (FFN_TILE, D_MODEL), 0)
    o_ref[...] = jnp.where(row < FFN_DIM, w_ref[...], 0.0).astype(BF16)


def _cast_kernel(w_ref, o_ref):
    o_ref[...] = w_ref[...].astype(BF16)


def _cast_transposed_kernel(wt_ref, tail_ref, o_ref, t_ref):
    o_ref[...] = wt_ref[...].T.astype(BF16)
    t_ref[...] = tail_ref[...].astype(BF16)


def _cast_ffn_weights(w_gu, w_down):
    n = w_gu.shape[0]
    wgu = pl.pallas_call(
        _cast_wgu_kernel,
        grid=(n, D_MODEL // CAST_ROWS),
        in_specs=[pl.BlockSpec((None, CAST_ROWS, 2 * FFN_DIM), lambda a, r: (a, r, 0))],
        out_specs=pl.BlockSpec((None, CAST_ROWS, 2 * FFN_PAD), lambda a, r: (a, r, 0)),
        out_shape=jax.ShapeDtypeStruct((n, D_MODEL, 2 * FFN_PAD), BF16),
        compiler_params=_params("parallel", "parallel"),
        name="cast_wgu",
    )(w_gu)
    wd = pl.pallas_call(
        _cast_wd_kernel,
        grid=(n, FFN_PAD // FFN_TILE),
        in_specs=[pl.BlockSpec((None, FFN_TILE, D_MODEL), lambda a, r: (a, r, 0))],
        out_specs=pl.BlockSpec((None, FFN_TILE, D_MODEL), lambda a, r: (a, r, 0)),
        out_shape=jax.ShapeDtypeStruct((n, FFN_PAD, D_MODEL), BF16),
        compiler_params=_params("parallel", "parallel"),
        name="cast_wd",
    )(w_down)
    return wgu, wd


def _cast_proj_weights(w, j, n_main):
    width = w.shape[2]
    main = jax.ShapeDtypeStruct((D_MODEL, n_main), BF16)
    if width == n_main:
        return pl.pallas_call(
            _cast_kernel, grid=(D_MODEL // CAST_ROWS,),
            in_specs=[pl.BlockSpec((None, CAST_ROWS, width), lambda r: (j, r, 0))],
            out_specs=pl.BlockSpec((CAST_ROWS, n_main), lambda r: (r, 0)),
            out_shape=main, compiler_params=_params("parallel"), name="cast_proj")(w)
    tail = width - n_main
    assert n_main % tail == 0
    wt = jnp.swapaxes(w, 1, 2)
    return pl.pallas_call(
        _cast_transposed_kernel, grid=(n_main // CAST_ROWS,),
        in_specs=[pl.BlockSpec((None, CAST_ROWS, D_MODEL), lambda r: (j, r, 0)),
                  pl.BlockSpec((None, tail, D_MODEL), lambda r: (j, n_main // tail, 0))],
        out_specs=[pl.BlockSpec((D_MODEL, CAST_ROWS), lambda r: (0, r)),
                   pl.BlockSpec((tail, D_MODEL), lambda r: (0, 0))],
        out_shape=[main, jax.ShapeDtypeStruct((tail, D_MODEL), BF16)],
        compiler_params=_params("arbitrary"), name="cast_proj_t")(wt, wt)


ROW_CHUNKS = ROW_TILE // BLOCK


def _ffn_kernel(*refs, shift, n_src):
    n_in = ROW_CHUNKS if shift else 1
    chunk_refs = refs[:n_in]
    head_ref, g_ref, wg_ref, wu_ref, wd_ref, o_ref, xn_ref = refs[n_in:]

    j = pl.program_id(1)

    def step(xn):
        a = jnp.dot(xn, wg_ref[...], preferred_element_type=F32)
        b = jnp.dot(xn, wu_ref[...], preferred_element_type=F32)
        act = (a * (0.5 * _sigmoid(a)) * b).astype(BF16)
        return jnp.dot(act, wd_ref[...], preferred_element_type=F32)

    @pl.when(j == 0)
    def _():
        parts = []
        for u, ref in enumerate(chunk_refs):
            idx = ROW_CHUNKS * pl.program_id(0) + u + shift
            blk = ref[...]
            if shift < 0:
                blk = jnp.where(idx < 0, head_ref[...], jnp.where(idx >= n_src, 0.0, blk))
            parts.append(blk)
        x = jnp.concatenate(parts, axis=0) if shift else parts[0]
        xn = (_rms_rows(x) * g_ref[...]).astype(BF16)
        xn_ref[...] = xn
        o_ref[...] = x + step(xn)

    @pl.when(j > 0)
    def _():
        o_ref[...] += step(xn_ref[...])


def _ffn(src, head, gain, wgu, wd, a, shift, out_rows):
    n_src = src.shape[0] // BLOCK
    nf = FFN_PAD // FFN_TILE
    chunk = lambda u: pl.BlockSpec(
        (BLOCK, D_MODEL), lambda i, j: (jnp.clip(ROW_CHUNKS * i + u + shift, 0, n_src - 1), 0))
    row_specs = ([chunk(u) for u in range(ROW_CHUNKS)] if shift else
                 [pl.BlockSpec((ROW_TILE, D_MODEL), lambda i, j: (i, 0))])
    return pl.pallas_call(
        functools.partial(_ffn_kernel, shift=shift, n_src=n_src),
        grid=(pl.cdiv(out_rows, ROW_TILE), nf),
        in_specs=row_specs + [
            pl.BlockSpec((BLOCK, D_MODEL), lambda i, j: (0, 0)),
            pl.BlockSpec((1, D_MODEL), lambda i, j: (0, 0)),
            pl.BlockSpec((None, D_MODEL, FFN_TILE), lambda i, j: (a, 0, j)),
            pl.BlockSpec((None, D_MODEL, FFN_TILE), lambda i, j: (a, 0, j + nf)),
            pl.BlockSpec((None, FFN_TILE, D_MODEL), lambda i, j: (a, j, 0)),
        ],
        out_specs=pl.BlockSpec((ROW_TILE, D_MODEL), lambda i, j: (i, 0)),
        out_shape=jax.ShapeDtypeStruct((out_rows, D_MODEL), F32),
        scratch_shapes=[pltpu.VMEM((ROW_TILE, D_MODEL), BF16)],
        compiler_params=_params("parallel", "arbitrary"),
        name="ffn",
    )(*([src] * len(row_specs)), head, gain.reshape(1, D_MODEL), wgu, wgu, wd)


def _out_proj_kernel(h_ref, o_ref, w_ref, out_ref):
    out_ref[...] = h_ref[...] + jnp.dot(o_ref[...], w_ref[...], preferred_element_type=F32)


def _out_proj(hp, og, w_out):
    lp = hp.shape[0]
    return pl.pallas_call(
        _out_proj_kernel,
        grid=(lp // ROW_TILE,),
        in_specs=[
            pl.BlockSpec((ROW_TILE, D_MODEL), lambda i: (i, 0)),
            pl.BlockSpec((ROW_TILE, D_MODEL), lambda i: (i, 0)),
            pl.BlockSpec((D_MODEL, D_MODEL), lambda i: (0, 0)),
        ],
        out_specs=pl.BlockSpec((ROW_TILE, D_MODEL), lambda i: (i, 0)),
        out_shape=jax.ShapeDtypeStruct((lp, D_MODEL), F32),
        compiler_params=_params("parallel"),
        name="out_proj",
    )(hp, og, w_out.astype(BF16))


def _hgrn_in_kernel(h_ref, g_ref, wq_ref, wf_ref, wi_ref, wg_ref, lbl_ref,
                    q_ref, k_ref, lf_ref, v_ref, sg_ref, xn_ref, *, layer):
    def project(xn):
        qp = jnp.dot(xn, wq_ref[...], preferred_element_type=F32)
        fp = jnp.dot(xn, wf_ref[...], preferred_element_type=F32)
        ip = jnp.dot(xn, wi_ref[...], preferred_element_type=F32)
        gp = jnp.dot(xn, wg_ref[...], preferred_element_type=F32)
        lg = lbl_ref[...]
        e = jnp.exp(lg - jnp.max(lg, axis=0, keepdims=True))
        lb = jnp.sum(e[:layer + 1], axis=0, keepdims=True) / jnp.sum(e, axis=0, keepdims=True)
        q = qp * _sigmoid(qp)
        fg = lb + (1.0 - lb) * _sigmoid(fp)
        k = 1.0 - fg
        lf = jnp.log(fg) * LOG2E
        sg = _sigmoid(gp)
        for hh in range(PROJ_TILE // HEAD_DIM):
            sl = slice(hh * HEAD_DIM, (hh + 1) * HEAD_DIM)
            q_ref[hh] = q[:, sl].astype(BF16)
            k_ref[hh] = k[:, sl].astype(BF16)
            lf_ref[hh] = lf[:, sl]
            v_ref[hh] = ip[:, sl].astype(BF16)
            sg_ref[hh] = sg[:, sl].astype(BF16)

    @pl.when(pl.program_id(1) == 0)
    def _():
        xn = (_rms_rows(h_ref[...]) * g_ref[...]).astype(BF16)
        xn_ref[...] = xn
        project(xn)

    @pl.when(pl.program_id(1) > 0)
    def _():
        project(xn_ref[...])


def _hgrn_in(hp, gain, w_in, j, lb_logits, layer):
    lp = hp.shape[0]
    nc = D_MODEL // PROJ_TILE
    hpt = PROJ_TILE // HEAD_DIM
    w = _cast_proj_weights(w_in, j, 4 * D_MODEL)
    wspec = lambda s: pl.BlockSpec((D_MODEL, PROJ_TILE), lambda i, c, s=s: (0, s * nc + c))
    hm_spec = pl.BlockSpec((hpt, ROW_TILE, HEAD_DIM), lambda i, c: (c, i, 0))
    hm = lambda dt: jax.ShapeDtypeStruct((HEADS, lp, HEAD_DIM), dt)
    return pl.pallas_call(
        functools.partial(_hgrn_in_kernel, layer=layer),
        grid=(lp // ROW_TILE, nc),
        in_specs=[
            pl.BlockSpec((ROW_TILE, D_MODEL), lambda i, c: (i, 0)),
            pl.BlockSpec((1, D_MODEL), lambda i, c: (0, 0)),
            wspec(0), wspec(1), wspec(2), wspec(3),
            pl.BlockSpec((DEPTH + 1, PROJ_TILE), lambda i, c: (0, c)),
        ],
        out_specs=[hm_spec] * 5,
        out_shape=[hm(BF16), hm(BF16), hm(F32), hm(BF16), hm(BF16)],
        scratch_shapes=[pltpu.VMEM((ROW_TILE, D_MODEL), BF16)],
        compiler_params=_params("parallel", "arbitrary"),
        name="hgrn_in",
    )(hp, gain.reshape(1, D_MODEL), w, w, w, w, lb_logits)


_ROW_LEVELS = (64, 32, 16, 8)
_MM_LEVELS = (4, 2)
FAST_BLOCK = 32
FAST_RANGE = 90.0
OUT_SLICES = 8


def _hgrn_consts():
    t = np.arange(BLOCK)[:, None]
    j = np.arange(BLOCK)[None, :]
    blocks = [j <= t]
    for b in _MM_LEVELS:
        ref = 2 * b * (t // (2 * b)) + b - 1
        odd = (t // b) % 2 == 1
        blocks.append(np.where(odd, (j > ref) & (j <= t), (j > t) & (j <= ref)))
    g_all = np.concatenate(blocks, axis=0).astype(np.float32)
    masks = []
    for b in _ROW_LEVELS + _MM_LEVELS + (1,):
        masks.append((t // (2 * b) == j // (2 * b)) & ((t // b) % 2 == 1) & ((j // b) % 2 == 0))
    masks.append(t == j)
    masks.append((t // FAST_BLOCK == j // FAST_BLOCK) & (j <= t))
    return jnp.asarray(g_all, BF16), jnp.asarray(np.stack(masks).astype(np.float32))


def _hgrn_rec_kernel(q_ref, k_ref, lf_ref, v_ref, sg_ref, gmat_ref, mask_ref, on_ref, h_ref, w_ref,
                     out_ref, st_ref, og_ref):
    c = pl.program_id(0)

    @pl.when(c == 0)
    def _():
        st_ref[...] = jnp.zeros_like(st_ref)

    def boundary_exponent(b, size):
        parts = []
        for s in range(0, BLOCK, 2 * size):
            r = b[s + size - 1:s + size]
            parts += [r - b[s:s + size], b[s + size:s + 2 * size] - r]
        return jnp.concatenate(parts, axis=0)

    def middle_exponents(b):
        dq, dk = [], []
        for s in range(0, BLOCK, FAST_BLOCK):
            m = b[s + FAST_BLOCK // 2 - 1:s + FAST_BLOCK // 2]
            dq.append(b[s:s + FAST_BLOCK] - m)
            dk.append(m - b[s:s + FAST_BLOCK])
        return jnp.concatenate(dq, axis=0), jnp.concatenate(dk, axis=0)

    n_row, n_mm = len(_ROW_LEVELS), len(_MM_LEVELS)
    n_masks = n_row + n_mm + 2
    n_coarse = _ROW_LEVELS.index(FAST_BLOCK) + 1
    width = D_MODEL // OUT_SLICES

    def step(project, fast):
        hs = range(HEADS)
        n_lv = n_coarse + 1 if fast else n_masks
        g_rows = BLOCK if fast else BLOCK * (1 + n_mm)
        lf = [lf_ref[h] for h in hs]
        q = [q_ref[h] for h in hs]
        k = [k_ref[h] for h in hs]
        x3 = [jnp.dot(gmat_ref[:g_rows, :], jnp.concatenate(_split3(l), axis=1),
                      preferred_element_type=F32) for l in lf]
        x = [t[:, :HEAD_DIM] + t[:, HEAD_DIM:2 * HEAD_DIM] + t[:, 2 * HEAD_DIM:] for t in x3]
        b = [t[:BLOCK] for t in x]
        b_last = [t[BLOCK - 1:BLOCK] for t in b]
        att = [jnp.zeros((BLOCK, BLOCK), F32) for _ in hs]
        for lv in range(n_lv):
            for u in hs:
                if fast and lv == n_coarse:
                    dq, dk = middle_exponents(b[u])
                    a = lax.dot_general(q[u] * jnp.exp2(dq).astype(BF16),
                                        k[u] * jnp.exp2(dk).astype(BF16), _NT,
                                        preferred_element_type=F32)
                    att[u] = jnp.where(mask_ref[n_masks] > 0.0, a, att[u])
                    continue
                if lv < n_row:
                    el = jnp.exp2(boundary_exponent(b[u], _ROW_LEVELS[lv])).astype(BF16)
                    qt, kt = q[u] * el, k[u] * el
                elif lv < n_row + n_mm:
                    m = lv - n_row + 1
                    el = jnp.exp2(x[u][m * BLOCK:(m + 1) * BLOCK]).astype(BF16)
                    qt, kt = q[u] * el, k[u] * el
                elif lv == n_row + n_mm:
                    qt, kt = q[u] * jnp.exp2(lf[u]).astype(BF16), k[u]
                else:
                    qt, kt = q[u], k[u]
                a = lax.dot_general(qt, kt, _NT, preferred_element_type=F32)
                att[u] = att[u] + mask_ref[lv] * a
            if project:
                for piece in range(lv * OUT_SLICES // n_lv, (lv + 1) * OUT_SLICES // n_lv):
                    sl = slice(piece * width, (piece + 1) * width)
                    out_ref[:, sl] = h_ref[:, sl] + jnp.dot(og_ref[...], w_ref[:, sl],
                                                            preferred_element_type=F32)
        rows = pl.ds(pl.multiple_of((c % 2) * BLOCK, BLOCK), BLOCK)
        for h in hs:
            qb = q[h] * jnp.exp2(b[h]).astype(BF16)
            kd = k[h] * jnp.exp2(b_last[h] - b[h]).astype(BF16)
            vt = v_ref[h].astype(F32).T.astype(BF16)
            st = st_ref[h]
            lhs = jnp.concatenate([att[h].astype(BF16), qb], axis=1)
            rhs = jnp.concatenate([vt, st.astype(BF16)], axis=1)
            o = lax.dot_general(lhs, rhs, _NT, preferred_element_type=F32)
            st_ref[h] = st * jnp.exp2(b_last[h]) + jnp.dot(vt, kd, preferred_element_type=F32)
            og = _rms_rows(o) * on_ref[...] * sg_ref[h].astype(F32)
            og_ref[rows, h * HEAD_DIM:(h + 1) * HEAD_DIM] = og.astype(BF16)

    block_sums = [jnp.sum(lf_ref[h].reshape(BLOCK // FAST_BLOCK, FAST_BLOCK, HEAD_DIM), axis=1)
                  for h in range(HEADS)]
    fast_now = jnp.min(jnp.concatenate(block_sums, axis=0)) >= -FAST_RANGE
    project_now = (c % 2 == 0) & (c > 0)
    for project in (True, False):
        for fast in (True, False):
            pl.when((project_now if project else jnp.logical_not(project_now))
                    & (fast_now if fast else jnp.logical_not(fast_now)))(
                functools.partial(step, project, fast))


def _hgrn_rec(q, k, lf, v, sg, onorm, hp, w_out):
    lp = q.shape[1]
    n = lp // BLOCK
    assert n % 2 == 0
    gmat, masks = _hgrn_consts()
    hm_spec = pl.BlockSpec((HEADS, BLOCK, HEAD_DIM), lambda c: (0, jnp.minimum(c, n - 1), 0))
    row_spec = pl.BlockSpec((2 * BLOCK, D_MODEL), lambda c: (jnp.maximum(c // 2 - 1, 0), 0))
    return pl.pallas_call(
        _hgrn_rec_kernel,
        grid=(n + 1,),
        in_specs=[hm_spec] * 5 + [
            pl.BlockSpec(gmat.shape, lambda c: (0, 0)),
            pl.BlockSpec(masks.shape, lambda c: (0, 0, 0)),
            pl.BlockSpec((1, HEAD_DIM), lambda c: (0, 0)),
            row_spec,
            pl.BlockSpec((D_MODEL, D_MODEL), lambda c: (0, 0)),
        ],
        out_specs=row_spec,
        out_shape=jax.ShapeDtypeStruct((lp, D_MODEL), F32),
        scratch_shapes=[pltpu.VMEM((HEADS, HEAD_DIM, HEAD_DIM), F32),
                        pltpu.VMEM((2 * BLOCK, D_MODEL), BF16)],
        compiler_params=_params("arbitrary"),
        name="hgrn_rec",
    )(q, k, lf, v, sg, gmat, masks, onorm.reshape(1, HEAD_DIM), hp, w_out.astype(BF16))


def _fox_consts():
    pq = np.zeros((HEADS, LANES, 4 * HEADS), np.float32)
    pk = np.zeros((HEADS, 4 * HEADS, LANES), np.float32)
    one, pad = 3 * HEADS, 3 * HEADS + 1
    for h in range(HEADS):
        for part in range(3):
            pq[h, part, part * HEADS + h] = 1.0
            pq[h, 3 + part, one] = 1.0
            pk[h, one, part] = 1.0
            pk[h, part * HEADS + h, 3 + part] = -1.0
        pk[h, pad, 3] = NEG_BIG
    tri = np.tril(np.ones((ROW_TILE, ROW_TILE), np.float32))
    return jnp.asarray(pq, BF16), jnp.asarray(pk, BF16), jnp.asarray(tri, BF16)


def _fox_in_kernel(h_ref, g_ref, wq_ref, wk_ref, wv_ref, wg_ref, wf_ref, bf_ref, gq_ref, gk_ref,
                   tri_ref, pq_ref, pk_ref,
                   qqt_ref, kk_ref, vt_ref, sg_ref, qn_ref, qmax_ref, kmax_ref,
                   xn_ref, src_ref, srct_ref, carry_ref):
    i = pl.program_id(0)
    c = pl.program_id(1)
    hpt = PROJ_TILE // HEAD_DIM

    def project(xn):
        qp = jnp.dot(xn, wq_ref[...], preferred_element_type=F32)
        kp = jnp.dot(xn, wk_ref[...], preferred_element_type=F32)
        vp = jnp.dot(xn, wv_ref[...], preferred_element_type=F32)
        gp = jnp.dot(xn, wg_ref[...], preferred_element_type=F32)
        sg = _sigmoid(gp)
        scale = HEAD_DIM ** -0.5 * LOG2E
        ones = jnp.ones((HEAD_DIM, LANES), BF16)
        for hh in range(hpt):
            sl = slice(hh * HEAD_DIM, (hh + 1) * HEAD_DIM)
            head = hpt * c + hh
            q = (_rms_rows(qp[:, sl]) * (gq_ref[...] * scale)).astype(BF16)
            k = (_rms_rows(kp[:, sl]) * gk_ref[...]).astype(BF16)
            qt = q.astype(F32).T
            qqt_ref[hh, :HEAD_DIM, :] = qt.astype(BF16)
            qqt_ref[hh, HEAD_DIM:, :] = jnp.dot(
                pq_ref[hh], srct_ref[...], preferred_element_type=F32).astype(BF16)
            kk_ref[hh, :, :HEAD_DIM] = k
            kk_ref[hh, :, HEAD_DIM:] = jnp.dot(
                src_ref[...], pk_ref[hh], preferred_element_type=F32).astype(BF16)
            vt_ref[hh, :HEAD_DIM, :] = vp[:, sl].T.astype(BF16)
            vt_ref[hh, HEAD_DIM:, :] = jnp.ones((V_ROWS - HEAD_DIM, ROW_TILE), BF16)
            sg_ref[hh] = sg[:, sl].astype(BF16)
            qn = jnp.sqrt(jnp.sum(qt * qt, axis=0, keepdims=True))
            qn_ref[hh] = qn
            kf = k.astype(F32)
            kn2 = jnp.dot((kf * kf).astype(BF16), ones, preferred_element_type=F32)
            kn = jnp.sqrt(jnp.max(kn2, axis=0, keepdims=True)) * NORM_SLACK
            qmax_ref[head] = jnp.maximum(qmax_ref[head], qn)
            kmax_ref[head] = jnp.maximum(kmax_ref[head], kn)

    @pl.when((i == 0) & (c == 0))
    def _():
        carry_ref[...] = jnp.zeros_like(carry_ref)
        qmax_ref[...] = jnp.zeros_like(qmax_ref)
        kmax_ref[...] = jnp.zeros_like(kmax_ref)

    @pl.when(c == 0)
    def _():
        xn = (_rms_rows(h_ref[...]) * g_ref[...]).astype(BF16)
        xn_ref[...] = xn
        z = lax.dot_general(xn, wf_ref[...], _NT, preferred_element_type=F32) + bf_ref[...]
        row = i * ROW_TILE + lax.broadcasted_iota(jnp.int32, (ROW_TILE, HEADS), 0)
        lf = jnp.minimum(z, 0.0) - jnp.log(1.0 + jnp.exp(-jnp.abs(z)))
        lf = jnp.where(row >= META_PAD, lf, 0.0)
        cum = carry_ref[...]
        for part in _split3(lf):
            cum = cum + jnp.dot(tri_ref[...], part, preferred_element_type=F32)
        carry_ref[...] = cum[ROW_TILE - 1:ROW_TILE]
        col = lax.broadcasted_iota(jnp.int32, (ROW_TILE, HEADS), 1)
        extra = jnp.where(col == 0, 1.0, jnp.where((col == 1) & (row < META_PAD), 1.0, 0.0))
        src = jnp.concatenate([p.astype(F32) for p in _split3(cum * LOG2E)] + [extra], axis=1)
        src_ref[...] = src.astype(BF16)
        srct_ref[...] = jnp.concatenate([src, jnp.zeros_like(src)], axis=1).T[:4 * HEADS].astype(BF16)
        project(xn)

    @pl.when(c > 0)
    def _():
        project(xn_ref[...])


def _fox_in(hp, gain, w_in, j, b_f, qnorm, knorm):
    lp = hp.shape[0]
    nc = D_MODEL // PROJ_TILE
    hpt = PROJ_TILE // HEAD_DIM
    kdim = HEAD_DIM + LANES
    w, wf = _cast_proj_weights(w_in, j, 4 * D_MODEL)
    pq, pk, tri = _fox_consts()
    wspec = lambda s: pl.BlockSpec((D_MODEL, PROJ_TILE), lambda i, c, s=s: (0, s * nc + c))
    vec = lambda n: pl.BlockSpec((1, n), lambda i, c: (0, 0))
    return pl.pallas_call(
        _fox_in_kernel,
        grid=(lp // ROW_TILE, nc),
        in_specs=[
            pl.BlockSpec((ROW_TILE, D_MODEL), lambda i, c: (i, 0)),
            vec(D_MODEL),
            wspec(0), wspec(1), wspec(2), wspec(3),
            pl.BlockSpec((HEADS, D_MODEL), lambda i, c: (0, 0)),
            vec(HEADS), vec(HEAD_DIM), vec(HEAD_DIM),
            pl.BlockSpec((ROW_TILE, ROW_TILE), lambda i, c: (0, 0)),
            pl.BlockSpec((hpt, LANES, 4 * HEADS), lambda i, c: (c, 0, 0)),
            pl.BlockSpec((hpt, 4 * HEADS, LANES), lambda i, c: (c, 0, 0)),
        ],
        out_specs=[pl.BlockSpec((hpt, kdim, ROW_TILE), lambda i, c: (c, 0, i)),
                   pl.BlockSpec((hpt, ROW_TILE, kdim), lambda i, c: (c, i, 0)),
                   pl.BlockSpec((hpt, V_ROWS, ROW_TILE), lambda i, c: (c, 0, i)),
                   pl.BlockSpec((hpt, ROW_TILE, HEAD_DIM), lambda i, c: (c, i, 0)),
                   pl.BlockSpec((hpt, 1, ROW_TILE), lambda i, c: (c, 0, i)),
                   pl.BlockSpec((HEADS, 1, ROW_TILE), lambda i, c: (0, 0, 0)),
                   pl.BlockSpec((HEADS, 1, LANES), lambda i, c: (0, 0, 0))],
        out_shape=[jax.ShapeDtypeStruct((HEADS, kdim, lp), BF16),
                   jax.ShapeDtypeStruct((HEADS, lp, kdim), BF16),
                   jax.ShapeDtypeStruct((HEADS, V_ROWS, lp), BF16),
                   jax.ShapeDtypeStruct((HEADS, lp, HEAD_DIM), BF16),
                   jax.ShapeDtypeStruct((HEADS, 1, lp), F32),
                   jax.ShapeDtypeStruct((HEADS, 1, ROW_TILE), F32),
                   jax.ShapeDtypeStruct((HEADS, 1, LANES), F32)],
        scratch_shapes=[pltpu.VMEM((ROW_TILE, D_MODEL), BF16),
                        pltpu.VMEM((ROW_TILE, 4 * HEADS), BF16),
                        pltpu.VMEM((4 * HEADS, ROW_TILE), BF16),
                        pltpu.VMEM((1, HEADS), F32)],
        compiler_params=_params("arbitrary", "arbitrary"),
        name="fox_in",
    )(hp, gain.reshape(1, D_MODEL), w, w, w, w, wf, b_f.reshape(1, HEADS),
      qnorm.reshape(1, HEAD_DIM), knorm.reshape(1, HEAD_DIM), tri, pq, pk)


def _fox_attn_kernel(safe_ref, kmax_ref, qqt_ref, qn_ref, kk_ref, vt_ref, sg_ref, o_ref,
                     m_ref, acc_ref, ring_ref):
    h = pl.program_id(0)
    i = pl.program_id(1)
    qs = i * ATT_Q
    n_groups = ATT_Q // ATT_GROUP
    n_steps = i * (ATT_Q // ATT_K)

    def scores(item):
        g, ks, nk, _ = item
        gs = slice(g * ATT_GROUP, (g + 1) * ATT_GROUP)
        return jnp.dot(kk_ref[pl.ds(ks, nk), :], qqt_ref[:, gs], preferred_element_type=F32)

    def step_items(j):
        return [(g, pl.multiple_of(j * ATT_K + sum(ATT_KSPLIT[:sub]), ATT_GROUP), nk, False)
                for sub, nk in enumerate(ATT_KSPLIT) for g in range(n_groups)]

    def sweep(fixed_shift):
        acc_ref[...] = jnp.zeros_like(acc_ref)
        if fixed_shift:
            m_ref[...] = qn_ref[...] * kmax_ref[h]
        else:
            m_ref[...] = jnp.full_like(m_ref, -jnp.inf)

        def update(item, s):
            g, ks, nk, masked = item
            gs = slice(g * ATT_GROUP, (g + 1) * ATT_GROUP)
            if masked:
                r = lax.broadcasted_iota(jnp.int32, s.shape, 0)
                c = lax.broadcasted_iota(jnp.int32, s.shape, 1) + g * ATT_GROUP
                s = jnp.where(r <= c, s, NEG_BIG)
            if fixed_shift:
                p = jnp.exp2(s - m_ref[:, gs])
                acc_ref[:HEAD_DIM, gs] += jnp.dot(vt_ref[:HEAD_DIM, pl.ds(ks, nk)], p.astype(BF16),
                                                  preferred_element_type=F32)
                acc_ref[HEAD_DIM:HEAD_DIM + 1, gs] += jnp.sum(p, axis=0, keepdims=True)
            else:
                vt = vt_ref[:, pl.ds(ks, nk)]
                m_prev = m_ref[:, gs]
                m_new = jnp.maximum(m_prev, jnp.max(s, axis=0, keepdims=True))
                alpha = jnp.exp2(m_prev - m_new)
                p = jnp.exp2((s - m_new).astype(BF16))
                m_ref[:, gs] = m_new
                acc_ref[:, gs] = alpha * acc_ref[:, gs] + jnp.dot(vt, p, preferred_element_type=F32)

        def run(items, first, ahead):
            pending = list(first) if first else [scores(item) for item in items[:ATT_AHEAD]]
            for n, item in enumerate(items):
                if n + ATT_AHEAD < len(items):
                    pending.append(scores(items[n + ATT_AHEAD]))
                else:
                    a = n + ATT_AHEAD - len(items)
                    ring_ref[a] = scores(ahead[a])
                update(item, pending.pop(0))

        run([(g, pl.multiple_of(qs, ATT_Q), (g + 1) * ATT_GROUP, True) for g in range(n_groups)],
            None, step_items(0)[:ATT_AHEAD])

        def body(j, carry):
            run(step_items(j), [ring_ref[a] for a in range(ATT_AHEAD)],
                step_items(jnp.minimum(j + 1, n_steps - 1))[:ATT_AHEAD])
            return carry

        lax.fori_loop(0, n_steps, body, 0)

    @pl.when(safe_ref[h] == 1)
    def _():
        sweep(True)

    @pl.when(safe_ref[h] != 1)
    def _():
        sweep(False)

    o = (acc_ref[:HEAD_DIM, :] / acc_ref[HEAD_DIM:HEAD_DIM + 1, :]).T * sg_ref[...].astype(F32)
    row = qs + lax.broadcasted_iota(jnp.int32, o.shape, 0)
    o_ref[...] = jnp.where(row >= META_PAD, o, 0.0).astype(BF16)


def _fox_attn(qqt, qn, qmax, kmax, kk, vt, sg):
    lp = kk.shape[1]
    kdim = kk.shape[2]
    qmax, kmax = jnp.max(qmax, axis=(1, 2)), jnp.max(kmax, axis=(1, 2))
    safe = (2.0 * qmax * kmax <= SAFE_SHIFT_RANGE).astype(jnp.int32)
    grid_spec = pltpu.PrefetchScalarGridSpec(
        num_scalar_prefetch=2,
        grid=(HEADS, lp // ATT_Q),
        in_specs=[pl.BlockSpec((None, kdim, ATT_Q), lambda h, i, *_: (h, 0, i)),
                  pl.BlockSpec((None, 1, ATT_Q), lambda h, i, *_: (h, 0, i)),
                  pl.BlockSpec((None, lp, kdim), lambda h, i, *_: (h, 0, 0)),
                  pl.BlockSpec((None, V_ROWS, lp), lambda h, i, *_: (h, 0, 0)),
                  pl.BlockSpec((None, ATT_Q, HEAD_DIM), lambda h, i, *_: (h, i, 0))],
        out_specs=pl.BlockSpec((ATT_Q, HEAD_DIM), lambda h, i, *_: (i, h)),
        scratch_shapes=[pltpu.VMEM((1, ATT_Q), F32), pltpu.VMEM((V_ROWS, ATT_Q), F32),
                        pltpu.VMEM((ATT_AHEAD, ATT_KSPLIT[0], ATT_GROUP), F32)])
    return pl.pallas_call(
        _fox_attn_kernel,
        grid_spec=grid_spec,
        out_shape=jax.ShapeDtypeStruct((lp, D_MODEL), BF16),
        compiler_params=_params("parallel", "arbitrary"),
        name="fox_attn",
    )(safe, kmax, qqt, qn, kk, vt, sg)


def kernel(x, meta_tokens, norm_g, ffn_w_gu, ffn_w_down, lb_logits, hg_w_in, hg_w_out, hg_onorm,
           fox_w_in, fox_b_f, fox_w_out, fox_qnorm, fox_knorm):
    assert x.shape[0] == 1 and x.shape[2] == D_MODEL
    seq = x.shape[1]
    real = META_PAD + N_META + seq
    lp = -(-real // ATT_Q) * ATT_Q
    assert seq % BLOCK == 0
    head = jnp.concatenate([jnp.zeros((META_PAD, D_MODEL), F32), meta_tokens.astype(F32)], axis=0)
    wgu, wd = _cast_ffn_weights(ffn_w_gu.reshape(2 * DEPTH, D_MODEL, 2 * FFN_DIM),
                                ffn_w_down.reshape(2 * DEPTH, FFN_DIM, D_MODEL))
    hp = x[0]
    for layer in range(DEPTH):
        hp = _ffn(hp, head, norm_g[layer, 0], wgu, wd, 2 * layer, -1 if layer == 0 else 0, lp)
        j = layer // 2
        if layer % 2 == 0:
            q, k, lf, v, sg = _hgrn_in(hp, norm_g[layer, 1], hg_w_in, j, lb_logits, layer)
            hp = _hgrn_rec(q, k, lf, v, sg, hg_onorm[j], hp, hg_w_out[j])
        else:
            qqt, kk, vt, sg, qn, qmax, kmax = _fox_in(hp, norm_g[layer, 1], fox_w_in, j, fox_b_f[j],
                                                      fox_qnorm[j], fox_knorm[j])
            og = _fox_attn(qqt, qn, qmax, kmax, kk, vt, sg)
            hp = _out_proj(hp, og, fox_w_out[j])
        last = layer == DEPTH - 1
        hp = _ffn(hp, head, norm_g[layer, 2], wgu, wd, 2 * layer + 1,
                  1 if last else 0, seq if last else lp)
    return hp[None]
```

```python
import functools

import numpy as np
import jax
import jax.numpy as jnp
from jax import lax
from jax.experimental import pallas as pl
from jax.experimental.pallas import tpu as pltpu

F32 = jnp.float32
BF16 = jnp.bfloat16

D_MODEL = 2048
DEPTH = 2
N_META = 16
BLOCK = 128
META_PAD = (-N_META) % BLOCK
RMS_EPS = 1e-6
HEADS = 16
HEAD_DIM = 128
FFN_DIM = 5504
NEG_BIG = -1e30
LOG2E = 1.4426950408889634

LANES = 128
ROW_TILE = 640
FFN_TILE = 512
FFN_PAD = -(-FFN_DIM // FFN_TILE) * FFN_TILE
PROJ_TILE = 512
CAST_ROWS = 256
SIDE_ROWS = 64
ATT_Q = 1280
ATT_K = 1280
ATT_KSPLIT = (512, 768)
ATT_GROUP = 256
ATT_AHEAD = 2
V_ROWS = HEAD_DIM + 16
NORM_SLACK = 1.01
SAFE_SHIFT_RANGE = 80.0
VMEM_LIMIT = 56 * 1024 * 1024

_NT = (((1,), (1,)), ((), ()))


def _params(*sem):
    return pltpu.CompilerParams(dimension_semantics=sem, vmem_limit_bytes=VMEM_LIMIT)


def _sigmoid(x):
    return 1.0 / (1.0 + jnp.exp(-x))


def _rms_rows(x):
    return x * lax.rsqrt(jnp.mean(x * x, axis=-1, keepdims=True) + RMS_EPS)


def _split3(x):
    hi = x.astype(BF16)
    r1 = x - hi.astype(F32)
    mid = r1.astype(BF16)
    lo = (r1 - mid.astype(F32)).astype(BF16)
    return hi, mid, lo


def _cast_wgu_kernel(w_ref, o_ref):
    zeros = jnp.zeros((o_ref.shape[0], FFN_PAD - FFN_DIM), BF16)
    o_ref[:, :FFN_DIM] = w_ref[:, :FFN_DIM].astype(BF16)
    o_ref[:, FFN_DIM:FFN_PAD] = zeros
    o_ref[:, FFN_PAD:FFN_PAD + FFN_DIM] = w_ref[:, FFN_DIM:].astype(BF16)
    o_ref[:, FFN_PAD + FFN_DIM:] = zeros


def _cast_wd_kernel(w_ref, o_ref):
    row = pl.program_id(1) * FFN_TILE + lax.broadcasted_iota(jnp.int32, (FFN_TILE, D_MODEL), 0)
    o_ref[...] = jnp.where(row < FFN_DIM, w_ref[...], 0.0).astype(BF16)


def _cast_kernel(w_ref, o_ref):
    o_ref[...] = w_ref[...].astype(BF16)


def _cast_transposed_kernel(wt_ref, tail_ref, o_ref, t_ref):
    o_ref[...] = wt_ref[...].T.astype(BF16)
    t_ref[...] = tail_ref[...].astype(BF16)


def _cast_ffn_weights(w_gu, w_down, n_gu):
    n = w_gu.shape[0]
    wgu = pl.pallas_call(
        _cast_wgu_kernel,
        grid=(n_gu, D_MODEL // CAST_ROWS),
        in_specs=[pl.BlockSpec((None, CAST_ROWS, 2 * FFN_DIM), lambda a, r: (a, r, 0))],
        out_specs=pl.BlockSpec((None, CAST_ROWS, 2 * FFN_PAD), lambda a, r: (a, r, 0)),
        out_shape=jax.ShapeDtypeStruct((n_gu, D_MODEL, 2 * FFN_PAD), BF16),
        compiler_params=_params("parallel", "parallel"),
        name="cast_wgu",
    )(w_gu)
    wd = pl.pallas_call(
        _cast_wd_kernel,
        grid=(n, FFN_PAD // FFN_TILE),
        in_specs=[pl.BlockSpec((None, FFN_TILE, D_MODEL), lambda a, r: (a, r, 0))],
        out_specs=pl.BlockSpec((None, FFN_TILE, D_MODEL), lambda a, r: (a, r, 0)),
        out_shape=jax.ShapeDtypeStruct((n, FFN_PAD, D_MODEL), BF16),
        compiler_params=_params("parallel", "parallel"),
        name="cast_wd",
    )(w_down)
    return wgu, wd


def _cast_proj_weights(w, j, n_main):
    width = w.shape[2]
    main = jax.ShapeDtypeStruct((D_MODEL, n_main), BF16)
    if width == n_main:
        return pl.pallas_call(
            _cast_kernel, grid=(D_MODEL // CAST_ROWS,),
            in_specs=[pl.BlockSpec((None, CAST_ROWS, width), lambda r: (j, r, 0))],
            out_specs=pl.BlockSpec((CAST_ROWS, n_main), lambda r: (r, 0)),
            out_shape=main, compiler_params=_params("parallel"), name="cast_proj")(w)
    tail = width - n_main
    assert n_main % tail == 0
    wt = jnp.swapaxes(w, 1, 2)
    return pl.pallas_call(
        _cast_transposed_kernel, grid=(n_main // CAST_ROWS,),
        in_specs=[pl.BlockSpec((None, CAST_ROWS, D_MODEL), lambda r: (j, r, 0)),
                  pl.BlockSpec((None, tail, D_MODEL), lambda r: (j, n_main // tail, 0))],
        out_specs=[pl.BlockSpec((D_MODEL, CAST_ROWS), lambda r: (0, r)),
                   pl.BlockSpec((tail, D_MODEL), lambda r: (0, 0))],
        out_shape=[main, jax.ShapeDtypeStruct((tail, D_MODEL), BF16)],
        compiler_params=_params("arbitrary"), name="cast_proj_t")(wt, wt)


ROW_CHUNKS = ROW_TILE // BLOCK


def _ffn_kernel(*refs, shift, n_src):
    n_in = ROW_CHUNKS if shift else 1
    chunk_refs = refs[:n_in]
    head_ref, g_ref, wg_ref, wu_ref, wd_ref, o_ref, xn_ref = refs[n_in:]

    j = pl.program_id(1)

    def step(xn):
        a = jnp.dot(xn, wg_ref[...], preferred_element_type=F32)
        b = jnp.dot(xn, wu_ref[...], preferred_element_type=F32)
        act = (a * (0.5 * _sigmoid(a)) * b).astype(BF16)
        return jnp.dot(act, wd_ref[...], preferred_element_type=F32)

    @pl.when(j == 0)
    def _():
        parts = []
        for u, ref in enumerate(chunk_refs):
            idx = ROW_CHUNKS * pl.program_id(0) + u + shift
            blk = ref[...]
            if shift < 0:
                blk = jnp.where(idx < 0, head_ref[...], jnp.where(idx >= n_src, 0.0, blk))
            parts.append(blk)
        x = jnp.concatenate(parts, axis=0) if shift else parts[0]
        xn = (_rms_rows(x) * g_ref[...]).astype(BF16)
        xn_ref[...] = xn
        o_ref[...] = x + step(xn)

    @pl.when(j > 0)
    def _():
        o_ref[...] += step(xn_ref[...])


def _ffn(src, head, gain, wgu, ag, wd, a, shift, out_rows):
    n_src = src.shape[0] // BLOCK
    nf = FFN_PAD // FFN_TILE
    chunk = lambda u: pl.BlockSpec(
        (BLOCK, D_MODEL), lambda i, j: (jnp.clip(ROW_CHUNKS * i + u + shift, 0, n_src - 1), 0))
    row_specs = ([chunk(u) for u in range(ROW_CHUNKS)] if shift else
                 [pl.BlockSpec((ROW_TILE, D_MODEL), lambda i, j: (i, 0))])
    return pl.pallas_call(
        functools.partial(_ffn_kernel, shift=shift, n_src=n_src),
        grid=(pl.cdiv(out_rows, ROW_TILE), nf),
        in_specs=row_specs + [
            pl.BlockSpec((BLOCK, D_MODEL), lambda i, j: (0, 0)),
            pl.BlockSpec((1, D_MODEL), lambda i, j: (0, 0)),
            pl.BlockSpec((None, D_MODEL, FFN_TILE), lambda i, j: (ag, 0, j)),
            pl.BlockSpec((None, D_MODEL, FFN_TILE), lambda i, j: (ag, 0, j + nf)),
            pl.BlockSpec((None, FFN_TILE, D_MODEL), lambda i, j: (a, j, 0)),
        ],
        out_specs=pl.BlockSpec((ROW_TILE, D_MODEL), lambda i, j: (i, 0)),
        out_shape=jax.ShapeDtypeStruct((out_rows, D_MODEL), F32),
        scratch_shapes=[pltpu.VMEM((ROW_TILE, D_MODEL), BF16)],
        compiler_params=_params("parallel", "arbitrary"),
        name="ffn",
    )(*([src] * len(row_specs)), head, gain.reshape(1, D_MODEL), wgu, wgu, wd)


def _out_proj_kernel(h_ref, o_ref, w_ref, out_ref):
    out_ref[...] = h_ref[...] + jnp.dot(o_ref[...], w_ref[...], preferred_element_type=F32)


def _out_proj(hp, og, w_out):
    lp = hp.shape[0]
    return pl.pallas_call(
        _out_proj_kernel,
        grid=(lp // ROW_TILE,),
        in_specs=[
            pl.BlockSpec((ROW_TILE, D_MODEL), lambda i: (i, 0)),
            pl.BlockSpec((ROW_TILE, D_MODEL), lambda i: (i, 0)),
            pl.BlockSpec((D_MODEL, D_MODEL), lambda i: (0, 0)),
        ],
        out_specs=pl.BlockSpec((ROW_TILE, D_MODEL), lambda i: (i, 0)),
        out_shape=jax.ShapeDtypeStruct((lp, D_MODEL), F32),
        compiler_params=_params("parallel"),
        name="out_proj",
    )(hp, og, w_out.astype(BF16))


def _hgrn_in_kernel(h_ref, g_ref, wq_ref, wf_ref, wi_ref, wg_ref, lbl_ref, side_ref,
                    q_ref, k_ref, lf_ref, v_ref, sg_ref, side_out_ref, xn_ref, *, layer):
    def project(xn):
        _cast_wgu_kernel(side_ref, side_out_ref)
        qp = jnp.dot(xn, wq_ref[...], preferred_element_type=F32)
        fp = jnp.dot(xn, wf_ref[...], preferred_element_type=F32)
        ip = jnp.dot(xn, wi_ref[...], preferred_element_type=F32)
        gp = jnp.dot(xn, wg_ref[...], preferred_element_type=F32)
        lg = lbl_ref[...]
        e = jnp.exp(lg - jnp.max(lg, axis=0, keepdims=True))
        lb = jnp.sum(e[:layer + 1], axis=0, keepdims=True) / jnp.sum(e, axis=0, keepdims=True)
        q = qp * _sigmoid(qp)
        fg = lb + (1.0 - lb) * _sigmoid(fp)
        k = 1.0 - fg
        lf = jnp.log(fg) * LOG2E
        sg = _sigmoid(gp)
        for hh in range(PROJ_TILE // HEAD_DIM):
            sl = slice(hh * HEAD_DIM, (hh + 1) * HEAD_DIM)
            q_ref[hh] = q[:, sl].astype(BF16)
            k_ref[hh] = k[:, sl].astype(BF16)
            lf_ref[hh] = lf[:, sl]
            v_ref[hh] = ip[:, sl].astype(BF16)
            sg_ref[hh] = sg[:, sl].astype(BF16)

    @pl.when(pl.program_id(1) == 0)
    def _():
        xn = (_rms_rows(h_ref[...]) * g_ref[...]).astype(BF16)
        xn_ref[...] = xn
        project(xn)

    @pl.when(pl.program_id(1) > 0)
    def _():
        project(xn_ref[...])


def _hgrn_in(hp, gain, w_in, j, lb_logits, layer, w_gu, n_done):
    lp = hp.shape[0]
    nc = D_MODEL // PROJ_TILE
    hpt = PROJ_TILE // HEAD_DIM
    per_w = D_MODEL // SIDE_ROWS
    n_side = (w_gu.shape[0] - n_done) * per_w
    assert n_side <= (lp // ROW_TILE) * nc
    side_block = lambda i, c: jnp.minimum(i * nc + c, n_side - 1)
    w = _cast_proj_weights(w_in, j, 4 * D_MODEL)
    wspec = lambda s: pl.BlockSpec((D_MODEL, PROJ_TILE), lambda i, c, s=s: (0, s * nc + c))
    hm_spec = pl.BlockSpec((hpt, ROW_TILE, HEAD_DIM), lambda i, c: (c, i, 0))
    hm = lambda dt: jax.ShapeDtypeStruct((HEADS, lp, HEAD_DIM), dt)
    return pl.pallas_call(
        functools.partial(_hgrn_in_kernel, layer=layer),
        grid=(lp // ROW_TILE, nc),
        in_specs=[
            pl.BlockSpec((ROW_TILE, D_MODEL), lambda i, c: (i, 0)),
            pl.BlockSpec((1, D_MODEL), lambda i, c: (0, 0)),
            wspec(0), wspec(1), wspec(2), wspec(3),
            pl.BlockSpec((DEPTH + 1, PROJ_TILE), lambda i, c: (0, c)),
            pl.BlockSpec((None, SIDE_ROWS, 2 * FFN_DIM),
                         lambda i, c: (n_done + side_block(i, c) // per_w, side_block(i, c) % per_w, 0)),
        ],
        out_specs=[hm_spec] * 5 + [
            pl.BlockSpec((None, SIDE_ROWS, 2 * FFN_PAD),
                         lambda i, c: (side_block(i, c) // per_w, side_block(i, c) % per_w, 0))],
        out_shape=[hm(BF16), hm(BF16), hm(F32), hm(BF16), hm(BF16),
                   jax.ShapeDtypeStruct((w_gu.shape[0] - n_done, D_MODEL, 2 * FFN_PAD), BF16)],
        scratch_shapes=[pltpu.VMEM((ROW_TILE, D_MODEL), BF16)],
        compiler_params=_params("arbitrary", "arbitrary"),
        name="hgrn_in",
    )(hp, gain.reshape(1, D_MODEL), w, w, w, w, lb_logits, w_gu)


_ROW_LEVELS = (64, 32, 16, 8)
_MM_LEVELS = (4, 2)
FAST_BLOCK = 32
FAST_RANGE = 90.0
OUT_SLICES = 8


def _hgrn_consts():
    t = np.arange(BLOCK)[:, None]
    j = np.arange(BLOCK)[None, :]
    blocks = [j <= t]
    for b in _MM_LEVELS:
        ref = 2 * b * (t // (2 * b)) + b - 1
        odd = (t // b) % 2 == 1
        blocks.append(np.where(odd, (j > ref) & (j <= t), (j > t) & (j <= ref)))
    g_all = np.concatenate(blocks, axis=0).astype(np.float32)
    masks = []
    for b in _ROW_LEVELS + _MM_LEVELS + (1,):
        masks.append((t // (2 * b) == j // (2 * b)) & ((t // b) % 2 == 1) & ((j // b) % 2 == 0))
    masks.append(t == j)
    masks.append((t // FAST_BLOCK == j // FAST_BLOCK) & (j <= t))
    return jnp.asarray(g_all, BF16), jnp.asarray(np.stack(masks).astype(np.float32))


def _hgrn_rec_kernel(q_ref, k_ref, lf_ref, v_ref, sg_ref, gmat_ref, mask_ref, on_ref, h_ref, w_ref,
                     out_ref, st_ref, og_ref):
    c = pl.program_id(0)

    @pl.when(c == 0)
    def _():
        st_ref[...] = jnp.zeros_like(st_ref)

    def boundary_exponent(b, size):
        parts = []
        for s in range(0, BLOCK, 2 * size):
            r = b[s + size - 1:s + size]
            parts += [r - b[s:s + size], b[s + size:s + 2 * size] - r]
        return jnp.concatenate(parts, axis=0)

    def middle_exponents(b):
        dq, dk = [], []
        for s in range(0, BLOCK, FAST_BLOCK):
            m = b[s + FAST_BLOCK // 2 - 1:s + FAST_BLOCK // 2]
            dq.append(b[s:s + FAST_BLOCK] - m)
            dk.append(m - b[s:s + FAST_BLOCK])
        return jnp.concatenate(dq, axis=0), jnp.concatenate(dk, axis=0)

    n_row, n_mm = len(_ROW_LEVELS), len(_MM_LEVELS)
    n_masks = n_row + n_mm + 2
    n_coarse = _ROW_LEVELS.index(FAST_BLOCK) + 1
    width = D_MODEL // OUT_SLICES

    def step(project, fast):
        hs = range(HEADS)
        n_lv = n_coarse + 1 if fast else n_masks
        g_rows = BLOCK if fast else BLOCK * (1 + n_mm)
        lf = [lf_ref[h] for h in hs]
        q = [q_ref[h] for h in hs]
        k = [k_ref[h] for h in hs]
        x3 = [jnp.dot(gmat_ref[:g_rows, :], jnp.concatenate(_split3(l), axis=1),
                      preferred_element_type=F32) for l in lf]
        x = [t[:, :HEAD_DIM] + t[:, HEAD_DIM:2 * HEAD_DIM] + t[:, 2 * HEAD_DIM:] for t in x3]
        b = [t[:BLOCK] for t in x]
        b_last = [t[BLOCK - 1:BLOCK] for t in b]
        att = [jnp.zeros((BLOCK, BLOCK), F32) for _ in hs]
        for lv in range(n_lv):
            for u in hs:
                if fast and lv == n_coarse:
                    dq, dk = middle_exponents(b[u])
                    a = lax.dot_general(q[u] * jnp.exp2(dq).astype(BF16),
                                        k[u] * jnp.exp2(dk).astype(BF16), _NT,
                                        preferred_element_type=F32)
                    att[u] = jnp.where(mask_ref[n_masks] > 0.0, a, att[u])
                    continue
                if lv < n_row:
                    el = jnp.exp2(boundary_exponent(b[u], _ROW_LEVELS[lv])).astype(BF16)
                    qt, kt = q[u] * el, k[u] * el
                elif lv < n_row + n_mm:
                    m = lv - n_row + 1
                    el = jnp.exp2(x[u][m * BLOCK:(m + 1) * BLOCK]).astype(BF16)
                    qt, kt = q[u] * el, k[u] * el
                elif lv == n_row + n_mm:
                    qt, kt = q[u] * jnp.exp2(lf[u]).astype(BF16), k[u]
                else:
                    qt, kt = q[u], k[u]
                a = lax.dot_general(qt, kt, _NT, preferred_element_type=F32)
                att[u] = att[u] + mask_ref[lv] * a
            if project:
                for piece in range(lv * OUT_SLICES // n_lv, (lv + 1) * OUT_SLICES // n_lv):
                    sl = slice(piece * width, (piece + 1) * width)
                    out_ref[:, sl] = h_ref[:, sl] + jnp.dot(og_ref[...], w_ref[:, sl],
                                                            preferred_element_type=F32)
        rows = pl.ds(pl.multiple_of((c % 2) * BLOCK, BLOCK), BLOCK)
        for h in hs:
            qb = q[h] * jnp.exp2(b[h]).astype(BF16)
            kd = k[h] * jnp.exp2(b_last[h] - b[h]).astype(BF16)
            vt = v_ref[h].astype(F32).T.astype(BF16)
            st = st_ref[h]
            lhs = jnp.concatenate([att[h].astype(BF16), qb], axis=1)
            rhs = jnp.concatenate([vt, st.astype(BF16)], axis=1)
            o = lax.dot_general(lhs, rhs, _NT, preferred_element_type=F32)
            st_ref[h] = st * jnp.exp2(b_last[h]) + jnp.dot(vt, kd, preferred_element_type=F32)
            og = _rms_rows(o) * on_ref[...] * sg_ref[h].astype(F32)
            og_ref[rows, h * HEAD_DIM:(h + 1) * HEAD_DIM] = og.astype(BF16)

    block_sums = [jnp.sum(lf_ref[h].reshape(BLOCK // FAST_BLOCK, FAST_BLOCK, HEAD_DIM), axis=1)
                  for h in range(HEADS)]
    fast_now = jnp.min(jnp.concatenate(block_sums, axis=0)) >= -FAST_RANGE
    project_now = (c % 2 == 0) & (c > 0)
    for project in (True, False):
        for fast in (True, False):
            pl.when((project_now if project else jnp.logical_not(project_now))
                    & (fast_now if fast else jnp.logical_not(fast_now)))(
                functools.partial(step, project, fast))


def _hgrn_rec(q, k, lf, v, sg, onorm, hp, w_out):
    lp = q.shape[1]
    n = lp // BLOCK
    assert n % 2 == 0
    gmat, masks = _hgrn_consts()
    hm_spec = pl.BlockSpec((HEADS, BLOCK, HEAD_DIM), lambda c: (0, jnp.minimum(c, n - 1), 0))
    row_spec = pl.BlockSpec((2 * BLOCK, D_MODEL), lambda c: (jnp.maximum(c // 2 - 1, 0), 0))
    return pl.pallas_call(
        _hgrn_rec_kernel,
        grid=(n + 1,),
        in_specs=[hm_spec] * 5 + [
            pl.BlockSpec(gmat.shape, lambda c: (0, 0)),
            pl.BlockSpec(masks.shape, lambda c: (0, 0, 0)),
            pl.BlockSpec((1, HEAD_DIM), lambda c: (0, 0)),
            row_spec,
            pl.BlockSpec((D_MODEL, D_MODEL), lambda c: (0, 0)),
        ],
        out_specs=row_spec,
        out_shape=jax.ShapeDtypeStruct((lp, D_MODEL), F32),
        scratch_shapes=[pltpu.VMEM((HEADS, HEAD_DIM, HEAD_DIM), F32),
                        pltpu.VMEM((2 * BLOCK, D_MODEL), BF16)],
        compiler_params=_params("arbitrary"),
        name="hgrn_rec",
    )(q, k, lf, v, sg, gmat, masks, onorm.reshape(1, HEAD_DIM), hp, w_out.astype(BF16))


def _fox_consts():
    pq = np.zeros((HEADS, LANES, 4 * HEADS), np.float32)
    pk = np.zeros((HEADS, 4 * HEADS, LANES), np.float32)
    one, pad = 3 * HEADS, 3 * HEADS + 1
    for h in range(HEADS):
        for part in range(3):
            pq[h, part, part * HEADS + h] = 1.0
            pq[h, 3 + part, one] = 1.0
            pk[h, one, part] = 1.0
            pk[h, part * HEADS + h, 3 + part] = -1.0
        pk[h, pad, 3] = NEG_BIG
    tri = np.tril(np.ones((ROW_TILE, ROW_TILE), np.float32))
    return jnp.asarray(pq, BF16), jnp.asarray(pk, BF16), jnp.asarray(tri, BF16)


def _fox_in_kernel(h_ref, g_ref, wq_ref, wk_ref, wv_ref, wg_ref, wf_ref, bf_ref, gq_ref, gk_ref,
                   tri_ref, pq_ref, pk_ref,
                   qqt_ref, kk_ref, vt_ref, sg_ref, qn_ref, qmax_ref, kmax_ref,
                   xn_ref, src_ref, srct_ref, carry_ref):
    i = pl.program_id(0)
    c = pl.program_id(1)
    hpt = PROJ_TILE // HEAD_DIM

    def project(xn):
        qp = jnp.dot(xn, wq_ref[...], preferred_element_type=F32)
        kp = jnp.dot(xn, wk_ref[...], preferred_element_type=F32)
        vp = jnp.dot(xn, wv_ref[...], preferred_element_type=F32)
        gp = jnp.dot(xn, wg_ref[...], preferred_element_type=F32)
        sg = _sigmoid(gp)
        scale = HEAD_DIM ** -0.5 * LOG2E
        ones = jnp.ones((HEAD_DIM, LANES), BF16)
        for hh in range(hpt):
            sl = slice(hh * HEAD_DIM, (hh + 1) * HEAD_DIM)
            head = hpt * c + hh
            q = (_rms_rows(qp[:, sl]) * (gq_ref[...] * scale)).astype(BF16)
            k = (_rms_rows(kp[:, sl]) * gk_ref[...]).astype(BF16)
            qt = q.astype(F32).T
            qqt_ref[hh, :HEAD_DIM, :] = qt.astype(BF16)
            qqt_ref[hh, HEAD_DIM:, :] = jnp.dot(
                pq_ref[hh], srct_ref[...], preferred_element_type=F32).astype(BF16)
            kk_ref[hh, :, :HEAD_DIM] = k
            kk_ref[hh, :, HEAD_DIM:] = jnp.dot(
                src_ref[...], pk_ref[hh], preferred_element_type=F32).astype(BF16)
            vt_ref[hh, :HEAD_DIM, :] = vp[:, sl].T.astype(BF16)
            vt_ref[hh, HEAD_DIM:, :] = jnp.ones((V_ROWS - HEAD_DIM, ROW_TILE), BF16)
            sg_ref[hh] = sg[:, sl].astype(BF16)
            qn = jnp.sqrt(jnp.sum(qt * qt, axis=0, keepdims=True))
            qn_ref[hh] = qn
            kf = k.astype(F32)
            kn2 = jnp.dot((kf * kf).astype(BF16), ones, preferred_element_type=F32)
            kn = jnp.sqrt(jnp.max(kn2, axis=0, keepdims=True)) * NORM_SLACK
            qmax_ref[head] = jnp.maximum(qmax_ref[head], qn)
            kmax_ref[head] = jnp.maximum(kmax_ref[head], kn)

    @pl.when((i == 0) & (c == 0))
    def _():
        carry_ref[...] = jnp.zeros_like(carry_ref)
        qmax_ref[...] = jnp.zeros_like(qmax_ref)
        kmax_ref[...] = jnp.zeros_like(kmax_ref)

    @pl.when(c == 0)
    def _():
        xn = (_rms_rows(h_ref[...]) * g_ref[...]).astype(BF16)
        xn_ref[...] = xn
        z = lax.dot_general(xn, wf_ref[...], _NT, preferred_element_type=F32) + bf_ref[...]
        row = i * ROW_TILE + lax.broadcasted_iota(jnp.int32, (ROW_TILE, HEADS), 0)
        lf = jnp.minimum(z, 0.0) - jnp.log(1.0 + jnp.exp(-jnp.abs(z)))
        lf = jnp.where(row >= META_PAD, lf, 0.0)
        cum = carry_ref[...]
        for part in _split3(lf):
            cum = cum + jnp.dot(tri_ref[...], part, preferred_element_type=F32)
        carry_ref[...] = cum[ROW_TILE - 1:ROW_TILE]
        col = lax.broadcasted_iota(jnp.int32, (ROW_TILE, HEADS), 1)
        extra = jnp.where(col == 0, 1.0, jnp.where((col == 1) & (row < META_PAD), 1.0, 0.0))
        src = jnp.concatenate([p.astype(F32) for p in _split3(cum * LOG2E)] + [extra], axis=1)
        src_ref[...] = src.astype(BF16)
        srct_ref[...] = jnp.concatenate([src, jnp.zeros_like(src)], axis=1).T[:4 * HEADS].astype(BF16)
        project(xn)

    @pl.when(c > 0)
    def _():
        project(xn_ref[...])


def _fox_in(hp, gain, w_in, j, b_f, qnorm, knorm):
    lp = hp.shape[0]
    nc = D_MODEL // PROJ_TILE
    hpt = PROJ_TILE // HEAD_DIM
    kdim = HEAD_DIM + LANES
    w, wf = _cast_proj_weights(w_in, j, 4 * D_MODEL)
    pq, pk, tri = _fox_consts()
    wspec = lambda s: pl.BlockSpec((D_MODEL, PROJ_TILE), lambda i, c, s=s: (0, s * nc + c))
    vec = lambda n: pl.BlockSpec((1, n), lambda i, c: (0, 0))
    return pl.pallas_call(
        _fox_in_kernel,
        grid=(lp // ROW_TILE, nc),
        in_specs=[
            pl.BlockSpec((ROW_TILE, D_MODEL), lambda i, c: (i, 0)),
            vec(D_MODEL),
            wspec(0), wspec(1), wspec(2), wspec(3),
            pl.BlockSpec((HEADS, D_MODEL), lambda i, c: (0, 0)),
            vec(HEADS), vec(HEAD_DIM), vec(HEAD_DIM),
            pl.BlockSpec((ROW_TILE, ROW_TILE), lambda i, c: (0, 0)),
            pl.BlockSpec((hpt, LANES, 4 * HEADS), lambda i, c: (c, 0, 0)),
            pl.BlockSpec((hpt, 4 * HEADS, LANES), lambda i, c: (c, 0, 0)),
        ],
        out_specs=[pl.BlockSpec((hpt, kdim, ROW_TILE), lambda i, c: (c, 0, i)),
                   pl.BlockSpec((hpt, ROW_TILE, kdim), lambda i, c: (c, i, 0)),
                   pl.BlockSpec((hpt, V_ROWS, ROW_TILE), lambda i, c: (c, 0, i)),
                   pl.BlockSpec((hpt, ROW_TILE, HEAD_DIM), lambda i, c: (c, i, 0)),
                   pl.BlockSpec((hpt, 1, ROW_TILE), lambda i, c: (c, 0, i)),
                   pl.BlockSpec((HEADS, 1, ROW_TILE), lambda i, c: (0, 0, 0)),
                   pl.BlockSpec((HEADS, 1, LANES), lambda i, c: (0, 0, 0))],
        out_shape=[jax.ShapeDtypeStruct((HEADS, kdim, lp), BF16),
                   jax.ShapeDtypeStruct((HEADS, lp, kdim), BF16),
                   jax.ShapeDtypeStruct((HEADS, V_ROWS, lp), BF16),
                   jax.ShapeDtypeStruct((HEADS, lp, HEAD_DIM), BF16),
                   jax.ShapeDtypeStruct((HEADS, 1, lp), F32),
                   jax.ShapeDtypeStruct((HEADS, 1, ROW_TILE), F32),
                   jax.ShapeDtypeStruct((HEADS, 1, LANES), F32)],
        scratch_shapes=[pltpu.VMEM((ROW_TILE, D_MODEL), BF16),
                        pltpu.VMEM((ROW_TILE, 4 * HEADS), BF16),
                        pltpu.VMEM((4 * HEADS, ROW_TILE), BF16),
                        pltpu.VMEM((1, HEADS), F32)],
        compiler_params=_params("arbitrary", "arbitrary"),
        name="fox_in",
    )(hp, gain.reshape(1, D_MODEL), w, w, w, w, wf, b_f.reshape(1, HEADS),
      qnorm.reshape(1, HEAD_DIM), knorm.reshape(1, HEAD_DIM), tri, pq, pk)


def _fox_attn_kernel(safe_ref, kmax_ref, qqt_ref, qn_ref, kk_ref, vt_ref, sg_ref, o_ref,
                     m_ref, acc_ref, ring_ref):
    h = pl.program_id(0)
    i = pl.program_id(1)
    qs = i * ATT_Q
    n_groups = ATT_Q // ATT_GROUP
    n_steps = i * (ATT_Q // ATT_K)

    def scores(item):
        g, ks, nk, _ = item
        gs = slice(g * ATT_GROUP, (g + 1) * ATT_GROUP)
        return jnp.dot(kk_ref[pl.ds(ks, nk), :], qqt_ref[:, gs], preferred_element_type=F32)

    def step_items(j):
        return [(g, pl.multiple_of(j * ATT_K + sum(ATT_KSPLIT[:sub]), ATT_GROUP), nk, False)
                for sub, nk in enumerate(ATT_KSPLIT) for g in range(n_groups)]

    def sweep(fixed_shift):
        acc_ref[...] = jnp.zeros_like(acc_ref)
        if fixed_shift:
            m_ref[...] = qn_ref[...] * kmax_ref[h]
        else:
            m_ref[...] = jnp.full_like(m_ref, -jnp.inf)

        def update(item, s):
            g, ks, nk, masked = item
            gs = slice(g * ATT_GROUP, (g + 1) * ATT_GROUP)
            if masked:
                r = lax.broadcasted_iota(jnp.int32, s.shape, 0)
                c = lax.broadcasted_iota(jnp.int32, s.shape, 1) + g * ATT_GROUP
                s = jnp.where(r <= c, s, NEG_BIG)
            if fixed_shift:
                p = jnp.exp2(s - m_ref[:, gs])
                acc_ref[:HEAD_DIM, gs] += jnp.dot(vt_ref[:HEAD_DIM, pl.ds(ks, nk)], p.astype(BF16),
                                                  preferred_element_type=F32)
                acc_ref[HEAD_DIM:HEAD_DIM + 1, gs] += jnp.sum(p, axis=0, keepdims=True)
            else:
                vt = vt_ref[:, pl.ds(ks, nk)]
                m_prev = m_ref[:, gs]
                m_new = jnp.maximum(m_prev, jnp.max(s, axis=0, keepdims=True))
                alpha = jnp.exp2(m_prev - m_new)
                p = jnp.exp2((s - m_new).astype(BF16))
                m_ref[:, gs] = m_new
                acc_ref[:, gs] = alpha * acc_ref[:, gs] + jnp.dot(vt, p, preferred_element_type=F32)

        def run(items, first, ahead):
            pending = list(first) if first else [scores(item) for item in items[:ATT_AHEAD]]
            for n, item in enumerate(items):
                if n + ATT_AHEAD < len(items):
                    pending.append(scores(items[n + ATT_AHEAD]))
                else:
                    a = n + ATT_AHEAD - len(items)
                    ring_ref[a] = scores(ahead[a])
                update(item, pending.pop(0))

        run([(g, pl.multiple_of(qs, ATT_Q), (g + 1) * ATT_GROUP, True) for g in range(n_groups)],
            None, step_items(0)[:ATT_AHEAD])

        def body(j, carry):
            run(step_items(j), [ring_ref[a] for a in range(ATT_AHEAD)],
                step_items(jnp.minimum(j + 1, n_steps - 1))[:ATT_AHEAD])
            return carry

        lax.fori_loop(0, n_steps, body, 0)

    @pl.when(safe_ref[h] == 1)
    def _():
        sweep(True)

    @pl.when(safe_ref[h] != 1)
    def _():
        sweep(False)

    o = (acc_ref[:HEAD_DIM, :] / acc_ref[HEAD_DIM:HEAD_DIM + 1, :]).T * sg_ref[...].astype(F32)
    row = qs + lax.broadcasted_iota(jnp.int32, o.shape, 0)
    o_ref[...] = jnp.where(row >= META_PAD, o, 0.0).astype(BF16)


def _fox_attn(qqt, qn, qmax, kmax, kk, vt, sg):
    lp = kk.shape[1]
    kdim = kk.shape[2]
    qmax, kmax = jnp.max(qmax, axis=(1, 2)), jnp.max(kmax, axis=(1, 2))
    safe = (2.0 * qmax * kmax <= SAFE_SHIFT_RANGE).astype(jnp.int32)
    grid_spec = pltpu.PrefetchScalarGridSpec(
        num_scalar_prefetch=2,
        grid=(HEADS, lp // ATT_Q),
        in_specs=[pl.BlockSpec((None, kdim, ATT_Q), lambda h, i, *_: (h, 0, i)),
                  pl.BlockSpec((None, 1, ATT_Q), lambda h, i, *_: (h, 0, i)),
                  pl.BlockSpec((None, lp, kdim), lambda h, i, *_: (h, 0, 0)),
                  pl.BlockSpec((None, V_ROWS, lp), lambda h, i, *_: (h, 0, 0)),
                  pl.BlockSpec((None, ATT_Q, HEAD_DIM), lambda h, i, *_: (h, i, 0))],
        out_specs=pl.BlockSpec((ATT_Q, HEAD_DIM), lambda h, i, *_: (i, h)),
        scratch_shapes=[pltpu.VMEM((1, ATT_Q), F32), pltpu.VMEM((V_ROWS, ATT_Q), F32),
                        pltpu.VMEM((ATT_AHEAD, ATT_KSPLIT[0], ATT_GROUP), F32)])
    return pl.pallas_call(
        _fox_attn_kernel,
        grid_spec=grid_spec,
        out_shape=jax.ShapeDtypeStruct((lp, D_MODEL), BF16),
        compiler_params=_params("parallel", "arbitrary"),
        name="fox_attn",
    )(safe, kmax, qqt, qn, kk, vt, sg)


def kernel(x, meta_tokens, norm_g, ffn_w_gu, ffn_w_down, lb_logits, hg_w_in, hg_w_out, hg_onorm,
           fox_w_in, fox_b_f, fox_w_out, fox_qnorm, fox_knorm):
    assert x.shape[0] == 1 and x.shape[2] == D_MODEL
    seq = x.shape[1]
    real = META_PAD + N_META + seq
    lp = -(-real // ATT_Q) * ATT_Q
    assert seq % BLOCK == 0
    head = jnp.concatenate([jnp.zeros((META_PAD, D_MODEL), F32), meta_tokens.astype(F32)], axis=0)
    w_gu = ffn_w_gu.reshape(2 * DEPTH, D_MODEL, 2 * FFN_DIM)
    wgu_first, wd = _cast_ffn_weights(w_gu, ffn_w_down.reshape(2 * DEPTH, FFN_DIM, D_MODEL), 1)
    wgu_rest = None
    gate_up = lambda a: (wgu_first, 0) if a == 0 else (wgu_rest, a - 1)
    hp = x[0]
    for layer in range(DEPTH):
        hp = _ffn(hp, head, norm_g[layer, 0], *gate_up(2 * layer), wd, 2 * layer,
                  -1 if layer == 0 else 0, lp)
        j = layer // 2
        if layer % 2 == 0:
            q, k, lf, v, sg, cast = _hgrn_in(hp, norm_g[layer, 1], hg_w_in, j, lb_logits, layer,
                                             w_gu, 1)
            wgu_rest = cast if wgu_rest is None else wgu_rest
            hp = _hgrn_rec(q, k, lf, v, sg, hg_onorm[j], hp, hg_w_out[j])
        else:
            qqt, kk, vt, sg, qn, qmax, kmax = _fox_in(hp, norm_g[layer, 1], fox_w_in, j, fox_b_f[j],
                                                      fox_qnorm[j], fox_knorm[j])
            og = _fox_attn(qqt, qn, qmax, kmax, kk, vt, sg)
            hp = _out_proj(hp, og, fox_w_out[j])
        last = layer == DEPTH - 1
        hp = _ffn(hp, head, norm_g[layer, 2], *gate_up(2 * layer + 1), wd, 2 * layer + 1,
                  1 if last else 0, seq if last else lp)
    return hp[None]
```

```python
import functools

import numpy as np
import jax
import jax.numpy as jnp
from jax import lax
from jax.experimental import pallas as pl
from jax.experimental.pallas import tpu as pltpu

F32 = jnp.float32
BF16 = jnp.bfloat16

D_MODEL = 2048
DEPTH = 2
N_META = 16
BLOCK = 128
META_PAD = (-N_META) % BLOCK
RMS_EPS = 1e-6
HEADS = 16
HEAD_DIM = 128
FFN_DIM = 5504
NEG_BIG = -1e30
LOG2E = 1.4426950408889634

LANES = 128
ROW_TILE = 640
FFN_TILE = 512
FFN_PAD = -(-FFN_DIM // FFN_TILE) * FFN_TILE
PROJ_TILE = 512
CAST_ROWS = 256
SIDE_ROWS = 64
ATT_Q = 1280
ATT_K = 1280
ATT_KSPLIT = (512, 768)
ATT_GROUP = 256
ATT_AHEAD = 2
V_ROWS = HEAD_DIM + 16
NORM_SLACK = 1.01
SAFE_SHIFT_RANGE = 80.0
VMEM_LIMIT = 56 * 1024 * 1024

_NT = (((1,), (1,)), ((), ()))


def _params(*sem):
    return pltpu.CompilerParams(dimension_semantics=sem, vmem_limit_bytes=VMEM_LIMIT)


def _sigmoid(x):
    return 1.0 / (1.0 + jnp.exp(-x))


def _rms_rows(x):
    return x * lax.rsqrt(jnp.mean(x * x, axis=-1, keepdims=True) + RMS_EPS)


def _split3(x):
    hi = x.astype(BF16)
    r1 = x - hi.astype(F32)
    mid = r1.astype(BF16)
    lo = (r1 - mid.astype(F32)).astype(BF16)
    return hi, mid, lo


def _cast_wgu_kernel(w_ref, o_ref):
    zeros = jnp.zeros((o_ref.shape[0], FFN_PAD - FFN_DIM), BF16)
    o_ref[:, :FFN_DIM] = w_ref[:, :FFN_DIM].astype(BF16)
    o_ref[:, FFN_DIM:FFN_PAD] = zeros
    o_ref[:, FFN_PAD:FFN_PAD + FFN_DIM] = w_ref[:, FFN_DIM:].astype(BF16)
    o_ref[:, FFN_PAD + FFN_DIM:] = zeros


def _cast_wd_kernel(w_ref, o_ref):
    row = pl.program_id(1) * FFN_TILE + lax.broadcasted_iota(jnp.int32, (FFN_TILE, D_MODEL), 0)
    o_ref[...] = jnp.where(row < FFN_DIM, w_ref[...], 0.0).astype(BF16)


def _cast_kernel(w_ref, o_ref):
    o_ref[...] = w_ref[...].astype(BF16)


def _cast_transposed_kernel(wt_ref, tail_ref, o_ref, t_ref):
    o_ref[...] = wt_ref[...].T.astype(BF16)
    t_ref[...] = tail_ref[...].astype(BF16)


def _cast_ffn_weights(w_gu, w_down, n_gu):
    n = w_gu.shape[0]
    wgu = pl.pallas_call(
        _cast_wgu_kernel,
        grid=(n_gu, D_MODEL // CAST_ROWS),
        in_specs=[pl.BlockSpec((None, CAST_ROWS, 2 * FFN_DIM), lambda a, r: (a, r, 0))],
        out_specs=pl.BlockSpec((None, CAST_ROWS, 2 * FFN_PAD), lambda a, r: (a, r, 0)),
        out_shape=jax.ShapeDtypeStruct((n_gu, D_MODEL, 2 * FFN_PAD), BF16),
        compiler_params=_params("parallel", "parallel"),
        name="cast_wgu",
    )(w_gu)
    wd = pl.pallas_call(
        _cast_wd_kernel,
        grid=(n, FFN_PAD // FFN_TILE),
        in_specs=[pl.BlockSpec((None, FFN_TILE, D_MODEL), lambda a, r: (a, r, 0))],
        out_specs=pl.BlockSpec((None, FFN_TILE, D_MODEL), lambda a, r: (a, r, 0)),
        out_shape=jax.ShapeDtypeStruct((n, FFN_PAD, D_MODEL), BF16),
        compiler_params=_params("parallel", "parallel"),
        name="cast_wd",
    )(w_down)
    return wgu, wd


def _cast_proj_weights(w, j, n_main):
    width = w.shape[2]
    main = jax.ShapeDtypeStruct((D_MODEL, n_main), BF16)
    if width == n_main:
        return pl.pallas_call(
            _cast_kernel, grid=(D_MODEL // CAST_ROWS,),
            in_specs=[pl.BlockSpec((None, CAST_ROWS, width), lambda r: (j, r, 0))],
            out_specs=pl.BlockSpec((CAST_ROWS, n_main), lambda r: (r, 0)),
            out_shape=main, compiler_params=_params("parallel"), name="cast_proj")(w)
    tail = width - n_main
    assert n_main % tail == 0
    wt = jnp.swapaxes(w, 1, 2)
    return pl.pallas_call(
        _cast_transposed_kernel, grid=(n_main // CAST_ROWS,),
        in_specs=[pl.BlockSpec((None, CAST_ROWS, D_MODEL), lambda r: (j, r, 0)),
                  pl.BlockSpec((None, tail, D_MODEL), lambda r: (j, n_main // tail, 0))],
        out_specs=[pl.BlockSpec((D_MODEL, CAST_ROWS), lambda r: (0, r)),
                   pl.BlockSpec((tail, D_MODEL), lambda r: (0, 0))],
        out_shape=[main, jax.ShapeDtypeStruct((tail, D_MODEL), BF16)],
        compiler_params=_params("arbitrary"), name="cast_proj_t")(wt, wt)


ROW_CHUNKS = ROW_TILE // BLOCK


def _ffn_kernel(*refs, shift, n_src):
    n_in = ROW_CHUNKS if shift else 1
    chunk_refs = refs[:n_in]
    head_ref, g_ref, wg_ref, wu_ref, wd_ref, o_ref, xn_ref = refs[n_in:]

    j = pl.program_id(1)

    def step(xn):
        a = jnp.dot(xn, wg_ref[...], preferred_element_type=F32)
        b = jnp.dot(xn, wu_ref[...], preferred_element_type=F32)
        act = (a * (0.5 * _sigmoid(a)) * b).astype(BF16)
        return jnp.dot(act, wd_ref[...], preferred_element_type=F32)

    @pl.when(j == 0)
    def _():
        parts = []
        for u, ref in enumerate(chunk_refs):
            idx = ROW_CHUNKS * pl.program_id(0) + u + shift
            blk = ref[...]
            if shift < 0:
                blk = jnp.where(idx < 0, head_ref[...], jnp.where(idx >= n_src, 0.0, blk))
            parts.append(blk)
        x = jnp.concatenate(parts, axis=0) if shift else parts[0]
        xn = (_rms_rows(x) * g_ref[...]).astype(BF16)
        xn_ref[...] = xn
        o_ref[...] = x + step(xn)

    @pl.when(j > 0)
    def _():
        o_ref[...] += step(xn_ref[...])


def _ffn(src, head, gain, wgu, ag, wd, a, shift, out_rows):
    n_src = src.shape[0] // BLOCK
    nf = FFN_PAD // FFN_TILE
    chunk = lambda u: pl.BlockSpec(
        (BLOCK, D_MODEL), lambda i, j: (jnp.clip(ROW_CHUNKS * i + u + shift, 0, n_src - 1), 0))
    row_specs = ([chunk(u) for u in range(ROW_CHUNKS)] if shift else
                 [pl.BlockSpec((ROW_TILE, D_MODEL), lambda i, j: (i, 0))])
    return pl.pallas_call(
        functools.partial(_ffn_kernel, shift=shift, n_src=n_src),
        grid=(pl.cdiv(out_rows, ROW_TILE), nf),
        in_specs=row_specs + [
            pl.BlockSpec((BLOCK, D_MODEL), lambda i, j: (0, 0)),
            pl.BlockSpec((1, D_MODEL), lambda i, j: (0, 0)),
            pl.BlockSpec((None, D_MODEL, FFN_TILE), lambda i, j: (ag, 0, j)),
            pl.BlockSpec((None, D_MODEL, FFN_TILE), lambda i, j: (ag, 0, j + nf)),
            pl.BlockSpec((None, FFN_TILE, D_MODEL), lambda i, j: (a, j, 0)),
        ],
        out_specs=pl.BlockSpec((ROW_TILE, D_MODEL), lambda i, j: (i, 0)),
        out_shape=jax.ShapeDtypeStruct((out_rows, D_MODEL), F32),
        scratch_shapes=[pltpu.VMEM((ROW_TILE, D_MODEL), BF16)],
        compiler_params=_params("parallel", "arbitrary"),
        name="ffn",
    )(*([src] * len(row_specs)), head, gain.reshape(1, D_MODEL), wgu, wgu, wd)


def _out_proj_kernel(h_ref, o_ref, w_ref, out_ref):
    out_ref[...] = h_ref[...] + jnp.dot(o_ref[...], w_ref[...], preferred_element_type=F32)


def _out_proj(hp, og, w_out):
    lp = hp.shape[0]
    return pl.pallas_call(
        _out_proj_kernel,
        grid=(lp // ROW_TILE,),
        in_specs=[
            pl.BlockSpec((ROW_TILE, D_MODEL), lambda i: (i, 0)),
            pl.BlockSpec((ROW_TILE, D_MODEL), lambda i: (i, 0)),
            pl.BlockSpec((D_MODEL, D_MODEL), lambda i: (0, 0)),
        ],
        out_specs=pl.BlockSpec((ROW_TILE, D_MODEL), lambda i: (i, 0)),
        out_shape=jax.ShapeDtypeStruct((lp, D_MODEL), F32),
        compiler_params=_params("parallel"),
        name="out_proj",
    )(hp, og, w_out.astype(BF16))


def _hgrn_in_kernel(h_ref, g_ref, wq_ref, wf_ref, wi_ref, wg_ref, lbl_ref, side_ref,
                    q_ref, k_ref, lf_ref, v_ref, sg_ref, side_out_ref, xn_ref, *, layer):
    def project(xn):
        _cast_wgu_kernel(side_ref, side_out_ref)
        qp = jnp.dot(xn, wq_ref[...], preferred_element_type=F32)
        fp = jnp.dot(xn, wf_ref[...], preferred_element_type=F32)
        ip = jnp.dot(xn, wi_ref[...], preferred_element_type=F32)
        gp = jnp.dot(xn, wg_ref[...], preferred_element_type=F32)
        lg = lbl_ref[...]
        e = jnp.exp(lg - jnp.max(lg, axis=0, keepdims=True))
        lb = jnp.sum(e[:layer + 1], axis=0, keepdims=True) / jnp.sum(e, axis=0, keepdims=True)
        q = qp * _sigmoid(qp)
        fg = lb + (1.0 - lb) * _sigmoid(fp)
        k = 1.0 - fg
        lf = jnp.log(fg) * LOG2E
        sg = _sigmoid(gp)
        for hh in range(PROJ_TILE // HEAD_DIM):
            sl = slice(hh * HEAD_DIM, (hh + 1) * HEAD_DIM)
            q_ref[hh] = q[:, sl].astype(BF16)
            k_ref[hh] = k[:, sl].astype(BF16)
            lf_ref[hh] = lf[:, sl]
            v_ref[hh] = ip[:, sl].astype(BF16)
            sg_ref[hh] = sg[:, sl].astype(BF16)

    @pl.when(pl.program_id(1) == 0)
    def _():
        xn = (_rms_rows(h_ref[...]) * g_ref[...]).astype(BF16)
        xn_ref[...] = xn
        project(xn)

    @pl.when(pl.program_id(1) > 0)
    def _():
        project(xn_ref[...])


def _hgrn_in(hp, gain, w_in, j, lb_logits, layer, w_gu, n_done):
    lp = hp.shape[0]
    nc = D_MODEL // PROJ_TILE
    hpt = PROJ_TILE // HEAD_DIM
    per_w = D_MODEL // SIDE_ROWS
    n_side = (w_gu.shape[0] - n_done) * per_w
    assert n_side <= (lp // ROW_TILE) * nc
    side_block = lambda i, c: jnp.minimum(i * nc + c, n_side - 1)
    w = _cast_proj_weights(w_in, j, 4 * D_MODEL)
    wspec = lambda s: pl.BlockSpec((D_MODEL, PROJ_TILE), lambda i, c, s=s: (0, s * nc + c))
    hm_spec = pl.BlockSpec((hpt, ROW_TILE, HEAD_DIM), lambda i, c: (c, i, 0))
    hm = lambda dt: jax.ShapeDtypeStruct((HEADS, lp, HEAD_DIM), dt)
    return pl.pallas_call(
        functools.partial(_hgrn_in_kernel, layer=layer),
        grid=(lp // ROW_TILE, nc),
        in_specs=[
            pl.BlockSpec((ROW_TILE, D_MODEL), lambda i, c: (i, 0)),
            pl.BlockSpec((1, D_MODEL), lambda i, c: (0, 0)),
            wspec(0), wspec(1), wspec(2), wspec(3),
            pl.BlockSpec((DEPTH + 1, PROJ_TILE), lambda i, c: (0, c)),
            pl.BlockSpec((None, SIDE_ROWS, 2 * FFN_DIM),
                         lambda i, c: (n_done + side_block(i, c) // per_w, side_block(i, c) % per_w, 0)),
        ],
        out_specs=[hm_spec] * 5 + [
            pl.BlockSpec((None, SIDE_ROWS, 2 * FFN_PAD),
                         lambda i, c: (side_block(i, c) // per_w, side_block(i, c) % per_w, 0))],
        out_shape=[hm(BF16), hm(BF16), hm(F32), hm(BF16), hm(BF16),
                   jax.ShapeDtypeStruct((w_gu.shape[0] - n_done, D_MODEL, 2 * FFN_PAD), BF16)],
        scratch_shapes=[pltpu.VMEM((ROW_TILE, D_MODEL), BF16)],
        compiler_params=_params("arbitrary", "arbitrary"),
        name="hgrn_in",
    )(hp, gain.reshape(1, D_MODEL), w, w, w, w, lb_logits, w_gu)


_ROW_LEVELS = (64, 32, 16, 8)
_MM_LEVELS = (4, 2)
FAST_BLOCKS = (64, 32)
FAST_RANGE = 90.0
OUT_SLICES = 8


def _hgrn_consts():
    t = np.arange(BLOCK)[:, None]
    j = np.arange(BLOCK)[None, :]
    blocks = [j <= t]
    for b in _MM_LEVELS:
        ref = 2 * b * (t // (2 * b)) + b - 1
        odd = (t // b) % 2 == 1
        blocks.append(np.where(odd, (j > ref) & (j <= t), (j > t) & (j <= ref)))
    g_all = np.concatenate(blocks, axis=0).astype(np.float32)
    masks = []
    for b in _ROW_LEVELS + _MM_LEVELS + (1,):
        masks.append((t // (2 * b) == j // (2 * b)) & ((t // b) % 2 == 1) & ((j // b) % 2 == 0))
    masks.append(t == j)
    masks += [(t // fb == j // fb) & (j <= t) for fb in FAST_BLOCKS]
    return jnp.asarray(g_all, BF16), jnp.asarray(np.stack(masks).astype(np.float32))


def _hgrn_rec_kernel(q_ref, k_ref, lf_ref, v_ref, sg_ref, gmat_ref, mask_ref, on_ref, h_ref, w_ref,
                     out_ref, st_ref, og_ref):
    c = pl.program_id(0)

    @pl.when(c == 0)
    def _():
        st_ref[...] = jnp.zeros_like(st_ref)

    def boundary_exponent(b, size):
        parts = []
        for s in range(0, BLOCK, 2 * size):
            r = b[s + size - 1:s + size]
            parts += [r - b[s:s + size], b[s + size:s + 2 * size] - r]
        return jnp.concatenate(parts, axis=0)

    def middle_exponents(b, size):
        dq, dk = [], []
        for s in range(0, BLOCK, size):
            m = b[s + size // 2 - 1:s + size // 2]
            dq.append(b[s:s + size] - m)
            dk.append(m - b[s:s + size])
        return jnp.concatenate(dq, axis=0), jnp.concatenate(dk, axis=0)

    n_row, n_mm = len(_ROW_LEVELS), len(_MM_LEVELS)
    n_masks = n_row + n_mm + 2
    width = D_MODEL // OUT_SLICES

    def step(project, fast):
        hs = range(HEADS)
        n_coarse = None if fast is None else _ROW_LEVELS.index(FAST_BLOCKS[fast]) + 1
        n_lv = n_masks if fast is None else n_coarse + 1
        g_rows = BLOCK * (1 + n_mm) if fast is None else BLOCK
        lf = [lf_ref[h] for h in hs]
        q = [q_ref[h] for h in hs]
        k = [k_ref[h] for h in hs]
        x3 = [jnp.dot(gmat_ref[:g_rows, :], jnp.concatenate(_split3(l), axis=1),
                      preferred_element_type=F32) for l in lf]
        x = [t[:, :HEAD_DIM] + t[:, HEAD_DIM:2 * HEAD_DIM] + t[:, 2 * HEAD_DIM:] for t in x3]
        b = [t[:BLOCK] for t in x]
        b_last = [t[BLOCK - 1:BLOCK] for t in b]
        att = [jnp.zeros((BLOCK, BLOCK), F32) for _ in hs]
        for lv in range(n_lv):
            for u in hs:
                if lv == n_coarse:
                    dq, dk = middle_exponents(b[u], FAST_BLOCKS[fast])
                    a = lax.dot_general(q[u] * jnp.exp2(dq).astype(BF16),
                                        k[u] * jnp.exp2(dk).astype(BF16), _NT,
                                        preferred_element_type=F32)
                    att[u] = jnp.where(mask_ref[n_masks + fast] > 0.0, a, att[u])
                    continue
                if lv < n_row:
                    el = jnp.exp2(boundary_exponent(b[u], _ROW_LEVELS[lv])).astype(BF16)
                    qt, kt = q[u] * el, k[u] * el
                elif lv < n_row + n_mm:
                    m = lv - n_row + 1
                    el = jnp.exp2(x[u][m * BLOCK:(m + 1) * BLOCK]).astype(BF16)
                    qt, kt = q[u] * el, k[u] * el
                elif lv == n_row + n_mm:
                    qt, kt = q[u] * jnp.exp2(lf[u]).astype(BF16), k[u]
                else:
                    qt, kt = q[u], k[u]
                a = lax.dot_general(qt, kt, _NT, preferred_element_type=F32)
                att[u] = att[u] + mask_ref[lv] * a
            if project:
                for piece in range(lv * OUT_SLICES // n_lv, (lv + 1) * OUT_SLICES // n_lv):
                    sl = slice(piece * width, (piece + 1) * width)
                    out_ref[:, sl] = h_ref[:, sl] + jnp.dot(og_ref[...], w_ref[:, sl],
                                                            preferred_element_type=F32)
        rows = pl.ds(pl.multiple_of((c % 2) * BLOCK, BLOCK), BLOCK)
        for h in hs:
            qb = q[h] * jnp.exp2(b[h]).astype(BF16)
            kd = k[h] * jnp.exp2(b_last[h] - b[h]).astype(BF16)
            vt = v_ref[h].astype(F32).T.astype(BF16)
            st = st_ref[h]
            lhs = jnp.concatenate([att[h].astype(BF16), qb], axis=1)
            rhs = jnp.concatenate([vt, st.astype(BF16)], axis=1)
            o = lax.dot_general(lhs, rhs, _NT, preferred_element_type=F32)
            st_ref[h] = st * jnp.exp2(b_last[h]) + jnp.dot(vt, kd, preferred_element_type=F32)
            og = _rms_rows(o) * on_ref[...] * sg_ref[h].astype(F32)
            og_ref[rows, h * HEAD_DIM:(h + 1) * HEAD_DIM] = og.astype(BF16)

    small = min(FAST_BLOCKS)
    worst = [None] * len(FAST_BLOCKS)
    for h in range(HEADS):
        sums = jnp.sum(lf_ref[h].reshape(BLOCK // small, small, HEAD_DIM), axis=1)
        for n, size in enumerate(FAST_BLOCKS):
            per = size // small
            for g in range(BLOCK // size):
                total = sums[g * per:g * per + 1]
                for r in range(1, per):
                    total = total + sums[g * per + r:g * per + r + 1]
                worst[n] = total if worst[n] is None else jnp.minimum(worst[n], total)
    ok = [jnp.min(w) >= -FAST_RANGE for w in worst]
    project_now = (c % 2 == 0) & (c > 0)
    earlier = jnp.bool_(False)
    for fast in list(range(len(FAST_BLOCKS))) + [None]:
        chosen = jnp.logical_not(earlier) & (ok[fast] if fast is not None else True)
        for project in (True, False):
            pl.when(chosen & (project_now if project else jnp.logical_not(project_now)))(
                functools.partial(step, project, fast))
        earlier = earlier | chosen


def _hgrn_rec(q, k, lf, v, sg, onorm, hp, w_out):
    lp = q.shape[1]
    n = lp // BLOCK
    assert n % 2 == 0
    gmat, masks = _hgrn_consts()
    hm_spec = pl.BlockSpec((HEADS, BLOCK, HEAD_DIM), lambda c: (0, jnp.minimum(c, n - 1), 0))
    row_spec = pl.BlockSpec((2 * BLOCK, D_MODEL), lambda c: (jnp.maximum(c // 2 - 1, 0), 0))
    return pl.pallas_call(
        _hgrn_rec_kernel,
        grid=(n + 1,),
        in_specs=[hm_spec] * 5 + [
            pl.BlockSpec(gmat.shape, lambda c: (0, 0)),
            pl.BlockSpec(masks.shape, lambda c: (0, 0, 0)),
            pl.BlockSpec((1, HEAD_DIM), lambda c: (0, 0)),
            row_spec,
            pl.BlockSpec((D_MODEL, D_MODEL), lambda c: (0, 0)),
        ],
        out_specs=row_spec,
        out_shape=jax.ShapeDtypeStruct((lp, D_MODEL), F32),
        scratch_shapes=[pltpu.VMEM((HEADS, HEAD_DIM, HEAD_DIM), F32),
                        pltpu.VMEM((2 * BLOCK, D_MODEL), BF16)],
        compiler_params=_params("arbitrary"),
        name="hgrn_rec",
    )(q, k, lf, v, sg, gmat, masks, onorm.reshape(1, HEAD_DIM), hp, w_out.astype(BF16))


def _fox_consts():
    pq = np.zeros((HEADS, LANES, 4 * HEADS), np.float32)
    pk = np.zeros((HEADS, 4 * HEADS, LANES), np.float32)
    one, pad = 3 * HEADS, 3 * HEADS + 1
    for h in range(HEADS):
        for part in range(3):
            pq[h, part, part * HEADS + h] = 1.0
            pq[h, 3 + part, one] = 1.0
            pk[h, one, part] = 1.0
            pk[h, part * HEADS + h, 3 + part] = -1.0
        pk[h, pad, 3] = NEG_BIG
    tri = np.triu(np.ones((ROW_TILE, ROW_TILE), np.float32))
    return jnp.asarray(pq, BF16), jnp.asarray(pk, BF16), jnp.asarray(tri, BF16)


def _fox_in_kernel(h_ref, g_ref, wq_ref, wk_ref, wv_ref, wg_ref, wf_ref, bf_ref, gq_ref, gk_ref,
                   tri_ref, pq_ref, pk_ref,
                   qqt_ref, kk_ref, vt_ref, sg_ref, qn_ref, qmax_ref, kmax_ref,
                   xn_ref, src_ref, srct_ref, carry_ref):
    i = pl.program_id(0)
    c = pl.program_id(1)
    hpt = PROJ_TILE // HEAD_DIM

    def project(xn):
        qp = jnp.dot(xn, wq_ref[...], preferred_element_type=F32)
        kp = jnp.dot(xn, wk_ref[...], preferred_element_type=F32)
        vp = jnp.dot(xn, wv_ref[...], preferred_element_type=F32)
        gp = jnp.dot(xn, wg_ref[...], preferred_element_type=F32)
        sg = _sigmoid(gp)
        scale = HEAD_DIM ** -0.5 * LOG2E
        ones = jnp.ones((HEAD_DIM, LANES), BF16)
        for hh in range(hpt):
            sl = slice(hh * HEAD_DIM, (hh + 1) * HEAD_DIM)
            head = hpt * c + hh
            q = (_rms_rows(qp[:, sl]) * (gq_ref[...] * scale)).astype(BF16)
            k = (_rms_rows(kp[:, sl]) * gk_ref[...]).astype(BF16)
            qt = q.astype(F32).T
            qqt_ref[hh, :HEAD_DIM, :] = qt.astype(BF16)
            qqt_ref[hh, HEAD_DIM:, :] = jnp.dot(
                pq_ref[hh], srct_ref[...], preferred_element_type=F32).astype(BF16)
            kk_ref[hh, :, :HEAD_DIM] = k
            kk_ref[hh, :, HEAD_DIM:] = jnp.dot(
                src_ref[...], pk_ref[hh], preferred_element_type=F32).astype(BF16)
            vt_ref[hh, :HEAD_DIM, :] = vp[:, sl].T.astype(BF16)
            vt_ref[hh, HEAD_DIM:, :] = jnp.ones((V_ROWS - HEAD_DIM, ROW_TILE), BF16)
            sg_ref[hh] = sg[:, sl].astype(BF16)
            qn = jnp.sqrt(jnp.sum(qt * qt, axis=0, keepdims=True))
            qn_ref[hh] = qn
            kf = k.astype(F32)
            kn2 = jnp.dot((kf * kf).astype(BF16), ones, preferred_element_type=F32)
            kn = jnp.sqrt(jnp.max(kn2, axis=0, keepdims=True)) * NORM_SLACK
            qmax_ref[head] = jnp.maximum(qmax_ref[head], qn)
            kmax_ref[head] = jnp.maximum(kmax_ref[head], kn)

    @pl.when((i == 0) & (c == 0))
    def _():
        carry_ref[...] = jnp.zeros_like(carry_ref)
        qmax_ref[...] = jnp.zeros_like(qmax_ref)
        kmax_ref[...] = jnp.zeros_like(kmax_ref)

    @pl.when(c == 0)
    def _():
        xn = (_rms_rows(h_ref[...]) * g_ref[...]).astype(BF16)
        xn_ref[...] = xn
        z = lax.dot_general(wf_ref[...], xn, _NT, preferred_element_type=F32) + bf_ref[...]
        pos = i * ROW_TILE + lax.broadcasted_iota(jnp.int32, (HEADS, ROW_TILE), 1)
        lf = jnp.minimum(z, 0.0) - jnp.log(1.0 + jnp.exp(-jnp.abs(z)))
        lf = jnp.where(pos >= META_PAD, lf, 0.0)
        cum = carry_ref[...]
        for part in _split3(lf):
            cum = cum + jnp.dot(part, tri_ref[...], preferred_element_type=F32)
        carry_ref[...] = cum[:, ROW_TILE - 1:ROW_TILE]
        sub = lax.broadcasted_iota(jnp.int32, (HEADS, ROW_TILE), 0)
        extra = jnp.where(sub == 0, 1.0, jnp.where((sub == 1) & (pos < META_PAD), 1.0, 0.0))
        src_t = jnp.concatenate([p.astype(F32) for p in _split3(cum * LOG2E)] + [extra], axis=0)
        srct_ref[...] = src_t.astype(BF16)
        src_ref[...] = jnp.concatenate([src_t, jnp.zeros_like(src_t)], axis=0).T[:, :4 * HEADS].astype(BF16)
        project(xn)

    @pl.when(c > 0)
    def _():
        project(xn_ref[...])


def _fox_in(hp, gain, w_in, j, b_f, qnorm, knorm):
    lp = hp.shape[0]
    nc = D_MODEL // PROJ_TILE
    hpt = PROJ_TILE // HEAD_DIM
    kdim = HEAD_DIM + LANES
    w, wf = _cast_proj_weights(w_in, j, 4 * D_MODEL)
    pq, pk, tri = _fox_consts()
    wspec = lambda s: pl.BlockSpec((D_MODEL, PROJ_TILE), lambda i, c, s=s: (0, s * nc + c))
    vec = lambda n: pl.BlockSpec((1, n), lambda i, c: (0, 0))
    return pl.pallas_call(
        _fox_in_kernel,
        grid=(lp // ROW_TILE, nc),
        in_specs=[
            pl.BlockSpec((ROW_TILE, D_MODEL), lambda i, c: (i, 0)),
            vec(D_MODEL),
            wspec(0), wspec(1), wspec(2), wspec(3),
            pl.BlockSpec((HEADS, D_MODEL), lambda i, c: (0, 0)),
            pl.BlockSpec((HEADS, 1), lambda i, c: (0, 0)), vec(HEAD_DIM), vec(HEAD_DIM),
            pl.BlockSpec((ROW_TILE, ROW_TILE), lambda i, c: (0, 0)),
            pl.BlockSpec((hpt, LANES, 4 * HEADS), lambda i, c: (c, 0, 0)),
            pl.BlockSpec((hpt, 4 * HEADS, LANES), lambda i, c: (c, 0, 0)),
        ],
        out_specs=[pl.BlockSpec((hpt, kdim, ROW_TILE), lambda i, c: (c, 0, i)),
                   pl.BlockSpec((hpt, ROW_TILE, kdim), lambda i, c: (c, i, 0)),
                   pl.BlockSpec((hpt, V_ROWS, ROW_TILE), lambda i, c: (c, 0, i)),
                   pl.BlockSpec((hpt, ROW_TILE, HEAD_DIM), lambda i, c: (c, i, 0)),
                   pl.BlockSpec((hpt, 1, ROW_TILE), lambda i, c: (c, 0, i)),
                   pl.BlockSpec((HEADS, 1, ROW_TILE), lambda i, c: (0, 0, 0)),
                   pl.BlockSpec((HEADS, 1, LANES), lambda i, c: (0, 0, 0))],
        out_shape=[jax.ShapeDtypeStruct((HEADS, kdim, lp), BF16),
                   jax.ShapeDtypeStruct((HEADS, lp, kdim), BF16),
                   jax.ShapeDtypeStruct((HEADS, V_ROWS, lp), BF16),
                   jax.ShapeDtypeStruct((HEADS, lp, HEAD_DIM), BF16),
                   jax.ShapeDtypeStruct((HEADS, 1, lp), F32),
                   jax.ShapeDtypeStruct((HEADS, 1, ROW_TILE), F32),
                   jax.ShapeDtypeStruct((HEADS, 1, LANES), F32)],
        scratch_shapes=[pltpu.VMEM((ROW_TILE, D_MODEL), BF16),
                        pltpu.VMEM((ROW_TILE, 4 * HEADS), BF16),
                        pltpu.VMEM((4 * HEADS, ROW_TILE), BF16),
                        pltpu.VMEM((HEADS, 1), F32)],
        compiler_params=_params("arbitrary", "arbitrary"),
        name="fox_in",
    )(hp, gain.reshape(1, D_MODEL), w, w, w, w, wf, b_f.reshape(HEADS, 1),
      qnorm.reshape(1, HEAD_DIM), knorm.reshape(1, HEAD_DIM), tri, pq, pk)


def _fox_attn_kernel(safe_ref, kmax_ref, qqt_ref, qn_ref, kk_ref, vt_ref, sg_ref, o_ref,
                     m_ref, acc_ref, ring_ref):
    h = pl.program_id(0)
    i = pl.program_id(1)
    qs = i * ATT_Q
    n_groups = ATT_Q // ATT_GROUP
    n_steps = i * (ATT_Q // ATT_K)

    def scores(item):
        g, ks, nk, _ = item
        gs = slice(g * ATT_GROUP, (g + 1) * ATT_GROUP)
        return jnp.dot(kk_ref[pl.ds(ks, nk), :], qqt_ref[:, gs], preferred_element_type=F32)

    def step_items(j):
        return [(g, pl.multiple_of(j * ATT_K + sum(ATT_KSPLIT[:sub]), ATT_GROUP), nk, False)
                for sub, nk in enumerate(ATT_KSPLIT) for g in range(n_groups)]

    def sweep(fixed_shift):
        acc_ref[...] = jnp.zeros_like(acc_ref)
        if fixed_shift:
            m_ref[...] = qn_ref[...] * kmax_ref[h]
        else:
            m_ref[...] = jnp.full_like(m_ref, -jnp.inf)

        def update(item, s):
            g, ks, nk, masked = item
            gs = slice(g * ATT_GROUP, (g + 1) * ATT_GROUP)
            if masked:
                r = lax.broadcasted_iota(jnp.int32, s.shape, 0)
                c = lax.broadcasted_iota(jnp.int32, s.shape, 1) + g * ATT_GROUP
                s = jnp.where(r <= c, s, NEG_BIG)
            if fixed_shift:
                p = jnp.exp2(s - m_ref[:, gs])
                acc_ref[:HEAD_DIM, gs] += jnp.dot(vt_ref[:HEAD_DIM, pl.ds(ks, nk)], p.astype(BF16),
                                                  preferred_element_type=F32)
                acc_ref[HEAD_DIM:HEAD_DIM + 1, gs] += jnp.sum(p, axis=0, keepdims=True)
            else:
                vt = vt_ref[:, pl.ds(ks, nk)]
                m_prev = m_ref[:, gs]
                m_new = jnp.maximum(m_prev, jnp.max(s, axis=0, keepdims=True))
                alpha = jnp.exp2(m_prev - m_new)
                p = jnp.exp2((s - m_new).astype(BF16))
                m_ref[:, gs] = m_new
                acc_ref[:, gs] = alpha * acc_ref[:, gs] + jnp.dot(vt, p, preferred_element_type=F32)

        def run(items, first, ahead):
            pending = list(first) if first else [scores(item) for item in items[:ATT_AHEAD]]
            for n, item in enumerate(items):
                if n + ATT_AHEAD < len(items):
                    pending.append(scores(items[n + ATT_AHEAD]))
                else:
                    a = n + ATT_AHEAD - len(items)
                    ring_ref[a] = scores(ahead[a])
                update(item, pending.pop(0))

        run([(g, pl.multiple_of(qs, ATT_Q), (g + 1) * ATT_GROUP, True) for g in range(n_groups)],
            None, step_items(0)[:ATT_AHEAD])

        def body(j, carry):
            run(step_items(j), [ring_ref[a] for a in range(ATT_AHEAD)],
                step_items(jnp.minimum(j + 1, n_steps - 1))[:ATT_AHEAD])
            return carry

        lax.fori_loop(0, n_steps, body, 0)

    @pl.when(safe_ref[h] == 1)
    def _():
        sweep(True)

    @pl.when(safe_ref[h] != 1)
    def _():
        sweep(False)

    o = (acc_ref[:HEAD_DIM, :] / acc_ref[HEAD_DIM:HEAD_DIM + 1, :]).T * sg_ref[...].astype(F32)
    row = qs + lax.broadcasted_iota(jnp.int32, o.shape, 0)
    o_ref[...] = jnp.where(row >= META_PAD, o, 0.0).astype(BF16)


def _fox_attn(qqt, qn, qmax, kmax, kk, vt, sg):
    lp = kk.shape[1]
    kdim = kk.shape[2]
    qmax, kmax = jnp.max(qmax, axis=(1, 2)), jnp.max(kmax, axis=(1, 2))
    safe = (2.0 * qmax * kmax <= SAFE_SHIFT_RANGE).astype(jnp.int32)
    grid_spec = pltpu.PrefetchScalarGridSpec(
        num_scalar_prefetch=2,
        grid=(HEADS, lp // ATT_Q),
        in_specs=[pl.BlockSpec((None, kdim, ATT_Q), lambda h, i, *_: (h, 0, i)),
                  pl.BlockSpec((None, 1, ATT_Q), lambda h, i, *_: (h, 0, i)),
                  pl.BlockSpec((None, lp, kdim), lambda h, i, *_: (h, 0, 0)),
                  pl.BlockSpec((None, V_ROWS, lp), lambda h, i, *_: (h, 0, 0)),
                  pl.BlockSpec((None, ATT_Q, HEAD_DIM), lambda h, i, *_: (h, i, 0))],
        out_specs=pl.BlockSpec((ATT_Q, HEAD_DIM), lambda h, i, *_: (i, h)),
        scratch_shapes=[pltpu.VMEM((1, ATT_Q), F32), pltpu.VMEM((V_ROWS, ATT_Q), F32),
                        pltpu.VMEM((ATT_AHEAD, ATT_KSPLIT[0], ATT_GROUP), F32)])
    return pl.pallas_call(
        _fox_attn_kernel,
        grid_spec=grid_spec,
        out_shape=jax.ShapeDtypeStruct((lp, D_MODEL), BF16),
        compiler_params=_params("parallel", "arbitrary"),
        name="fox_attn",
    )(safe, kmax, qqt, qn, kk, vt, sg)


def kernel(x, meta_tokens, norm_g, ffn_w_gu, ffn_w_down, lb_logits, hg_w_in, hg_w_out, hg_onorm,
           fox_w_in, fox_b_f, fox_w_out, fox_qnorm, fox_knorm):
    assert x.shape[0] == 1 and x.shape[2] == D_MODEL
    seq = x.shape[1]
    real = META_PAD + N_META + seq
    lp = -(-real // ATT_Q) * ATT_Q
    assert seq % BLOCK == 0
    head = jnp.concatenate([jnp.zeros((META_PAD, D_MODEL), F32), meta_tokens.astype(F32)], axis=0)
    w_gu = ffn_w_gu.reshape(2 * DEPTH, D_MODEL, 2 * FFN_DIM)
    wgu_first, wd = _cast_ffn_weights(w_gu, ffn_w_down.reshape(2 * DEPTH, FFN_DIM, D_MODEL), 1)
    wgu_rest = None
    gate_up = lambda a: (wgu_first, 0) if a == 0 else (wgu_rest, a - 1)
    hp = x[0]
    for layer in range(DEPTH):
        hp = _ffn(hp, head, norm_g[layer, 0], *gate_up(2 * layer), wd, 2 * layer,
                  -1 if layer == 0 else 0, lp)
        j = layer // 2
        if layer % 2 == 0:
            q, k, lf, v, sg, cast = _hgrn_in(hp, norm_g[layer, 1], hg_w_in, j, lb_logits, layer,
                                             w_gu, 1)
            wgu_rest = cast if wgu_rest is None else wgu_rest
            hp = _hgrn_rec(q, k, lf, v, sg, hg_onorm[j], hp, hg_w_out[j])
        else:
            qqt, kk, vt, sg, qn, qmax, kmax = _fox_in(hp, norm_g[layer, 1], fox_w_in, j, fox_b_f[j],
                                                      fox_qnorm[j], fox_knorm[j])
            og = _fox_attn(qqt, qn, qmax, kmax, kk, vt, sg)
            hp = _out_proj(hp, og, fox_w_out[j])
        last = layer == DEPTH - 1
        hp = _ffn(hp, head, norm_g[layer, 2], *gate_up(2 * layer + 1), wd, 2 * layer + 1,
                  1 if last else 0, seq if last else lp)
    return hp[None]
```

```python
import functools

import numpy as np
import jax
import jax.numpy as jnp
from jax import lax
from jax.experimental import pallas as pl
from jax.experimental.pallas import tpu as pltpu

F32 = jnp.float32
BF16 = jnp.bfloat16

D_MODEL = 2048
DEPTH = 2
N_META = 16
BLOCK = 128
META_PAD = (-N_META) % BLOCK
RMS_EPS = 1e-6
HEADS = 16
HEAD_DIM = 128
FFN_DIM = 5504
NEG_BIG = -1e30
LOG2E = 1.4426950408889634

LANES = 128
ROW_TILE = 640
FFN_TILE = 512
FFN_PAD = -(-FFN_DIM // FFN_TILE) * FFN_TILE
PROJ_TILE = 512
CAST_ROWS = 256
SIDE_ROWS = 64
ATT_Q = 1280
ATT_K = 1280
ATT_KSPLIT = (512, 768)
ATT_GROUP = 256
ATT_AHEAD = 2
V_ROWS = HEAD_DIM + 16
NORM_SLACK = 1.01
SAFE_SHIFT_RANGE = 80.0
VMEM_LIMIT = 56 * 1024 * 1024

_NT = (((1,), (1,)), ((), ()))


def _params(*sem):
    return pltpu.CompilerParams(dimension_semantics=sem, vmem_limit_bytes=VMEM_LIMIT)


def _sigmoid(x):
    return 1.0 / (1.0 + jnp.exp(-x))


def _rms_rows(x):
    return x * lax.rsqrt(jnp.mean(x * x, axis=-1, keepdims=True) + RMS_EPS)


def _split3(x):
    hi = x.astype(BF16)
    r1 = x - hi.astype(F32)
    mid = r1.astype(BF16)
    lo = (r1 - mid.astype(F32)).astype(BF16)
    return hi, mid, lo


def _cast_wgu_kernel(w_ref, o_ref):
    zeros = jnp.zeros((o_ref.shape[0], FFN_PAD - FFN_DIM), BF16)
    o_ref[:, :FFN_DIM] = w_ref[:, :FFN_DIM].astype(BF16)
    o_ref[:, FFN_DIM:FFN_PAD] = zeros
    o_ref[:, FFN_PAD:FFN_PAD + FFN_DIM] = w_ref[:, FFN_DIM:].astype(BF16)
    o_ref[:, FFN_PAD + FFN_DIM:] = zeros


def _cast_wd_kernel(w_ref, o_ref):
    row = pl.program_id(1) * FFN_TILE + lax.broadcasted_iota(jnp.int32, (FFN_TILE, D_MODEL), 0)
    o_ref[...] = jnp.where(row < FFN_DIM, w_ref[...], 0.0).astype(BF16)


def _cast_kernel(w_ref, o_ref):
    o_ref[...] = w_ref[...].astype(BF16)


def _cast_transposed_kernel(wt_ref, tail_ref, o_ref, t_ref):
    o_ref[...] = wt_ref[...].T.astype(BF16)
    t_ref[...] = tail_ref[...].astype(BF16)


def _cast_ffn_weights(w_gu, w_down, n_gu):
    n = w_gu.shape[0]
    wgu = pl.pallas_call(
        _cast_wgu_kernel,
        grid=(n_gu, D_MODEL // CAST_ROWS),
        in_specs=[pl.BlockSpec((None, CAST_ROWS, 2 * FFN_DIM), lambda a, r: (a, r, 0))],
        out_specs=pl.BlockSpec((None, CAST_ROWS, 2 * FFN_PAD), lambda a, r: (a, r, 0)),
        out_shape=jax.ShapeDtypeStruct((n_gu, D_MODEL, 2 * FFN_PAD), BF16),
        compiler_params=_params("parallel", "parallel"),
        name="cast_wgu",
    )(w_gu)
    wd = pl.pallas_call(
        _cast_wd_kernel,
        grid=(n, FFN_PAD // FFN_TILE),
        in_specs=[pl.BlockSpec((None, FFN_TILE, D_MODEL), lambda a, r: (a, r, 0))],
        out_specs=pl.BlockSpec((None, FFN_TILE, D_MODEL), lambda a, r: (a, r, 0)),
        out_shape=jax.ShapeDtypeStruct((n, FFN_PAD, D_MODEL), BF16),
        compiler_params=_params("parallel", "parallel"),
        name="cast_wd",
    )(w_down)
    return wgu, wd


def _cast_proj_weights(w, j, n_main):
    width = w.shape[2]
    main = jax.ShapeDtypeStruct((D_MODEL, n_main), BF16)
    if width == n_main:
        return pl.pallas_call(
            _cast_kernel, grid=(D_MODEL // CAST_ROWS,),
            in_specs=[pl.BlockSpec((None, CAST_ROWS, width), lambda r: (j, r, 0))],
            out_specs=pl.BlockSpec((CAST_ROWS, n_main), lambda r: (r, 0)),
            out_shape=main, compiler_params=_params("parallel"), name="cast_proj")(w)
    tail = width - n_main
    assert n_main % tail == 0
    wt = jnp.swapaxes(w, 1, 2)
    return pl.pallas_call(
        _cast_transposed_kernel, grid=(n_main // CAST_ROWS,),
        in_specs=[pl.BlockSpec((None, CAST_ROWS, D_MODEL), lambda r: (j, r, 0)),
                  pl.BlockSpec((None, tail, D_MODEL), lambda r: (j, n_main // tail, 0))],
        out_specs=[pl.BlockSpec((D_MODEL, CAST_ROWS), lambda r: (0, r)),
                   pl.BlockSpec((tail, D_MODEL), lambda r: (0, 0))],
        out_shape=[main, jax.ShapeDtypeStruct((tail, D_MODEL), BF16)],
        compiler_params=_params("arbitrary"), name="cast_proj_t")(wt, wt)


ROW_CHUNKS = ROW_TILE // BLOCK


def _ffn_kernel(*refs, shift, n_src, n_tiles):
    n_in = ROW_CHUNKS if shift else 1
    chunk_refs = refs[:n_in]
    head_ref, g_ref, wg_ref, wu_ref, wd_ref, o_ref, xn_ref = refs[n_in:]

    j = pl.program_id(1)

    def step(xn):
        a = jnp.dot(xn, wg_ref[...], preferred_element_type=F32)
        b = jnp.dot(xn, wu_ref[...], preferred_element_type=F32)
        act = (a * (0.5 * _sigmoid(a)) * b).astype(BF16)
        return jnp.dot(act, wd_ref[...], preferred_element_type=F32)

    @pl.when(j == 0)
    def _():
        parts = []
        for u, ref in enumerate(chunk_refs):
            idx = ROW_CHUNKS * pl.program_id(0) + u + shift
            blk = ref[...]
            if shift < 0:
                if u + shift < 0:
                    blk = jnp.where(idx < 0, head_ref[...], blk)
                if ROW_CHUNKS * (n_tiles - 1) + u + shift >= n_src:
                    blk = jnp.where(idx >= n_src, 0.0, blk)
            parts.append(blk)
        x = jnp.concatenate(parts, axis=0) if shift else parts[0]
        xn = (_rms_rows(x) * g_ref[...]).astype(BF16)
        xn_ref[...] = xn
        o_ref[...] = x + step(xn)

    @pl.when(j > 0)
    def _():
        o_ref[...] += step(xn_ref[...])


def _ffn(src, head, gain, wgu, ag, wd, a, shift, out_rows):
    n_src = src.shape[0] // BLOCK
    nf = FFN_PAD // FFN_TILE
    chunk = lambda u: pl.BlockSpec(
        (BLOCK, D_MODEL), lambda i, j: (jnp.clip(ROW_CHUNKS * i + u + shift, 0, n_src - 1), 0))
    row_specs = ([chunk(u) for u in range(ROW_CHUNKS)] if shift else
                 [pl.BlockSpec((ROW_TILE, D_MODEL), lambda i, j: (i, 0))])
    return pl.pallas_call(
        functools.partial(_ffn_kernel, shift=shift, n_src=n_src,
                          n_tiles=pl.cdiv(out_rows, ROW_TILE)),
        grid=(pl.cdiv(out_rows, ROW_TILE), nf),
        in_specs=row_specs + [
            pl.BlockSpec((BLOCK, D_MODEL), lambda i, j: (0, 0)),
            pl.BlockSpec((1, D_MODEL), lambda i, j: (0, 0)),
            pl.BlockSpec((None, D_MODEL, FFN_TILE), lambda i, j: (ag, 0, j)),
            pl.BlockSpec((None, D_MODEL, FFN_TILE), lambda i, j: (ag, 0, j + nf)),
            pl.BlockSpec((None, FFN_TILE, D_MODEL), lambda i, j: (a, j, 0)),
        ],
        out_specs=pl.BlockSpec((ROW_TILE, D_MODEL), lambda i, j: (i, 0)),
        out_shape=jax.ShapeDtypeStruct((out_rows, D_MODEL), F32),
        scratch_shapes=[pltpu.VMEM((ROW_TILE, D_MODEL), BF16)],
        compiler_params=_params("parallel", "arbitrary"),
        name="ffn",
    )(*([src] * len(row_specs)), head, gain.reshape(1, D_MODEL), wgu, wgu, wd)


def _out_proj_kernel(h_ref, o_ref, w_ref, out_ref):
    out_ref[...] = h_ref[...] + jnp.dot(o_ref[...], w_ref[...], preferred_element_type=F32)


def _out_proj(hp, og, w_out):
    lp = hp.shape[0]
    return pl.pallas_call(
        _out_proj_kernel,
        grid=(lp // ROW_TILE,),
        in_specs=[
            pl.BlockSpec((ROW_TILE, D_MODEL), lambda i: (i, 0)),
            pl.BlockSpec((ROW_TILE, D_MODEL), lambda i: (i, 0)),
            pl.BlockSpec((D_MODEL, D_MODEL), lambda i: (0, 0)),
        ],
        out_specs=pl.BlockSpec((ROW_TILE, D_MODEL), lambda i: (i, 0)),
        out_shape=jax.ShapeDtypeStruct((lp, D_MODEL), F32),
        compiler_params=_params("parallel"),
        name="out_proj",
    )(hp, og, w_out.astype(BF16))


def _hgrn_in_kernel(h_ref, g_ref, wq_ref, wf_ref, wi_ref, wg_ref, lbl_ref, side_ref,
                    q_ref, k_ref, lf_ref, v_ref, sg_ref, side_out_ref, xn_ref, *, layer):
    def project(xn):
        _cast_wgu_kernel(side_ref, side_out_ref)
        qp = jnp.dot(xn, wq_ref[...], preferred_element_type=F32)
        fp = jnp.dot(xn, wf_ref[...], preferred_element_type=F32)
        ip = jnp.dot(xn, wi_ref[...], preferred_element_type=F32)
        gp = jnp.dot(xn, wg_ref[...], preferred_element_type=F32)
        lg = lbl_ref[...]
        e = jnp.exp(lg - jnp.max(lg, axis=0, keepdims=True))
        lb = jnp.sum(e[:layer + 1], axis=0, keepdims=True) / jnp.sum(e, axis=0, keepdims=True)
        q = qp * _sigmoid(qp)
        fg = lb + (1.0 - lb) * _sigmoid(fp)
        k = 1.0 - fg
        lf = jnp.log(fg) * LOG2E
        sg = _sigmoid(gp)
        for hh in range(PROJ_TILE // HEAD_DIM):
            sl = slice(hh * HEAD_DIM, (hh + 1) * HEAD_DIM)
            q_ref[hh] = q[:, sl].astype(BF16)
            k_ref[hh] = k[:, sl].astype(BF16)
            lf_ref[hh] = lf[:, sl]
            v_ref[hh] = ip[:, sl].astype(BF16)
            sg_ref[hh] = sg[:, sl].astype(BF16)

    @pl.when(pl.program_id(1) == 0)
    def _():
        xn = (_rms_rows(h_ref[...]) * g_ref[...]).astype(BF16)
        xn_ref[...] = xn
        project(xn)

    @pl.when(pl.program_id(1) > 0)
    def _():
        project(xn_ref[...])


def _hgrn_in(hp, gain, w_in, j, lb_logits, layer, w_gu, n_done):
    lp = hp.shape[0]
    nc = D_MODEL // PROJ_TILE
    hpt = PROJ_TILE // HEAD_DIM
    per_w = D_MODEL // SIDE_ROWS
    n_side = (w_gu.shape[0] - n_done) * per_w
    assert n_side <= (lp // ROW_TILE) * nc
    side_block = lambda i, c: jnp.minimum(i * nc + c, n_side - 1)
    w = _cast_proj_weights(w_in, j, 4 * D_MODEL)
    wspec = lambda s: pl.BlockSpec((D_MODEL, PROJ_TILE), lambda i, c, s=s: (0, s * nc + c))
    hm_spec = pl.BlockSpec((hpt, ROW_TILE, HEAD_DIM), lambda i, c: (c, i, 0))
    hm = lambda dt: jax.ShapeDtypeStruct((HEADS, lp, HEAD_DIM), dt)
    return pl.pallas_call(
        functools.partial(_hgrn_in_kernel, layer=layer),
        grid=(lp // ROW_TILE, nc),
        in_specs=[
            pl.BlockSpec((ROW_TILE, D_MODEL), lambda i, c: (i, 0)),
            pl.BlockSpec((1, D_MODEL), lambda i, c: (0, 0)),
            wspec(0), wspec(1), wspec(2), wspec(3),
            pl.BlockSpec((DEPTH + 1, PROJ_TILE), lambda i, c: (0, c)),
            pl.BlockSpec((None, SIDE_ROWS, 2 * FFN_DIM),
                         lambda i, c: (n_done + side_block(i, c) // per_w, side_block(i, c) % per_w, 0)),
        ],
        out_specs=[hm_spec] * 5 + [
            pl.BlockSpec((None, SIDE_ROWS, 2 * FFN_PAD),
                         lambda i, c: (side_block(i, c) // per_w, side_block(i, c) % per_w, 0))],
        out_shape=[hm(BF16), hm(BF16), hm(F32), hm(BF16), hm(BF16),
                   jax.ShapeDtypeStruct((w_gu.shape[0] - n_done, D_MODEL, 2 * FFN_PAD), BF16)],
        scratch_shapes=[pltpu.VMEM((ROW_TILE, D_MODEL), BF16)],
        compiler_params=_params("arbitrary", "arbitrary"),
        name="hgrn_in",
    )(hp, gain.reshape(1, D_MODEL), w, w, w, w, lb_logits, w_gu)


_ROW_LEVELS = (64, 32, 16, 8)
_MM_LEVELS = (4, 2)
FAST_BLOCKS = (64, 32)
FAST_RANGE = 90.0
OUT_SLICES = 8


def _hgrn_consts():
    t = np.arange(BLOCK)[:, None]
    j = np.arange(BLOCK)[None, :]
    blocks = [j <= t]
    for b in _MM_LEVELS:
        ref = 2 * b * (t // (2 * b)) + b - 1
        odd = (t // b) % 2 == 1
        blocks.append(np.where(odd, (j > ref) & (j <= t), (j > t) & (j <= ref)))
    g_all = np.concatenate(blocks, axis=0).astype(np.float32)
    masks = []
    for b in _ROW_LEVELS + _MM_LEVELS + (1,):
        masks.append((t // (2 * b) == j // (2 * b)) & ((t // b) % 2 == 1) & ((j // b) % 2 == 0))
    masks.append(t == j)
    masks += [(t // fb == j // fb) & (j <= t) for fb in FAST_BLOCKS]
    return jnp.asarray(g_all, BF16), jnp.asarray(np.stack(masks).astype(np.float32))


def _hgrn_rec_kernel(q_ref, k_ref, lf_ref, v_ref, sg_ref, gmat_ref, mask_ref, on_ref, h_ref, w_ref,
                     out_ref, st_ref, og_ref):
    c = pl.program_id(0)

    @pl.when(c == 0)
    def _():
        st_ref[...] = jnp.zeros_like(st_ref)

    def boundary_exponent(b, size):
        parts = []
        for s in range(0, BLOCK, 2 * size):
            r = b[s + size - 1:s + size]
            parts += [r - b[s:s + size], b[s + size:s + 2 * size] - r]
        return jnp.concatenate(parts, axis=0)

    def middle_exponents(b, size):
        dq, dk = [], []
        for s in range(0, BLOCK, size):
            m = b[s + size // 2 - 1:s + size // 2]
            dq.append(b[s:s + size] - m)
            dk.append(m - b[s:s + size])
        return jnp.concatenate(dq, axis=0), jnp.concatenate(dk, axis=0)

    n_row, n_mm = len(_ROW_LEVELS), len(_MM_LEVELS)
    n_masks = n_row + n_mm + 2
    width = D_MODEL // OUT_SLICES

    def step(project, fast):
        hs = range(HEADS)
        n_coarse = None if fast is None else _ROW_LEVELS.index(FAST_BLOCKS[fast]) + 1
        n_lv = n_masks if fast is None else n_coarse + 1
        g_rows = BLOCK * (1 + n_mm) if fast is None else BLOCK
        lf = [lf_ref[h] for h in hs]
        q = [q_ref[h] for h in hs]
        k = [k_ref[h] for h in hs]
        x3 = [jnp.dot(gmat_ref[:g_rows, :], jnp.concatenate(_split3(l), axis=1),
                      preferred_element_type=F32) for l in lf]
        x = [t[:, :HEAD_DIM] + t[:, HEAD_DIM:2 * HEAD_DIM] + t[:, 2 * HEAD_DIM:] for t in x3]
        b = [t[:BLOCK] for t in x]
        b_last = [t[BLOCK - 1:BLOCK] for t in b]
        att = [jnp.zeros((BLOCK, BLOCK), F32) for _ in hs]
        for lv in range(n_lv):
            for u in hs:
                if lv == n_coarse:
                    dq, dk = middle_exponents(b[u], FAST_BLOCKS[fast])
                    a = lax.dot_general(q[u] * jnp.exp2(dq).astype(BF16),
                                        k[u] * jnp.exp2(dk).astype(BF16), _NT,
                                        preferred_element_type=F32)
                    att[u] = jnp.where(mask_ref[n_masks + fast] > 0.0, a, att[u])
                    continue
                if lv < n_row:
                    el = jnp.exp2(boundary_exponent(b[u], _ROW_LEVELS[lv])).astype(BF16)
                    qt, kt = q[u] * el, k[u] * el
                elif lv < n_row + n_mm:
                    m = lv - n_row + 1
                    el = jnp.exp2(x[u][m * BLOCK:(m + 1) * BLOCK]).astype(BF16)
                    qt, kt = q[u] * el, k[u] * el
                elif lv == n_row + n_mm:
                    qt, kt = q[u] * jnp.exp2(lf[u]).astype(BF16), k[u]
                else:
                    qt, kt = q[u], k[u]
                a = lax.dot_general(qt, kt, _NT, preferred_element_type=F32)
                att[u] = att[u] + mask_ref[lv] * a
            if project:
                for piece in range(lv * OUT_SLICES // n_lv, (lv + 1) * OUT_SLICES // n_lv):
                    sl = slice(piece * width, (piece + 1) * width)
                    out_ref[:, sl] = h_ref[:, sl] + jnp.dot(og_ref[...], w_ref[:, sl],
                                                            preferred_element_type=F32)
        rows = pl.ds(pl.multiple_of((c % 2) * BLOCK, BLOCK), BLOCK)
        for h in hs:
            qb = q[h] * jnp.exp2(b[h]).astype(BF16)
            kd = k[h] * jnp.exp2(b_last[h] - b[h]).astype(BF16)
            vt = v_ref[h].astype(F32).T.astype(BF16)
            st = st_ref[h]
            lhs = jnp.concatenate([att[h].astype(BF16), qb], axis=1)
            rhs = jnp.concatenate([vt, st.astype(BF16)], axis=1)
            o = lax.dot_general(lhs, rhs, _NT, preferred_element_type=F32)
            st_ref[h] = st * jnp.exp2(b_last[h]) + jnp.dot(vt, kd, preferred_element_type=F32)
            og = _rms_rows(o) * on_ref[...] * sg_ref[h].astype(F32)
            og_ref[rows, h * HEAD_DIM:(h + 1) * HEAD_DIM] = og.astype(BF16)

    small = min(FAST_BLOCKS)
    worst = [None] * len(FAST_BLOCKS)
    for h in range(HEADS):
        sums = jnp.sum(lf_ref[h].reshape(BLOCK // small, small, HEAD_DIM), axis=1)
        for n, size in enumerate(FAST_BLOCKS):
            per = size // small
            for g in range(BLOCK // size):
                total = sums[g * per:g * per + 1]
                for r in range(1, per):
                    total = total + sums[g * per + r:g * per + r + 1]
                worst[n] = total if worst[n] is None else jnp.minimum(worst[n], total)
    ok = [jnp.min(w) >= -FAST_RANGE for w in worst]
    project_now = (c % 2 == 0) & (c > 0)
    earlier = jnp.bool_(False)
    for fast in list(range(len(FAST_BLOCKS))) + [None]:
        chosen = jnp.logical_not(earlier) & (ok[fast] if fast is not None else True)
        for project in (True, False):
            pl.when(chosen & (project_now if project else jnp.logical_not(project_now)))(
                functools.partial(step, project, fast))
        earlier = earlier | chosen


def _hgrn_rec(q, k, lf, v, sg, onorm, hp, w_out):
    lp = q.shape[1]
    n = lp // BLOCK
    assert n % 2 == 0
    gmat, masks = _hgrn_consts()
    hm_spec = pl.BlockSpec((HEADS, BLOCK, HEAD_DIM), lambda c: (0, jnp.minimum(c, n - 1), 0))
    row_spec = pl.BlockSpec((2 * BLOCK, D_MODEL), lambda c: (jnp.maximum(c // 2 - 1, 0), 0))
    return pl.pallas_call(
        _hgrn_rec_kernel,
        grid=(n + 1,),
        in_specs=[hm_spec] * 5 + [
            pl.BlockSpec(gmat.shape, lambda c: (0, 0)),
            pl.BlockSpec(masks.shape, lambda c: (0, 0, 0)),
            pl.BlockSpec((1, HEAD_DIM), lambda c: (0, 0)),
            row_spec,
            pl.BlockSpec((D_MODEL, D_MODEL), lambda c: (0, 0)),
        ],
        out_specs=row_spec,
        out_shape=jax.ShapeDtypeStruct((lp, D_MODEL), F32),
        scratch_shapes=[pltpu.VMEM((HEADS, HEAD_DIM, HEAD_DIM), F32),
                        pltpu.VMEM((2 * BLOCK, D_MODEL), BF16)],
        compiler_params=_params("arbitrary"),
        name="hgrn_rec",
    )(q, k, lf, v, sg, gmat, masks, onorm.reshape(1, HEAD_DIM), hp, w_out.astype(BF16))


def _fox_consts():
    pq = np.zeros((HEADS, LANES, 4 * HEADS), np.float32)
    pk = np.zeros((HEADS, 4 * HEADS, LANES), np.float32)
    one, pad = 3 * HEADS, 3 * HEADS + 1
    for h in range(HEADS):
        for part in range(3):
            pq[h, part, part * HEADS + h] = 1.0
            pq[h, 3 + part, one] = 1.0
            pk[h, one, part] = 1.0
            pk[h, part * HEADS + h, 3 + part] = -1.0
        pk[h, pad, 3] = NEG_BIG
    tri = np.triu(np.ones((ROW_TILE, ROW_TILE), np.float32))
    return jnp.asarray(pq, BF16), jnp.asarray(pk, BF16), jnp.asarray(tri, BF16)


def _fox_in_kernel(h_ref, g_ref, wq_ref, wk_ref, wv_ref, wg_ref, wf_ref, bf_ref, gq_ref, gk_ref,
                   tri_ref, pq_ref, pk_ref,
                   qqt_ref, kk_ref, vt_ref, sg_ref, qn_ref, qmax_ref, kmax_ref,
                   xn_ref, src_ref, srct_ref, carry_ref):
    i = pl.program_id(0)
    c = pl.program_id(1)
    hpt = PROJ_TILE // HEAD_DIM

    def project(xn):
        qp = jnp.dot(xn, wq_ref[...], preferred_element_type=F32)
        kp = jnp.dot(xn, wk_ref[...], preferred_element_type=F32)
        vp = jnp.dot(xn, wv_ref[...], preferred_element_type=F32)
        gp = jnp.dot(xn, wg_ref[...], preferred_element_type=F32)
        sg = _sigmoid(gp)
        scale = HEAD_DIM ** -0.5 * LOG2E
        ones = jnp.ones((HEAD_DIM, LANES), BF16)
        for hh in range(hpt):
            sl = slice(hh * HEAD_DIM, (hh + 1) * HEAD_DIM)
            head = hpt * c + hh
            q = (_rms_rows(qp[:, sl]) * (gq_ref[...] * scale)).astype(BF16)
            k = (_rms_rows(kp[:, sl]) * gk_ref[...]).astype(BF16)
            qt = q.astype(F32).T
            qqt_ref[hh, :HEAD_DIM, :] = qt.astype(BF16)
            qqt_ref[hh, HEAD_DIM:, :] = jnp.dot(
                pq_ref[hh], srct_ref[...], preferred_element_type=F32).astype(BF16)
            kk_ref[hh, :, :HEAD_DIM] = k
            kk_ref[hh, :, HEAD_DIM:] = jnp.dot(
                src_ref[...], pk_ref[hh], preferred_element_type=F32).astype(BF16)
            vt_ref[hh, :HEAD_DIM, :] = vp[:, sl].T.astype(BF16)
            vt_ref[hh, HEAD_DIM:, :] = jnp.ones((V_ROWS - HEAD_DIM, ROW_TILE), BF16)
            sg_ref[hh] = sg[:, sl].astype(BF16)
            qn = jnp.sqrt(jnp.sum(qt * qt, axis=0, keepdims=True))
            qn_ref[hh] = qn
            kf = k.astype(F32)
            kn2 = jnp.dot((kf * kf).astype(BF16), ones, preferred_element_type=F32)
            kn = jnp.sqrt(jnp.max(kn2, axis=0, keepdims=True)) * NORM_SLACK
            qmax_ref[head] = jnp.maximum(qmax_ref[head], qn)
            kmax_ref[head] = jnp.maximum(kmax_ref[head], kn)

    @pl.when((i == 0) & (c == 0))
    def _():
        carry_ref[...] = jnp.zeros_like(carry_ref)
        qmax_ref[...] = jnp.zeros_like(qmax_ref)
        kmax_ref[...] = jnp.zeros_like(kmax_ref)

    @pl.when(c == 0)
    def _():
        xn = (_rms_rows(h_ref[...]) * g_ref[...]).astype(BF16)
        xn_ref[...] = xn
        z = lax.dot_general(wf_ref[...], xn, _NT, preferred_element_type=F32) + bf_ref[...]
        pos = i * ROW_TILE + lax.broadcasted_iota(jnp.int32, (HEADS, ROW_TILE), 1)
        lf = jnp.minimum(z, 0.0) - jnp.log(1.0 + jnp.exp(-jnp.abs(z)))
        lf = jnp.where(pos >= META_PAD, lf, 0.0)
        cum = carry_ref[...]
        for part in _split3(lf):
            cum = cum + jnp.dot(part, tri_ref[...], preferred_element_type=F32)
        carry_ref[...] = cum[:, ROW_TILE - 1:ROW_TILE]
        sub = lax.broadcasted_iota(jnp.int32, (HEADS, ROW_TILE), 0)
        extra = jnp.where(sub == 0, 1.0, jnp.where((sub == 1) & (pos < META_PAD), 1.0, 0.0))
        src_t = jnp.concatenate([p.astype(F32) for p in _split3(cum * LOG2E)] + [extra], axis=0)
        srct_ref[...] = src_t.astype(BF16)
        src_ref[...] = jnp.concatenate([src_t, jnp.zeros_like(src_t)], axis=0).T[:, :4 * HEADS].astype(BF16)
        project(xn)

    @pl.when(c > 0)
    def _():
        project(xn_ref[...])


def _fox_in(hp, gain, w_in, j, b_f, qnorm, knorm):
    lp = hp.shape[0]
    nc = D_MODEL // PROJ_TILE
    hpt = PROJ_TILE // HEAD_DIM
    kdim = HEAD_DIM + LANES
    w, wf = _cast_proj_weights(w_in, j, 4 * D_MODEL)
    pq, pk, tri = _fox_consts()
    wspec = lambda s: pl.BlockSpec((D_MODEL, PROJ_TILE), lambda i, c, s=s: (0, s * nc + c))
    vec = lambda n: pl.BlockSpec((1, n), lambda i, c: (0, 0))
    return pl.pallas_call(
        _fox_in_kernel,
        grid=(lp // ROW_TILE, nc),
        in_specs=[
            pl.BlockSpec((ROW_TILE, D_MODEL), lambda i, c: (i, 0)),
            vec(D_MODEL),
            wspec(0), wspec(1), wspec(2), wspec(3),
            pl.BlockSpec((HEADS, D_MODEL), lambda i, c: (0, 0)),
            pl.BlockSpec((HEADS, 1), lambda i, c: (0, 0)), vec(HEAD_DIM), vec(HEAD_DIM),
            pl.BlockSpec((ROW_TILE, ROW_TILE), lambda i, c: (0, 0)),
            pl.BlockSpec((hpt, LANES, 4 * HEADS), lambda i, c: (c, 0, 0)),
            pl.BlockSpec((hpt, 4 * HEADS, LANES), lambda i, c: (c, 0, 0)),
        ],
        out_specs=[pl.BlockSpec((hpt, kdim, ROW_TILE), lambda i, c: (c, 0, i)),
                   pl.BlockSpec((hpt, ROW_TILE, kdim), lambda i, c: (c, i, 0)),
                   pl.BlockSpec((hpt, V_ROWS, ROW_TILE), lambda i, c: (c, 0, i)),
                   pl.BlockSpec((hpt, ROW_TILE, HEAD_DIM), lambda i, c: (c, i, 0)),
                   pl.BlockSpec((hpt, 1, ROW_TILE), lambda i, c: (c, 0, i)),
                   pl.BlockSpec((HEADS, 1, ROW_TILE), lambda i, c: (0, 0, 0)),
                   pl.BlockSpec((HEADS, 1, LANES), lambda i, c: (0, 0, 0))],
        out_shape=[jax.ShapeDtypeStruct((HEADS, kdim, lp), BF16),
                   jax.ShapeDtypeStruct((HEADS, lp, kdim), BF16),
                   jax.ShapeDtypeStruct((HEADS, V_ROWS, lp), BF16),
                   jax.ShapeDtypeStruct((HEADS, lp, HEAD_DIM), BF16),
                   jax.ShapeDtypeStruct((HEADS, 1, lp), F32),
                   jax.ShapeDtypeStruct((HEADS, 1, ROW_TILE), F32),
                   jax.ShapeDtypeStruct((HEADS, 1, LANES), F32)],
        scratch_shapes=[pltpu.VMEM((ROW_TILE, D_MODEL), BF16),
                        pltpu.VMEM((ROW_TILE, 4 * HEADS), BF16),
                        pltpu.VMEM((4 * HEADS, ROW_TILE), BF16),
                        pltpu.VMEM((HEADS, 1), F32)],
        compiler_params=_params("arbitrary", "arbitrary"),
        name="fox_in",
    )(hp, gain.reshape(1, D_MODEL), w, w, w, w, wf, b_f.reshape(HEADS, 1),
      qnorm.reshape(1, HEAD_DIM), knorm.reshape(1, HEAD_DIM), tri, pq, pk)


def _fox_attn_kernel(safe_ref, kmax_ref, qqt_ref, qn_ref, kk_ref, vt_ref, sg_ref, o_ref,
                     m_ref, acc_ref, ring_ref):
    h = pl.program_id(0)
    i = pl.program_id(1)
    qs = i * ATT_Q
    n_groups = ATT_Q // ATT_GROUP
    n_steps = i * (ATT_Q // ATT_K)

    def scores(item):
        g, ks, nk, _ = item
        gs = slice(g * ATT_GROUP, (g + 1) * ATT_GROUP)
        return jnp.dot(kk_ref[pl.ds(ks, nk), :], qqt_ref[:, gs], preferred_element_type=F32)

    def step_items(j):
        return [(g, pl.multiple_of(j * ATT_K + sum(ATT_KSPLIT[:sub]), ATT_GROUP), nk, False)
                for sub, nk in enumerate(ATT_KSPLIT) for g in range(n_groups)]

    def sweep(fixed_shift):
        acc_ref[...] = jnp.zeros_like(acc_ref)
        if fixed_shift:
            m_ref[...] = qn_ref[...] * kmax_ref[h]
        else:
            m_ref[...] = jnp.full_like(m_ref, -jnp.inf)

        def update(item, s):
            g, ks, nk, masked = item
            gs = slice(g * ATT_GROUP, (g + 1) * ATT_GROUP)
            if masked:
                r = lax.broadcasted_iota(jnp.int32, s.shape, 0)
                c = lax.broadcasted_iota(jnp.int32, s.shape, 1) + g * ATT_GROUP
                s = jnp.where(r <= c, s, NEG_BIG)
            if fixed_shift:
                p = jnp.exp2(s - m_ref[:, gs])
                acc_ref[:HEAD_DIM, gs] += jnp.dot(vt_ref[:HEAD_DIM, pl.ds(ks, nk)], p.astype(BF16),
                                                  preferred_element_type=F32)
                acc_ref[HEAD_DIM:HEAD_DIM + 1, gs] += jnp.sum(p, axis=0, keepdims=True)
            else:
                vt = vt_ref[:, pl.ds(ks, nk)]
                m_prev = m_ref[:, gs]
                m_new = jnp.maximum(m_prev, jnp.max(s, axis=0, keepdims=True))
                alpha = jnp.exp2(m_prev - m_new)
                p = jnp.exp2((s - m_new).astype(BF16))
                m_ref[:, gs] = m_new
                acc_ref[:, gs] = alpha * acc_ref[:, gs] + jnp.dot(vt, p, preferred_element_type=F32)

        def run(items, first, ahead):
            pending = list(first) if first else [scores(item) for item in items[:ATT_AHEAD]]
            for n, item in enumerate(items):
                if n + ATT_AHEAD < len(items):
                    pending.append(scores(items[n + ATT_AHEAD]))
                else:
                    a = n + ATT_AHEAD - len(items)
                    ring_ref[a] = scores(ahead[a])
                update(item, pending.pop(0))

        run([(g, pl.multiple_of(qs, ATT_Q), (g + 1) * ATT_GROUP, True) for g in range(n_groups)],
            None, step_items(0)[:ATT_AHEAD])

        def body(j, carry):
            run(step_items(j), [ring_ref[a] for a in range(ATT_AHEAD)],
                step_items(jnp.minimum(j + 1, n_steps - 1))[:ATT_AHEAD])
            return carry

        lax.fori_loop(0, n_steps, body, 0)

    @pl.when(safe_ref[h] == 1)
    def _():
        sweep(True)

    @pl.when(safe_ref[h] != 1)
    def _():
        sweep(False)

    o = (acc_ref[:HEAD_DIM, :] / acc_ref[HEAD_DIM:HEAD_DIM + 1, :]).T * sg_ref[...].astype(F32)
    row = qs + lax.broadcasted_iota(jnp.int32, o.shape, 0)
    o_ref[...] = jnp.where(row >= META_PAD, o, 0.0).astype(BF16)


def _fox_attn(qqt, qn, qmax, kmax, kk, vt, sg):
    lp = kk.shape[1]
    kdim = kk.shape[2]
    qmax, kmax = jnp.max(qmax, axis=(1, 2)), jnp.max(kmax, axis=(1, 2))
    safe = (2.0 * qmax * kmax <= SAFE_SHIFT_RANGE).astype(jnp.int32)
    grid_spec = pltpu.PrefetchScalarGridSpec(
        num_scalar_prefetch=2,
        grid=(HEADS, lp // ATT_Q),
        in_specs=[pl.BlockSpec((None, kdim, ATT_Q), lambda h, i, *_: (h, 0, i)),
                  pl.BlockSpec((None, 1, ATT_Q), lambda h, i, *_: (h, 0, i)),
                  pl.BlockSpec((None, lp, kdim), lambda h, i, *_: (h, 0, 0)),
                  pl.BlockSpec((None, V_ROWS, lp), lambda h, i, *_: (h, 0, 0)),
                  pl.BlockSpec((None, ATT_Q, HEAD_DIM), lambda h, i, *_: (h, i, 0))],
        out_specs=pl.BlockSpec((ATT_Q, HEAD_DIM), lambda h, i, *_: (i, h)),
        scratch_shapes=[pltpu.VMEM((1, ATT_Q), F32), pltpu.VMEM((V_ROWS, ATT_Q), F32),
                        pltpu.VMEM((ATT_AHEAD, ATT_KSPLIT[0], ATT_GROUP), F32)])
    return pl.pallas_call(
        _fox_attn_kernel,
        grid_spec=grid_spec,
        out_shape=jax.ShapeDtypeStruct((lp, D_MODEL), BF16),
        compiler_params=_params("parallel", "arbitrary"),
        name="fox_attn",
    )(safe, kmax, qqt, qn, kk, vt, sg)


def kernel(x, meta_tokens, norm_g, ffn_w_gu, ffn_w_down, lb_logits, hg_w_in, hg_w_out, hg_onorm,
           fox_w_in, fox_b_f, fox_w_out, fox_qnorm, fox_knorm):
    assert x.shape[0] == 1 and x.shape[2] == D_MODEL
    seq = x.shape[1]
    real = META_PAD + N_META + seq
    lp = -(-real // ATT_Q) * ATT_Q
    assert seq % BLOCK == 0
    head = jnp.concatenate([jnp.zeros((META_PAD, D_MODEL), F32), meta_tokens.astype(F32)], axis=0)
    w_gu = ffn_w_gu.reshape(2 * DEPTH, D_MODEL, 2 * FFN_DIM)
    wgu_first, wd = _cast_ffn_weights(w_gu, ffn_w_down.reshape(2 * DEPTH, FFN_DIM, D_MODEL), 1)
    wgu_rest = None
    gate_up = lambda a: (wgu_first, 0) if a == 0 else (wgu_rest, a - 1)
    hp = x[0]
    for layer in range(DEPTH):
        hp = _ffn(hp, head, norm_g[layer, 0], *gate_up(2 * layer), wd, 2 * layer,
                  -1 if layer == 0 else 0, lp)
        j = layer // 2
        if layer % 2 == 0:
            q, k, lf, v, sg, cast = _hgrn_in(hp, norm_g[layer, 1], hg_w_in, j, lb_logits, layer,
                                             w_gu, 1)
            wgu_rest = cast if wgu_rest is None else wgu_rest
            hp = _hgrn_rec(q, k, lf, v, sg, hg_onorm[j], hp, hg_w_out[j])
        else:
            qqt, kk, vt, sg, qn, qmax, kmax = _fox_in(hp, norm_g[layer, 1], fox_w_in, j, fox_b_f[j],
                                                      fox_qnorm[j], fox_knorm[j])
            og = _fox_attn(qqt, qn, qmax, kmax, kk, vt, sg)
            hp = _out_proj(hp, og, fox_w_out[j])
        last = layer == DEPTH - 1
        hp = _ffn(hp, head, norm_g[layer, 2], *gate_up(2 * layer + 1), wd, 2 * layer + 1,
                  1 if last else 0, seq if last else lp)
    return hp[None]
```
